```python
import math
import jax, jax.numpy as jnp
from jax import lax
import numpy as np

D_MODEL = 1024
BATCH = 4
SEQ = 4096
DEPTH = 4

HEAD_DIM = 128
N_MIX_HEADS = D_MODEL // HEAD_DIM
N_DN_HEADS = N_MIX_HEADS // 2
N_FOX_HEADS = N_MIX_HEADS - N_DN_HEADS
N_SB_HEADS = N_MIX_HEADS
D_DN = N_DN_HEADS * HEAD_DIM
D_FOX = N_FOX_HEADS * HEAD_DIM
CONV_WIDTH = 4
CHUNK = 64
Q_BLOCK = 128
D_FF = 2816
N_EVEN = (DEPTH + 1) // 2
N_ODD = DEPTH // 2
EPS = 1e-6
EVEN_SPLITS = (3 * D_DN, D_DN, N_DN_HEADS, N_DN_HEADS, D_FOX, D_FOX, D_FOX, D_FOX, N_FOX_HEADS)
D_IN_EVEN = 3 * D_DN + D_DN + 2 * N_DN_HEADS + 4 * D_FOX + N_FOX_HEADS
D_IN_ODD = 3 * N_SB_HEADS * HEAD_DIM

kernel_name = 'hybrid_deltanet_fox_stickbreak_macaron'


def _split(t, sizes):
    out, off = [], 0
    for s in sizes:
        out.append(t[..., off:off + s])
        off += s
    return out


def rmsnorm(x, g):
    xf = x.astype(jnp.float32)
    y = xf * lax.rsqrt(jnp.mean(xf * xf, axis=-1, keepdims=True) + EPS)
    return (y * g.astype(jnp.float32)).astype(x.dtype)


def l2norm(t):
    tf = t.astype(jnp.float32)
    return tf * lax.rsqrt(jnp.sum(tf * tf, axis=-1, keepdims=True) + EPS)


def swiglu_ffn(x, w_gu, w_down):
    g, u = jnp.split(x @ w_gu, 2, axis=-1)
    return (jax.nn.silu(g) * u) @ w_down


def split_heads(t, n):
    b, s, _ = t.shape
    return t.reshape(b, s, n, -1).transpose(0, 2, 1, 3)


def merge_heads(t):
    b, n, s, d = t.shape
    return t.transpose(0, 2, 1, 3).reshape(b, s, n * d)


def causal_conv_silu(x, w):
    s = x.shape[1]
    xp = jnp.pad(x, ((0, 0), (CONV_WIDTH - 1, 0), (0, 0)))
    y = sum(xp[:, i:i + s, :] * w[i] for i in range(CONV_WIDTH))
    return jax.nn.silu(y)


def gated_delta_rule(q, k, v, beta, g):
    b, h, s, d = q.shape
    n = s // CHUNK
    q = q * d ** -0.5
    rc = lambda t: t.reshape(b, h, n, CHUNK, *t.shape[3:])
    q, k, v, beta, g = (rc(t) for t in (q, k, v, beta, g))
    gc = jnp.cumsum(g, axis=-1)
    tri_incl = jnp.tril(jnp.ones((CHUNK, CHUNK), bool))
    tri_strict = jnp.tril(jnp.ones((CHUNK, CHUNK), bool), -1)
    decay_mat = jnp.where(tri_incl, jnp.exp(jnp.where(tri_incl, gc[..., :, None] - gc[..., None, :], 0.0)), 0.0)
    k_beta = k * beta[..., None]
    v_beta = v * beta[..., None]
    a = jnp.where(tri_strict, jnp.einsum('bhnid,bhnjd->bhnij', k_beta, k) * decay_mat, 0.0)
    lhs = jnp.eye(CHUNK, dtype=jnp.float32) + a
    rhs = jnp.concatenate([v_beta, k_beta * jnp.exp(gc)[..., None]], axis=-1)
    sol = lax.linalg.triangular_solve(lhs, rhs, left_side=True, lower=True)
    u, w = sol[..., :d], sol[..., d:]
    attn_intra = jnp.einsum('bhnid,bhnjd->bhnij', q, k) * decay_mat
    q_dec = q * jnp.exp(gc)[..., None]
    k_dec = k * jnp.exp(gc[..., -1:] - gc)[..., None]
    g_last = jnp.exp(gc[..., -1])

    def step(state, inp):
        u_c, w_c, qd_c, kd_c, at_c, gl_c = inp
        v_new = u_c - jnp.einsum('bhcd,bhde->bhce', w_c, state)
        o = jnp.einsum('bhcd,bhde->bhce', qd_c, state) + jnp.einsum('bhij,bhje->bhie', at_c, v_new)
        state = state * gl_c[..., None, None] + jnp.einsum('bhcd,bhce->bhde', kd_c, v_new)
        return state, o

    xs = tuple(jnp.moveaxis(t, 2, 0) for t in (u, w, q_dec, k_dec, attn_intra, g_last))
    _, o = lax.scan(step, jnp.zeros((b, h, d, d), jnp.float32), xs)
    return jnp.moveaxis(o, 0, 2).reshape(b, h, s, d)


def forgetting_attention(q, k, v, log_f):
    b, h, s, d = q.shape
    nb = s // Q_BLOCK
    c = jnp.cumsum(log_f, axis=-1)
    qb = q.reshape(b, h, nb, Q_BLOCK, d).transpose(2, 0, 1, 3, 4)
    cb = c.reshape(b, h, nb, Q_BLOCK).transpose(2, 0, 1, 3)
    pos_k = jnp.arange(s)

    def block(args):
        i, q_i, c_i = args
        pos_q = i * Q_BLOCK + jnp.arange(Q_BLOCK)
        logits = jnp.einsum('bhqd,bhkd->bhqk', q_i, k).astype(jnp.float32) * d ** -0.5
        logits = logits + (c_i[..., :, None] - c[..., None, :])
        logits = jnp.where(pos_k[None, :] <= pos_q[:, None], logits, -jnp.inf)
        p = jax.nn.softmax(logits, axis=-1)
        return jnp.einsum('bhqk,bhkd->bhqd', p.astype(v.dtype), v)

    o = lax.map(block, (jnp.arange(nb), qb, cb))
    return o.transpose(1, 2, 0, 3, 4).reshape(b, h, s, d)


def stick_breaking_attention(q, k, v):
    b, h, s, d = q.shape
    nb = s // Q_BLOCK
    qb = q.reshape(b, h, nb, Q_BLOCK, d).transpose(2, 0, 1, 3, 4)
    pos_k = jnp.arange(s)

    def block(args):
        i, q_i = args
        pos_q = i * Q_BLOCK + jnp.arange(Q_BLOCK)
        z = jnp.einsum('bhqd,bhkd->bhqk', q_i, k).astype(jnp.float32) * d ** -0.5
        before = pos_k[None, :] < pos_q[:, None]
        log_1m = jnp.where(before, jax.nn.log_sigmoid(-z), 0.0)
        tail = lax.cumsum(log_1m, axis=3, reverse=True) - log_1m
        a = jnp.where(before, jnp.exp(jax.nn.log_sigmoid(z) + tail), 0.0)
        return jnp.einsum('bhqk,bhkd->bhqd', a.astype(v.dtype), v)

    o = lax.map(block, (jnp.arange(nb), qb))
    return o.transpose(1, 2, 0, 3, 4).reshape(b, h, s, d)


def deltanet_fox_mixer(h, w_in, conv_w, a_log, dt_bias, dn_norm_g, q_norm_g, k_norm_g, f_bias, w_out):
    dn_qkv, dn_gate, dn_b, dn_a, fq, fk, fv, f_gate, f_pre = _split(h @ w_in, EVEN_SPLITS)
    dq, dk, dv = jnp.split(causal_conv_silu(dn_qkv, conv_w), 3, axis=-1)
    dq = l2norm(split_heads(dq, N_DN_HEADS))
    dk = l2norm(split_heads(dk, N_DN_HEADS))
    dv = split_heads(dv, N_DN_HEADS).astype(jnp.float32)
    beta = jax.nn.sigmoid(dn_b.astype(jnp.float32)).transpose(0, 2, 1)
    g = (-jnp.exp(a_log.astype(jnp.float32)) * jax.nn.softplus(dn_a.astype(jnp.float32) + dt_bias.astype(jnp.float32))).transpose(0, 2, 1)
    o_dn = gated_delta_rule(dq, dk, dv, beta, g).astype(h.dtype)
    o_dn = merge_heads(rmsnorm(o_dn, dn_norm_g)) * jax.nn.silu(dn_gate)
    fq = rmsnorm(split_heads(fq, N_FOX_HEADS), q_norm_g)
    fk = rmsnorm(split_heads(fk, N_FOX_HEADS), k_norm_g)
    fv = split_heads(fv, N_FOX_HEADS)
    log_f = jax.nn.log_sigmoid(f_pre.astype(jnp.float32) + f_bias.astype(jnp.float32)).transpose(0, 2, 1)
    o_fox = merge_heads(forgetting_attention(fq, fk, fv, log_f)) * jax.nn.sigmoid(f_gate)
    return jnp.concatenate([o_dn, o_fox], axis=-1) @ w_out


def stick_breaking_mixer(h, w_in, w_out):
    q, k, v = jnp.split(h @ w_in, 3, axis=-1)
    o = stick_breaking_attention(split_heads(q, N_SB_HEADS), split_heads(k, N_SB_HEADS), split_heads(v, N_SB_HEADS))
    return merge_heads(o) @ w_out


def setup_inputs(seed: int = 0) -> dict:
    key = jax.random.key(seed)
    ks = jax.random.split(key, 20)
    f32 = jnp.float32

    def w(k, shape, fan_in):
        return jax.random.normal(k, shape, f32) * fan_in ** -0.5

    def gain(k, shape):
        return 1.0 + 0.02 * jax.random.normal(k, shape, f32)

    dt = jnp.exp(jax.random.uniform(ks[8], (N_EVEN, N_DN_HEADS), f32, math.log(1e-3), math.log(1e-1)))
    return {
        'x': jax.random.normal(ks[0], (BATCH, SEQ, D_MODEL), f32),
        'norm_ffn1': gain(ks[1], (DEPTH, D_MODEL)),
        'ffn1_w_gu': w(ks[2], (DEPTH, D_MODEL, 2 * D_FF), D_MODEL),
        'ffn1_w_down': w(ks[3], (DEPTH, D_FF, D_MODEL), D_FF),
        'norm_mix': gain(ks[4], (DEPTH, D_MODEL)),
        'w_in_even': w(ks[5], (N_EVEN, D_MODEL, D_IN_EVEN), D_MODEL),
        'dn_conv_w': w(ks[6], (N_EVEN, CONV_WIDTH, 3 * D_DN), CONV_WIDTH),
        'dn_a_log': jnp.log(jax.random.uniform(ks[7], (N_EVEN, N_DN_HEADS), f32, 1.0, 16.0)),
        'dn_dt_bias': dt + jnp.log(-jnp.expm1(-dt)),
        'dn_norm_g': gain(ks[9], (N_EVEN, HEAD_DIM)),
        'fox_q_norm_g': gain(ks[10], (N_EVEN, HEAD_DIM)),
        'fox_k_norm_g': gain(ks[11], (N_EVEN, HEAD_DIM)),
        'fox_f_bias': jax.random.uniform(ks[12], (N_EVEN, N_FOX_HEADS), f32, 1.0, 4.0),
        'w_out_even': w(ks[13], (N_EVEN, D_DN + D_FOX, D_MODEL), D_DN + D_FOX),
        'w_in_odd': w(ks[14], (N_ODD, D_MODEL, D_IN_ODD), D_MODEL),
        'w_out_odd': w(ks[15], (N_ODD, N_SB_HEADS * HEAD_DIM, D_MODEL), N_SB_HEADS * HEAD_DIM),
        'norm_ffn2': gain(ks[16], (DEPTH, D_MODEL)),
        'ffn2_w_gu': w(ks[17], (DEPTH, D_MODEL, 2 * D_FF), D_MODEL),
        'ffn2_w_down': w(ks[18], (DEPTH, D_FF, D_MODEL), D_FF),
    }


def reference(x, norm_ffn1, ffn1_w_gu, ffn1_w_down, norm_mix, w_in_even, dn_conv_w, dn_a_log,
              dn_dt_bias, dn_norm_g, fox_q_norm_g, fox_k_norm_g, fox_f_bias, w_out_even,
              w_in_odd, w_out_odd, norm_ffn2, ffn2_w_gu, ffn2_w_down):
    for l in range(DEPTH):
        x = x + 0.5 * swiglu_ffn(rmsnorm(x, norm_ffn1[l]), ffn1_w_gu[l], ffn1_w_down[l])
        h = rmsnorm(x, norm_mix[l])
        j = l // 2
        if l % 2 == 0:
            x = x + deltanet_fox_mixer(h, w_in_even[j], dn_conv_w[j], dn_a_log[j], dn_dt_bias[j],
                                       dn_norm_g[j], fox_q_norm_g[j], fox_k_norm_g[j], fox_f_bias[j],
                                       w_out_even[j])
        else:
            x = x + stick_breaking_mixer(h, w_in_odd[j], w_out_odd[j])
        x = x + 0.5 * swiglu_ffn(rmsnorm(x, norm_ffn2[l]), ffn2_w_gu[l], ffn2_w_down[l])
    return x
```

```python
import functools

import jax
import jax.numpy as jnp
from jax import lax
from jax.experimental import pallas as pl
from jax.experimental.pallas import tpu as pltpu

F32 = jnp.float32
BF16 = jnp.bfloat16

D_MODEL = 1024
HEAD_DIM = 128
N_DN_HEADS = 4
N_FOX_HEADS = 4
N_SB_HEADS = 8
D_DN = N_DN_HEADS * HEAD_DIM
D_FOX = N_FOX_HEADS * HEAD_DIM
CONV_WIDTH = 4
D_FF = 2816
EPS = 1e-6
SCALE = HEAD_DIM ** -0.5

LANES = 128
SUBLANES = 8
VMEM_LIMIT = 56 * 1024 * 1024

FFN_ROWS = 512
MIX_ROWS = 256
DN_CHUNK = 64

LANE_BETA = 0
LANE_DECAY = N_DN_HEADS
LANE_FORGET = 2 * N_DN_HEADS


def _dot(a, b):
    return jnp.dot(a.astype(BF16), b.astype(BF16), preferred_element_type=F32)


def _dot_nt(a, b):
    return lax.dot_general(a.astype(BF16), b.astype(BF16), (((1,), (1,)), ((), ())),
                           preferred_element_type=F32)


def _dot_tn(a, b):
    return lax.dot_general(a.astype(BF16), b.astype(BF16), (((0,), (0,)), ((), ())),
                           preferred_element_type=F32)


def _split_bf16(x, parts):
    out = []
    r = x
    for _ in range(parts - 1):
        p = r.astype(BF16)
        out.append(p)
        r = r - p.astype(F32)
    out.append(r.astype(BF16))
    return out


def _dot_exact_lhs(m_bf16, x, parts):
    acc = None
    for p in _split_bf16(x, parts):
        t = jnp.dot(m_bf16, p, preferred_element_type=F32)
        acc = t if acc is None else acc + t
    return acc


def _dot_exact_rhs(x, m_bf16, parts):
    acc = None
    for p in _split_bf16(x, parts):
        t = jnp.dot(p, m_bf16, preferred_element_type=F32)
        acc = t if acc is None else acc + t
    return acc


def _rms(xf, g):
    return xf * lax.rsqrt(jnp.mean(xf * xf, axis=-1, keepdims=True) + EPS) * g


def _l2n(xf):
    return xf * lax.rsqrt(jnp.sum(xf * xf, axis=-1, keepdims=True) + EPS)


def _softplus(x):
    return jnp.maximum(x, 0.0) + jnp.log1p(jnp.exp(-jnp.abs(x)))


def _log_sigmoid(x):
    return jnp.minimum(x, 0.0) - jnp.log1p(jnp.exp(-jnp.abs(x)))


def _sigmoid(x):
    return 1.0 / (1.0 + jnp.exp(-x))


def _silu(x):
    return x * _sigmoid(x)


def _const_spec(shape):
    return pl.BlockSpec(shape, lambda *_: (0,) * len(shape))


def _ffn_kernel(n_pre, *refs):
    x_ref = refs[0]
    pre = refs[1:1 + 2 * n_pre]
    g_ref, wg_ref, wu_ref, wd_ref, o_ref = refs[1 + 2 * n_pre:]
    x = x_ref[...]
    for p in range(n_pre):
        x = x + jnp.dot(pre[2 * p][...], pre[2 * p + 1][...], preferred_element_type=F32)
    h = _rms(x, g_ref[...]).astype(BF16)
    gate = jnp.dot(h, wg_ref[...], preferred_element_type=F32)
    up = jnp.dot(h, wu_ref[...], preferred_element_type=F32)
    act = (_silu(gate) * up).astype(BF16)
    o_ref[...] = x + 0.5 * jnp.dot(act, wd_ref[...], preferred_element_type=F32)


def _ffn_call(x, pre, norm_g, wg, wu, wd):
    t = x.shape[0]
    row = lambda i: (i, 0)
    in_specs = [pl.BlockSpec((FFN_ROWS, D_MODEL), row)]
    args = [x]
    for o, w in pre:
        in_specs += [pl.BlockSpec((FFN_ROWS, o.shape[1]), row), _const_spec(w.shape)]
        args += [o, w]
    in_specs += [_const_spec((1, D_MODEL)), _const_spec(wg.shape), _const_spec(wu.shape),
                 _const_spec(wd.shape)]
    args += [norm_g.reshape(1, D_MODEL), wg, wu, wd]
    return pl.pallas_call(
        functools.partial(_ffn_kernel, len(pre)),
        grid=(t // FFN_ROWS,),
        in_specs=in_specs,
        out_specs=pl.BlockSpec((FFN_ROWS, D_MODEL), row),
        out_shape=jax.ShapeDtypeStruct((t, D_MODEL), F32),
        compiler_params=pltpu.CompilerParams(
            dimension_semantics=("parallel",), vmem_limit_bytes=VMEM_LIMIT),
        name="ffn",
    )(*args)


def _inproj_even_kernel(x_ref, g_ref, w_ref, qg_ref, kg_ref,
                        qkv_ref, dgate_ref, fq_ref, fk_ref, fv_ref, fgate_ref, small_ref):
    h = _rms(x_ref[...], g_ref[...]).astype(BF16)
    c = 0

    def proj(width):
        nonlocal c
        y = jnp.dot(h, w_ref[:, c:c + width], preferred_element_type=F32)
        c += width
        return y

    qkv_ref[...] = proj(3 * D_DN)
    dgate_ref[...] = proj(D_DN)
    fq = proj(D_FOX)
    fk = proj(D_FOX)
    for hd in range(N_FOX_HEADS):
        sl = slice(hd * HEAD_DIM, (hd + 1) * HEAD_DIM)
        fq_ref[:, sl] = _rms(fq[:, sl], qg_ref[...]).astype(BF16)
        fk_ref[:, sl] = _rms(fk[:, sl], kg_ref[...]).astype(BF16)
    fv_ref[...] = proj(D_FOX).astype(BF16)
    fgate_ref[...] = proj(D_FOX)
    small_ref[...] = proj(LANES)


def _inproj_even_call(x, norm_g, w, q_g, k_g):
    t = x.shape[0]
    row = lambda i: (i, 0)
    widths = [(3 * D_DN, F32), (D_DN, F32), (D_FOX, BF16), (D_FOX, BF16), (D_FOX, BF16),
              (D_FOX, F32), (LANES, F32)]
    return pl.pallas_call(
        _inproj_even_kernel,
        grid=(t // FFN_ROWS,),
        in_specs=[pl.BlockSpec((FFN_ROWS, D_MODEL), row), _const_spec((1, D_MODEL)),
                  _const_spec(w.shape), _const_spec((1, HEAD_DIM)), _const_spec((1, HEAD_DIM))],
        out_specs=[pl.BlockSpec((FFN_ROWS, n), row) for n, _ in widths],
        out_shape=[jax.ShapeDtypeStruct((t, n), dt) for n, dt in widths],
        compiler_params=pltpu.CompilerParams(
            dimension_semantics=("parallel",), vmem_limit_bytes=VMEM_LIMIT),
        name="inproj_even",
    )(x, norm_g.reshape(1, D_MODEL), w, q_g.reshape(1, HEAD_DIM), k_g.reshape(1, HEAD_DIM))


def _inproj_odd_kernel(x_ref, g_ref, w_ref, o_ref):
    h = _rms(x_ref[...], g_ref[...]).astype(BF16)
    o_ref[...] = jnp.dot(h, w_ref[...], preferred_element_type=F32).astype(BF16)


def _inproj_odd_call(x, norm_g, w):
    t = x.shape[0]
    n = w.shape[1]
    row = lambda i: (i, 0)
    return pl.pallas_call(
        _inproj_odd_kernel,
        grid=(t // FFN_ROWS,),
        in_specs=[pl.BlockSpec((FFN_ROWS, D_MODEL), row), _const_spec((1, D_MODEL)),
                  _const_spec(w.shape)],
        out_specs=pl.BlockSpec((FFN_ROWS, n), row),
        out_shape=jax.ShapeDtypeStruct((t, n), BF16),
        compiler_params=pltpu.CompilerParams(
            dimension_semantics=("parallel",), vmem_limit_bytes=VMEM_LIMIT),
        name="inproj_odd",
    )(x, norm_g.reshape(1, D_MODEL), w)


def _iota2(n, axis):
    return lax.broadcasted_iota(jnp.int32, (n, n), axis)


def _unit_lower_inverse(a, row, col):
    n = a.shape[0]
    eye = (row == col).astype(F32)
    x = eye
    b = 1
    while b < DN_CHUNK:
        sh = b.bit_length() - 1
        off = ((row >> (sh + 1)) == (col >> (sh + 1))) & ((row >> sh) != (col >> sh))
        a21 = jnp.where(off, a, 0.0)
        if b == 1:
            x = eye - a21
        else:
            x = x - _dot(_dot(x, a21), x)
        b *= 2
    del n
    return x


def _deltanet_kernel(qkv_ref, halo_ref, gate_ref, small_ref, convw_ref, alog_ref, dtb_ref,
                     ng_ref, o_ref, xx_ref, state_ref):
    i = pl.program_id(1)
    n = MIX_ROWS

    @pl.when(i == 0)
    def _():
        state_ref[...] = jnp.zeros_like(state_ref)

    halo = halo_ref[...]
    xx_ref[0:SUBLANES, :] = jnp.where(i == 0, jnp.zeros_like(halo), halo)
    xx_ref[SUBLANES:SUBLANES + n, :] = qkv_ref[...]
    base = SUBLANES - (CONV_WIDTH - 1)
    y = None
    for j in range(CONV_WIDTH):
        t = xx_ref[base + j:base + j + n, :] * convw_ref[j:j + 1, :]
        y = t if y is None else y + t
    y = _silu(y)

    row = _iota2(n, 0)
    col = _iota2(n, 1)
    sh = DN_CHUNK.bit_length() - 1
    same = (row >> sh) == (col >> sh)
    incl = same & (col <= row)
    strict = same & (col < row)
    tri_incl = incl.astype(BF16)
    blk_ones = same.astype(BF16)

    sm = small_ref[...]
    beta_all = _sigmoid(sm)
    g_all = -jnp.exp(alog_ref[...]) * _softplus(sm + dtb_ref[...])
    gc_all = _dot_exact_lhs(tri_incl, g_all, 3)
    gl_all = _dot_exact_lhs(blk_ones, g_all, 3)
    gc_all_t = gc_all.T

    for hd in range(N_DN_HEADS):
        sl = slice(hd * HEAD_DIM, (hd + 1) * HEAD_DIM)
        q = _l2n(y[:, hd * HEAD_DIM:(hd + 1) * HEAD_DIM]) * SCALE
        k = _l2n(y[:, D_DN + hd * HEAD_DIM:D_DN + (hd + 1) * HEAD_DIM])
        v = y[:, 2 * D_DN + hd * HEAD_DIM:2 * D_DN + (hd + 1) * HEAD_DIM]
        beta = beta_all[:, LANE_BETA + hd:LANE_BETA + hd + 1]
        gc = gc_all[:, LANE_DECAY + hd:LANE_DECAY + hd + 1]
        gl = gl_all[:, LANE_DECAY + hd:LANE_DECAY + hd + 1]
        gc_row = gc_all_t[LANE_DECAY + hd:LANE_DECAY + hd + 1, :]
        decay = jnp.where(incl, jnp.exp(jnp.where(incl, gc - gc_row, 0.0)), 0.0)
        egc = jnp.exp(gc)
        kb = k * beta
        vb = v * beta
        a = jnp.where(strict, _dot_nt(kb, k) * decay, 0.0)
        tinv = _unit_lower_inverse(a, row, col)
        u = _dot(tinv, vb)
        w = _dot(tinv, kb * egc)
        attn = _dot_nt(q, k) * decay
        qd = q * egc
        kd = k * jnp.exp(gl - gc)
        egl = jnp.exp(gl)
        state = state_ref[hd]
        outs = []
        for c in range(n // DN_CHUNK):
            r = slice(c * DN_CHUNK, (c + 1) * DN_CHUNK)
            v_new = u[r] - _dot(w[r], state)
            outs.append(_dot(qd[r], state) + _dot(attn[r, r], v_new))
            state = state * egl[c * DN_CHUNK:c * DN_CHUNK + 1, :] + _dot_tn(kd[r], v_new)
        state_ref[hd] = state
        o = jnp.concatenate(outs, axis=0)
        o_ref[:, sl] = (_rms(o, ng_ref[...]) * _silu(gate_ref[:, sl])).astype(BF16)


def _deltanet_call(qkv, gate, small, conv_w, a_log, dt_bias, norm_g):
    b, s, _ = qkv.shape
    per_blk = MIX_ROWS // SUBLANES
    lane_vec = lambda v: jnp.zeros((1, LANES), F32).at[0, LANE_DECAY:LANE_DECAY + N_DN_HEADS].set(v)
    blk = lambda w: pl.BlockSpec((None, MIX_ROWS, w), lambda bi, i: (bi, i, 0))
    return pl.pallas_call(
        _deltanet_kernel,
        grid=(b, s // MIX_ROWS),
        in_specs=[blk(3 * D_DN),
                  pl.BlockSpec((None, SUBLANES, 3 * D_DN),
                               lambda bi, i: (bi, jnp.maximum(i * per_blk - 1, 0), 0)),
                  blk(D_DN), blk(LANES), _const_spec((CONV_WIDTH, 3 * D_DN)),
                  _const_spec((1, LANES)), _const_spec((1, LANES)), _const_spec((1, HEAD_DIM))],
        out_specs=blk(D_DN),
        out_shape=jax.ShapeDtypeStruct((b, s, D_DN), BF16),
        scratch_shapes=[pltpu.VMEM((SUBLANES + MIX_ROWS, 3 * D_DN), F32),
                        pltpu.VMEM((N_DN_HEADS, HEAD_DIM, HEAD_DIM), F32)],
        compiler_params=pltpu.CompilerParams(
            dimension_semantics=("parallel", "arbitrary"), vmem_limit_bytes=VMEM_LIMIT),
        name="deltanet",
    )(qkv, qkv, gate, small, conv_w, lane_vec(a_log), lane_vec(dt_bias),
      norm_g.reshape(1, HEAD_DIM))


def _fox_kernel(q_ref, k_ref, v_ref, small_ref, fbias_ref, gate_ref, o_ref, ccol_ref, crow_ref):
    i = pl.program_id(1)
    n = MIX_ROWS
    s = k_ref.shape[0]
    row = _iota2(n, 0)
    col = _iota2(n, 1)

    @pl.when(i == 0)
    def _():
        tri = (col <= row).astype(BF16)
        carry = jnp.zeros((1, LANES), F32)
        for blk in range(s // n):
            r = slice(blk * n, (blk + 1) * n)
            lf = _log_sigmoid(small_ref[r, :] + fbias_ref[...])
            c = _dot_exact_lhs(tri, lf, 3) + carry
            carry = c[n - 1:n, :]
            ccol_ref[r, :] = c
            crow_ref[:, r] = c.T

    q0 = pl.multiple_of(i * n, n)
    ccol = ccol_ref[pl.ds(q0, n), :]
    causal = col <= row
    for hd in range(N_FOX_HEADS):
        sl = slice(hd * HEAD_DIM, (hd + 1) * HEAD_DIM)
        lane = LANE_FORGET + hd
        q = q_ref[:, sl]
        cq = ccol[:, lane:lane + 1]

        def scores(k0):
            kb = k_ref[pl.ds(k0, n), sl]
            ck = crow_ref[lane:lane + 1, pl.ds(k0, n)]
            return _dot_nt(q, kb) * SCALE + (cq - ck)

        def update(carry, logits, k0):
            m, l, acc = carry
            m_new = jnp.maximum(m, jnp.max(logits, axis=-1, keepdims=True))
            alpha = jnp.exp(m - m_new)
            p = jnp.exp(logits - m_new)
            l = alpha * l + jnp.sum(p, axis=-1, keepdims=True)
            acc = alpha * acc + _dot(p, v_ref[pl.ds(k0, n), sl])
            return m_new, l, acc

        def body(j, carry):
            k0 = pl.multiple_of(j * n, n)
            return update(carry, scores(k0), k0)

        init = (jnp.full((n, 1), -1e30, F32), jnp.zeros((n, 1), F32),
                jnp.zeros((n, HEAD_DIM), F32))
        carry = lax.fori_loop(0, i, body, init)
        logits = jnp.where(causal, scores(q0), -1e30)
        m, l, acc = update(carry, logits, q0)
        o_ref[:, sl] = (acc / l * _sigmoid(gate_ref[:, sl])).astype(BF16)


def _fox_call(q, k, v, small, f_bias, gate):
    b, s, _ = q.shape
    blk = lambda w: pl.BlockSpec((None, MIX_ROWS, w), lambda bi, i: (bi, i, 0))
    full = lambda w: pl.BlockSpec((None, s, w), lambda bi, i: (bi, 0, 0))
    fb = jnp.zeros((1, LANES), F32).at[0, LANE_FORGET:LANE_FORGET + N_FOX_HEADS].set(f_bias)
    return pl.pallas_call(
        _fox_kernel,
        grid=(b, s // MIX_ROWS),
        in_specs=[blk(D_FOX), full(D_FOX), full(D_FOX), full(LANES), _const_spec((1, LANES)),
                  blk(D_FOX)],
        out_specs=blk(D_FOX),
        out_shape=jax.ShapeDtypeStruct((b, s, D_FOX), BF16),
        scratch_shapes=[pltpu.VMEM((s, LANES), F32), pltpu.VMEM((LANES, s), F32)],
        compiler_params=pltpu.CompilerParams(
            dimension_semantics=("parallel", "arbitrary"), vmem_limit_bytes=VMEM_LIMIT),
        name="fox_attention",
    )(q, k, v, small, fb, gate)


def _sb_kernel(q_ref, k_ref, v_ref, o_ref):
    i = pl.program_id(2)
    n = MIX_ROWS
    row = _iota2(n, 0)
    col = _iota2(n, 1)
    before = col < row
    later = (row > col).astype(BF16)
    q = q_ref[...]

    def block(k0, rest, masked):
        z = _dot_nt(q, k_ref[pl.ds(k0, n), :]) * SCALE
        lsn = _log_sigmoid(-z)
        lg = jnp.where(before, lsn, 0.0) if masked else lsn
        tail = _dot_exact_rhs(lg, later, 2) + rest
        a = jnp.exp(lsn + z + tail)
        if masked:
            a = jnp.where(before, a, 0.0)
        pv = _dot(a, v_ref[pl.ds(k0, n), :])
        return pv, rest + jnp.sum(lg, axis=-1, keepdims=True)

    q0 = pl.multiple_of(i * n, n)
    acc, rest = block(q0, jnp.zeros((n, 1), F32), True)

    def body(t, carry):
        acc, rest = carry
        k0 = pl.multiple_of((i - 1 - t) * n, n)
        pv, rest = block(k0, rest, False)
        return acc + pv, rest

    acc, rest = lax.fori_loop(0, i, body, (acc, rest))
    o_ref[...] = acc.astype(BF16)


def _sb_call(qkv):
    b, s, _ = qkv.shape
    nh = N_SB_HEADS
    return pl.pallas_call(
        _sb_kernel,
        grid=(b, nh, s // MIX_ROWS),
        in_specs=[pl.BlockSpec((None, MIX_ROWS, HEAD_DIM), lambda bi, h, i: (bi, i, h)),
                  pl.BlockSpec((None, s, HEAD_DIM), lambda bi, h, i: (bi, 0, nh + h)),
                  pl.BlockSpec((None, s, HEAD_DIM), lambda bi, h, i: (bi, 0, 2 * nh + h))],
        out_specs=pl.BlockSpec((None, MIX_ROWS, HEAD_DIM), lambda bi, h, i: (bi, i, h)),
        out_shape=jax.ShapeDtypeStruct((b, s, nh * HEAD_DIM), BF16),
        compiler_params=pltpu.CompilerParams(
            dimension_semantics=("parallel", "parallel", "arbitrary"),
            vmem_limit_bytes=VMEM_LIMIT),
        name="stickbreaking_attention",
    )(qkv, qkv, qkv)


def _even_w_in(w):
    sizes = (3 * D_DN, D_DN, N_DN_HEADS, N_DN_HEADS, D_FOX, D_FOX, D_FOX, D_FOX, N_FOX_HEADS)
    parts, off = [], 0
    for sz in sizes:
        parts.append(w[:, off:off + sz])
        off += sz
    qkv, dgate, dbeta, ddecay, fq, fk, fv, fgate, fpre = parts
    pad = jnp.zeros((w.shape[0], LANES - 2 * N_DN_HEADS - N_FOX_HEADS), w.dtype)
    small = jnp.concatenate([dbeta, ddecay, fpre, pad], axis=1)
    return jnp.concatenate([qkv, dgate, fq, fk, fv, fgate, small], axis=1)


def kernel(x, norm_ffn1, ffn1_w_gu, ffn1_w_down, norm_mix, w_in_even, dn_conv_w, dn_a_log,
           dn_dt_bias, dn_norm_g, fox_q_norm_g, fox_k_norm_g, fox_f_bias, w_out_even,
           w_in_odd, w_out_odd, norm_ffn2, ffn2_w_gu, ffn2_w_down):
    b, s, d = x.shape
    depth = norm_ffn1.shape[0]
    t = b * s
    bf = lambda a: a.astype(BF16)
    x = x.reshape(t, d)
    pre = []
    for l in range(depth):
        j = l // 2
        x = _ffn_call(x, pre, norm_ffn1[l], bf(ffn1_w_gu[l][:, :D_FF]), bf(ffn1_w_gu[l][:, D_FF:]),
                      bf(ffn1_w_down[l]))
        if l % 2 == 0:
            qkv, dgate, fq, fk, fv, fgate, small = _inproj_even_call(
                x, norm_mix[l], bf(_even_w_in(w_in_even[j])), fox_q_norm_g[j], fox_k_norm_g[j])
            r3 = lambda a: a.reshape(b, s, a.shape[-1])
            o_dn = _deltanet_call(r3(qkv), r3(dgate), r3(small), dn_conv_w[j], dn_a_log[j],
                                  dn_dt_bias[j], dn_norm_g[j])
            o_fox = _fox_call(r3(fq), r3(fk), r3(fv), r3(small), fox_f_bias[j], r3(fgate))
            w_out = bf(w_out_even[j])
            pre = [(o_dn.reshape(t, D_DN), w_out[:D_DN]), (o_fox.reshape(t, D_FOX), w_out[D_DN:])]
        else:
            qkv = _inproj_odd_call(x, norm_mix[l], bf(w_in_odd[j]))
            o = _sb_call(qkv.reshape(b, s, qkv.shape[-1]))
            pre = [(o.reshape(t, o.shape[-1]), bf(w_out_odd[j]))]
        x = _ffn_call(x, pre, norm_ffn2[l], bf(ffn2_w_gu[l][:, :D_FF]), bf(ffn2_w_gu[l][:, D_FF:]),
                      bf(ffn2_w_down[l]))
        pre = []
    return x.reshape(b, s, d)
```

```python
import functools

import jax
import jax.numpy as jnp
from jax import lax
from jax.experimental import pallas as pl
from jax.experimental.pallas import tpu as pltpu

F32 = jnp.float32
BF16 = jnp.bfloat16

D_MODEL = 1024
HEAD_DIM = 128
N_DN_HEADS = 4
N_FOX_HEADS = 4
N_SB_HEADS = 8
D_DN = N_DN_HEADS * HEAD_DIM
D_FOX = N_FOX_HEADS * HEAD_DIM
CONV_WIDTH = 4
D_FF = 2816
EPS = 1e-6
SCALE = HEAD_DIM ** -0.5

LANES = 128
SUBLANES = 8
VMEM_LIMIT = 56 * 1024 * 1024

FFN_ROWS = 512
MIX_ROWS = 256
DN_CHUNK = 64

LANE_BETA = 0
LANE_DECAY = N_DN_HEADS
LANE_FORGET = 2 * N_DN_HEADS


def _dot(a, b):
    return jnp.dot(a.astype(BF16), b.astype(BF16), preferred_element_type=F32)


def _dot_nt(a, b):
    return lax.dot_general(a.astype(BF16), b.astype(BF16), (((1,), (1,)), ((), ())),
                           preferred_element_type=F32)


def _dot_tn(a, b):
    return lax.dot_general(a.astype(BF16), b.astype(BF16), (((0,), (0,)), ((), ())),
                           preferred_element_type=F32)


def _split_bf16(x, parts):
    out = []
    r = x
    for _ in range(parts - 1):
        p = r.astype(BF16)
        out.append(p)
        r = r - p.astype(F32)
    out.append(r.astype(BF16))
    return out


def _dot_exact_lhs(m_bf16, x, parts):
    acc = None
    for p in _split_bf16(x, parts):
        t = jnp.dot(m_bf16, p, preferred_element_type=F32)
        acc = t if acc is None else acc + t
    return acc


def _dot_exact_rhs(x, m_bf16, parts):
    acc = None
    for p in _split_bf16(x, parts):
        t = jnp.dot(p, m_bf16, preferred_element_type=F32)
        acc = t if acc is None else acc + t
    return acc


def _rms(xf, g):
    return xf * lax.rsqrt(jnp.mean(xf * xf, axis=-1, keepdims=True) + EPS) * g


def _l2n(xf):
    return xf * lax.rsqrt(jnp.sum(xf * xf, axis=-1, keepdims=True) + EPS)


def _softplus(x):
    return jnp.maximum(x, 0.0) + jnp.log1p(jnp.exp(-jnp.abs(x)))


def _log_sigmoid(x):
    return jnp.minimum(x, 0.0) - jnp.log1p(jnp.exp(-jnp.abs(x)))


def _sigmoid(x):
    return 1.0 / (1.0 + jnp.exp(-x))


def _silu(x):
    return x * _sigmoid(x)


def _const_spec(shape):
    return pl.BlockSpec(shape, lambda *_: (0,) * len(shape))


def _ffn_kernel(n_pre, *refs):
    x_ref = refs[0]
    pre = refs[1:1 + 2 * n_pre]
    g_ref, wg_ref, wu_ref, wd_ref, o_ref = refs[1 + 2 * n_pre:]
    x = x_ref[...]
    for p in range(n_pre):
        x = x + jnp.dot(pre[2 * p][...], pre[2 * p + 1][...], preferred_element_type=F32)
    h = _rms(x, g_ref[...]).astype(BF16)
    gate = jnp.dot(h, wg_ref[...], preferred_element_type=F32)
    up = jnp.dot(h, wu_ref[...], preferred_element_type=F32)
    act = (_silu(gate) * up).astype(BF16)
    o_ref[...] = x + 0.5 * jnp.dot(act, wd_ref[...], preferred_element_type=F32)


def _ffn_call(x, pre, norm_g, wg, wu, wd):
    t = x.shape[0]
    row = lambda i: (i, 0)
    in_specs = [pl.BlockSpec((FFN_ROWS, D_MODEL), row)]
    args = [x]
    for o, w in pre:
        in_specs += [pl.BlockSpec((FFN_ROWS, o.shape[1]), row), _const_spec(w.shape)]
        args += [o, w]
    in_specs += [_const_spec((1, D_MODEL)), _const_spec(wg.shape), _const_spec(wu.shape),
                 _const_spec(wd.shape)]
    args += [norm_g.reshape(1, D_MODEL), wg, wu, wd]
    return pl.pallas_call(
        functools.partial(_ffn_kernel, len(pre)),
        grid=(t // FFN_ROWS,),
        in_specs=in_specs,
        out_specs=pl.BlockSpec((FFN_ROWS, D_MODEL), row),
        out_shape=jax.ShapeDtypeStruct((t, D_MODEL), F32),
        compiler_params=pltpu.CompilerParams(
            dimension_semantics=("parallel",), vmem_limit_bytes=VMEM_LIMIT),
        name="ffn",
    )(*args)


def _inproj_even_kernel(x_ref, g_ref, w_ref, qg_ref, kg_ref,
                        qkv_ref, dgate_ref, fq_ref, fk_ref, fv_ref, fgate_ref, small_ref):
    h = _rms(x_ref[...], g_ref[...]).astype(BF16)
    c = 0

    def proj(width):
        nonlocal c
        y = jnp.dot(h, w_ref[:, c:c + width], preferred_element_type=F32)
        c += width
        return y

    qkv_ref[...] = proj(3 * D_DN)
    dgate_ref[...] = proj(D_DN)
    fq = proj(D_FOX)
    fk = proj(D_FOX)
    for hd in range(N_FOX_HEADS):
        sl = slice(hd * HEAD_DIM, (hd + 1) * HEAD_DIM)
        fq_ref[:, sl] = _rms(fq[:, sl], qg_ref[...]).astype(BF16)
        fk_ref[:, sl] = _rms(fk[:, sl], kg_ref[...]).astype(BF16)
    fv_ref[...] = proj(D_FOX).astype(BF16)
    fgate_ref[...] = proj(D_FOX)
    small_ref[...] = proj(LANES)


def _inproj_even_call(x, norm_g, w, q_g, k_g):
    t = x.shape[0]
    row = lambda i: (i, 0)
    widths = [(3 * D_DN, F32), (D_DN, F32), (D_FOX, BF16), (D_FOX, BF16), (D_FOX, BF16),
              (D_FOX, F32), (LANES, F32)]
    return pl.pallas_call(
        _inproj_even_kernel,
        grid=(t // FFN_ROWS,),
        in_specs=[pl.BlockSpec((FFN_ROWS, D_MODEL), row), _const_spec((1, D_MODEL)),
                  _const_spec(w.shape), _const_spec((1, HEAD_DIM)), _const_spec((1, HEAD_DIM))],
        out_specs=[pl.BlockSpec((FFN_ROWS, n), row) for n, _ in widths],
        out_shape=[jax.ShapeDtypeStruct((t, n), dt) for n, dt in widths],
        compiler_params=pltpu.CompilerParams(
            dimension_semantics=("parallel",), vmem_limit_bytes=VMEM_LIMIT),
        name="inproj_even",
    )(x, norm_g.reshape(1, D_MODEL), w, q_g.reshape(1, HEAD_DIM), k_g.reshape(1, HEAD_DIM))


def _inproj_odd_kernel(x_ref, g_ref, w_ref, o_ref):
    h = _rms(x_ref[...], g_ref[...]).astype(BF16)
    o_ref[...] = jnp.dot(h, w_ref[...], preferred_element_type=F32).astype(BF16)


def _inproj_odd_call(x, norm_g, w):
    t = x.shape[0]
    n = w.shape[1]
    row = lambda i: (i, 0)
    return pl.pallas_call(
        _inproj_odd_kernel,
        grid=(t // FFN_ROWS,),
        in_specs=[pl.BlockSpec((FFN_ROWS, D_MODEL), row), _const_spec((1, D_MODEL)),
                  _const_spec(w.shape)],
        out_specs=pl.BlockSpec((FFN_ROWS, n), row),
        out_shape=jax.ShapeDtypeStruct((t, n), BF16),
        compiler_params=pltpu.CompilerParams(
            dimension_semantics=("parallel",), vmem_limit_bytes=VMEM_LIMIT),
        name="inproj_odd",
    )(x, norm_g.reshape(1, D_MODEL), w)


def _iota2(n, axis):
    return lax.broadcasted_iota(jnp.int32, (n, n), axis)


def _unit_lower_inverse(a, row, col):
    n = a.shape[0]
    eye = (row == col).astype(F32)
    x = eye
    b = 1
    while b < DN_CHUNK:
        sh = b.bit_length() - 1
        off = ((row >> (sh + 1)) == (col >> (sh + 1))) & ((row >> sh) != (col >> sh))
        a21 = jnp.where(off, a, 0.0)
        if b == 1:
            x = eye - a21
        else:
            x = x - _dot(_dot(x, a21), x)
        b *= 2
    del n
    return x


def _deltanet_kernel(qkv_ref, halo_ref, gate_ref, small_ref, convw_ref, alog_ref, dtb_ref,
                     ng_ref, o_ref, xx_ref, state_ref):
    i = pl.program_id(1)
    n = MIX_ROWS

    @pl.when(i == 0)
    def _():
        state_ref[...] = jnp.zeros_like(state_ref)

    halo = halo_ref[...]
    xx_ref[0:SUBLANES, :] = jnp.where(i == 0, jnp.zeros_like(halo), halo)
    xx_ref[SUBLANES:SUBLANES + n, :] = qkv_ref[...]
    base = SUBLANES - (CONV_WIDTH - 1)
    y = None
    for j in range(CONV_WIDTH):
        t = xx_ref[base + j:base + j + n, :] * convw_ref[j:j + 1, :]
        y = t if y is None else y + t
    y = _silu(y)

    row = _iota2(n, 0)
    col = _iota2(n, 1)
    sh = DN_CHUNK.bit_length() - 1
    same = (row >> sh) == (col >> sh)
    incl = same & (col <= row)
    strict = same & (col < row)
    tri_incl = incl.astype(BF16)
    blk_ones = same.astype(BF16)

    sm = small_ref[...]
    beta_all = _sigmoid(sm)
    g_all = -jnp.exp(alog_ref[...]) * _softplus(sm + dtb_ref[...])
    gc_all = _dot_exact_lhs(tri_incl, g_all, 3)
    gl_all = _dot_exact_lhs(blk_ones, g_all, 3)
    gc_all_t = gc_all.T

    for hd in range(N_DN_HEADS):
        sl = slice(hd * HEAD_DIM, (hd + 1) * HEAD_DIM)
        q = _l2n(y[:, hd * HEAD_DIM:(hd + 1) * HEAD_DIM]) * SCALE
        k = _l2n(y[:, D_DN + hd * HEAD_DIM:D_DN + (hd + 1) * HEAD_DIM])
        v = y[:, 2 * D_DN + hd * HEAD_DIM:2 * D_DN + (hd + 1) * HEAD_DIM]
        beta = beta_all[:, LANE_BETA + hd:LANE_BETA + hd + 1]
        gc = gc_all[:, LANE_DECAY + hd:LANE_DECAY + hd + 1]
        gl = gl_all[:, LANE_DECAY + hd:LANE_DECAY + hd + 1]
        gc_row = gc_all_t[LANE_DECAY + hd:LANE_DECAY + hd + 1, :]
        decay = jnp.where(incl, jnp.exp(jnp.where(incl, gc - gc_row, 0.0)), 0.0)
        egc = jnp.exp(gc)
        kb = k * beta
        vb = v * beta
        a = jnp.where(strict, _dot_nt(kb, k) * decay, 0.0)
        tinv = _unit_lower_inverse(a, row, col)
        u = _dot(tinv, vb)
        w = _dot(tinv, kb * egc)
        attn = _dot_nt(q, k) * decay
        qd = q * egc
        kd = k * jnp.exp(gl - gc)
        egl = jnp.exp(gl)
        state = state_ref[hd]
        outs = []
        for c in range(n // DN_CHUNK):
            r = slice(c * DN_CHUNK, (c + 1) * DN_CHUNK)
            v_new = u[r] - _dot(w[r], state)
            outs.append(_dot(qd[r], state) + _dot(attn[r, r], v_new))
            state = state * egl[c * DN_CHUNK:c * DN_CHUNK + 1, :] + _dot_tn(kd[r], v_new)
        state_ref[hd] = state
        o = jnp.concatenate(outs, axis=0)
        o_ref[:, sl] = (_rms(o, ng_ref[...]) * _silu(gate_ref[:, sl])).astype(BF16)


def _deltanet_call(qkv, gate, small, conv_w, a_log, dt_bias, norm_g):
    b, s, _ = qkv.shape
    per_blk = MIX_ROWS // SUBLANES
    lane_vec = lambda v: jnp.zeros((1, LANES), F32).at[0, LANE_DECAY:LANE_DECAY + N_DN_HEADS].set(v)
    blk = lambda w: pl.BlockSpec((None, MIX_ROWS, w), lambda bi, i: (bi, i, 0))
    return pl.pallas_call(
        _deltanet_kernel,
        grid=(b, s // MIX_ROWS),
        in_specs=[blk(3 * D_DN),
                  pl.BlockSpec((None, SUBLANES, 3 * D_DN),
                               lambda bi, i: (bi, jnp.maximum(i * per_blk - 1, 0), 0)),
                  blk(D_DN), blk(LANES), _const_spec((CONV_WIDTH, 3 * D_DN)),
                  _const_spec((1, LANES)), _const_spec((1, LANES)), _const_spec((1, HEAD_DIM))],
        out_specs=blk(D_DN),
        out_shape=jax.ShapeDtypeStruct((b, s, D_DN), BF16),
        scratch_shapes=[pltpu.VMEM((SUBLANES + MIX_ROWS, 3 * D_DN), F32),
                        pltpu.VMEM((N_DN_HEADS, HEAD_DIM, HEAD_DIM), F32)],
        compiler_params=pltpu.CompilerParams(
            dimension_semantics=("parallel", "arbitrary"), vmem_limit_bytes=VMEM_LIMIT),
        name="deltanet",
    )(qkv, qkv, gate, small, conv_w, lane_vec(a_log), lane_vec(dt_bias),
      norm_g.reshape(1, HEAD_DIM))


def _fox_kernel(q_ref, k_ref, v_ref, small_ref, fbias_ref, gate_ref, o_ref, crow_ref):
    i = pl.program_id(1)
    n = MIX_ROWS
    s = k_ref.shape[0]
    row = _iota2(n, 0)
    col = _iota2(n, 1)

    @pl.when(i == 0)
    def _():
        tri = (col <= row).astype(BF16)
        carry = jnp.zeros((1, LANES), F32)
        for blk in range(s // n):
            r = slice(blk * n, (blk + 1) * n)
            lf = _log_sigmoid(small_ref[r, :] + fbias_ref[...])
            c = _dot_exact_lhs(tri, lf, 3) + carry
            carry = c[n - 1:n, :]
            crow_ref[:, r] = c.T

    causal = col <= row

    def step(carry, k0, masked):
        out = []
        for hd in range(N_FOX_HEADS):
            sl = slice(hd * HEAD_DIM, (hd + 1) * HEAD_DIM)
            lane = LANE_FORGET + hd
            m, l, acc = carry[3 * hd:3 * hd + 3]
            ck = crow_ref[lane:lane + 1, pl.ds(k0, n)]
            logits = _dot_nt(q_ref[:, sl], k_ref[pl.ds(k0, n), sl]) * SCALE - ck
            if masked:
                logits = jnp.where(causal, logits, -1e30)
            m_new = jnp.maximum(m, jnp.max(logits, axis=-1, keepdims=True))
            alpha = jnp.exp(m - m_new)
            p = jnp.exp(logits - m_new)
            l = alpha * l + jnp.sum(p, axis=-1, keepdims=True)
            acc = alpha * acc + _dot(p, v_ref[pl.ds(k0, n), sl])
            out += [m_new, l, acc]
        return tuple(out)

    init = (jnp.full((n, 1), -1e30, F32), jnp.zeros((n, 1), F32),
            jnp.zeros((n, HEAD_DIM), F32)) * N_FOX_HEADS
    carry = lax.fori_loop(0, i, lambda j, c: step(c, pl.multiple_of(j * n, n), False), init)
    carry = step(carry, pl.multiple_of(i * n, n), True)
    for hd in range(N_FOX_HEADS):
        sl = slice(hd * HEAD_DIM, (hd + 1) * HEAD_DIM)
        _, l, acc = carry[3 * hd:3 * hd + 3]
        o_ref[:, sl] = (acc / l * _sigmoid(gate_ref[:, sl])).astype(BF16)


def _fox_call(q, k, v, small, f_bias, gate):
    b, s, _ = q.shape
    blk = lambda w: pl.BlockSpec((None, MIX_ROWS, w), lambda bi, i: (bi, i, 0))
    full = lambda w: pl.BlockSpec((None, s, w), lambda bi, i: (bi, 0, 0))
    fb = jnp.zeros((1, LANES), F32).at[0, LANE_FORGET:LANE_FORGET + N_FOX_HEADS].set(f_bias)
    return pl.pallas_call(
        _fox_kernel,
        grid=(b, s // MIX_ROWS),
        in_specs=[blk(D_FOX), full(D_FOX), full(D_FOX), full(LANES), _const_spec((1, LANES)),
                  blk(D_FOX)],
        out_specs=blk(D_FOX),
        out_shape=jax.ShapeDtypeStruct((b, s, D_FOX), BF16),
        scratch_shapes=[pltpu.VMEM((LANES, s), F32)],
        compiler_params=pltpu.CompilerParams(
            dimension_semantics=("parallel", "arbitrary"), vmem_limit_bytes=VMEM_LIMIT),
        name="fox_attention",
    )(q, k, v, small, fb, gate)


SB_CUTOFF = -105.0


def _sb_kernel(q_ref, k_ref, v_ref, o_ref):
    n = MIX_ROWS
    s = k_ref.shape[0]
    row = _iota2(n, 0)
    col = _iota2(n, 1)
    before = col < row
    later = jnp.concatenate([(row > col).astype(BF16), jnp.ones((n, LANES), BF16)], axis=1)
    later2 = jnp.concatenate([later, later], axis=0)

    def block(q, k0, rest, masked):
        z = _dot_nt(q, k_ref[pl.ds(k0, n), :]) * SCALE
        soft = jnp.log(1.0 + jnp.exp(-jnp.abs(z)))
        lsz = jnp.minimum(z, 0.0) - soft
        lsn = lsz - z
        lg = jnp.where(before, lsn, 0.0) if masked else lsn
        hi = lg.astype(BF16)
        lo = (lg - hi.astype(F32)).astype(BF16)
        sums = jnp.dot(jnp.concatenate([hi, lo], axis=1), later2, preferred_element_type=F32)
        tail = sums[:, :n] + jnp.concatenate([rest] * (n // LANES), axis=1)
        a = jnp.exp(lsz + tail)
        if masked:
            a = jnp.where(before, a, 0.0)
        return _dot(a, v_ref[pl.ds(k0, n), :]), rest + sums[:, n:]

    def live(rest):
        return (jnp.max(rest) > SB_CUTOFF).astype(jnp.int32)

    def qblock(i, _):
        q0 = pl.multiple_of(i * n, n)
        q = q_ref[pl.ds(q0, n), :]
        acc, rest = block(q, q0, jnp.zeros((n, LANES), F32), True)

        def cond(c):
            return (c[0] < i) & (c[3] > 0)

        def body(c):
            t, acc, rest, _ = c
            k0 = pl.multiple_of((i - 1 - t) * n, n)
            pv, rest = block(q, k0, rest, False)
            return t + 1, acc + pv, rest, live(rest)

        _, acc, _, _ = lax.while_loop(cond, body, (jnp.int32(0), acc, rest, live(rest)))
        o_ref[pl.ds(q0, n), :] = acc.astype(BF16)
        return 0

    lax.fori_loop(0, s // n, qblock, 0)


def _sb_call(qkv):
    b, s, _ = qkv.shape
    nh = N_SB_HEADS
    head = lambda off: pl.BlockSpec((None, s, HEAD_DIM), lambda bi, h: (bi, 0, off + h))
    return pl.pallas_call(
        _sb_kernel,
        grid=(b, nh),
        in_specs=[head(0), head(nh), head(2 * nh)],
        out_specs=head(0),
        out_shape=jax.ShapeDtypeStruct((b, s, nh * HEAD_DIM), BF16),
        compiler_params=pltpu.CompilerParams(
            dimension_semantics=("parallel", "parallel"), vmem_limit_bytes=VMEM_LIMIT),
        name="stickbreaking_attention",
    )(qkv, qkv, qkv)


def _even_w_in(w):
    sizes = (3 * D_DN, D_DN, N_DN_HEADS, N_DN_HEADS, D_FOX, D_FOX, D_FOX, D_FOX, N_FOX_HEADS)
    parts, off = [], 0
    for sz in sizes:
        parts.append(w[:, off:off + sz])
        off += sz
    qkv, dgate, dbeta, ddecay, fq, fk, fv, fgate, fpre = parts
    pad = jnp.zeros((w.shape[0], LANES - 2 * N_DN_HEADS - N_FOX_HEADS), w.dtype)
    small = jnp.concatenate([dbeta, ddecay, fpre, pad], axis=1)
    return jnp.concatenate([qkv, dgate, fq, fk, fv, fgate, small], axis=1)


def kernel(x, norm_ffn1, ffn1_w_gu, ffn1_w_down, norm_mix, w_in_even, dn_conv_w, dn_a_log,
           dn_dt_bias, dn_norm_g, fox_q_norm_g, fox_k_norm_g, fox_f_bias, w_out_even,
           w_in_odd, w_out_odd, norm_ffn2, ffn2_w_gu, ffn2_w_down):
    b, s, d = x.shape
    depth = norm_ffn1.shape[0]
    t = b * s
    bf = lambda a: a.astype(BF16)
    x = x.reshape(t, d)
    pre = []
    for l in range(depth):
        j = l // 2
        x = _ffn_call(x, pre, norm_ffn1[l], bf(ffn1_w_gu[l][:, :D_FF]), bf(ffn1_w_gu[l][:, D_FF:]),
                      bf(ffn1_w_down[l]))
        if l % 2 == 0:
            qkv, dgate, fq, fk, fv, fgate, small = _inproj_even_call(
                x, norm_mix[l], bf(_even_w_in(w_in_even[j])), fox_q_norm_g[j], fox_k_norm_g[j])
            r3 = lambda a: a.reshape(b, s, a.shape[-1])
            o_dn = _deltanet_call(r3(qkv), r3(dgate), r3(small), dn_conv_w[j], dn_a_log[j],
                                  dn_dt_bias[j], dn_norm_g[j])
            o_fox = _fox_call(r3(fq), r3(fk), r3(fv), r3(small), fox_f_bias[j], r3(fgate))
            w_out = bf(w_out_even[j])
            pre = [(o_dn.reshape(t, D_DN), w_out[:D_DN]), (o_fox.reshape(t, D_FOX), w_out[D_DN:])]
        else:
            qkv = _inproj_odd_call(x, norm_mix[l], bf(w_in_odd[j]))
            o = _sb_call(qkv.reshape(b, s, qkv.shape[-1]))
            pre = [(o.reshape(t, o.shape[-1]), bf(w_out_odd[j]))]
        x = _ffn_call(x, pre, norm_ffn2[l], bf(ffn2_w_gu[l][:, :D_FF]), bf(ffn2_w_gu[l][:, D_FF:]),
                      bf(ffn2_w_down[l]))
        pre = []
    return x.reshape(b, s, d)
```

```python
import functools

import jax
import jax.numpy as jnp
from jax import lax
from jax.experimental import pallas as pl
from jax.experimental.pallas import tpu as pltpu

F32 = jnp.float32
BF16 = jnp.bfloat16

D_MODEL = 1024
HEAD_DIM = 128
N_DN_HEADS = 4
N_FOX_HEADS = 4
N_SB_HEADS = 8
D_DN = N_DN_HEADS * HEAD_DIM
D_FOX = N_FOX_HEADS * HEAD_DIM
CONV_WIDTH = 4
D_FF = 2816
EPS = 1e-6
SCALE = HEAD_DIM ** -0.5

LANES = 128
SUBLANES = 8
VMEM_LIMIT = 56 * 1024 * 1024

FFN_ROWS = 512
SB_ROWS = 256
FOX_ROWS = 512
FOX_HEADS_PER_LOOP = 4
DN_ROWS = 512
DN_GROUP = 256
DN_CHUNK = 64

LANE_BETA = 0
LANE_DECAY = N_DN_HEADS
LANE_FORGET = 2 * N_DN_HEADS


def _dot(a, b):
    return jnp.dot(a.astype(BF16), b.astype(BF16), preferred_element_type=F32)


def _dot_nt(a, b):
    return lax.dot_general(a.astype(BF16), b.astype(BF16), (((1,), (1,)), ((), ())),
                           preferred_element_type=F32)


def _dot_tn(a, b):
    return lax.dot_general(a.astype(BF16), b.astype(BF16), (((0,), (0,)), ((), ())),
                           preferred_element_type=F32)


def _split_bf16(x, parts):
    out = []
    r = x
    for _ in range(parts - 1):
        p = r.astype(BF16)
        out.append(p)
        r = r - p.astype(F32)
    out.append(r.astype(BF16))
    return out


def _dot_exact_lhs(m_bf16, x, parts):
    acc = None
    for p in _split_bf16(x, parts):
        t = jnp.dot(m_bf16, p, preferred_element_type=F32)
        acc = t if acc is None else acc + t
    return acc


def _dot_exact_rhs(x, m_bf16, parts):
    acc = None
    for p in _split_bf16(x, parts):
        t = jnp.dot(p, m_bf16, preferred_element_type=F32)
        acc = t if acc is None else acc + t
    return acc


def _rms(xf, g):
    return xf * lax.rsqrt(jnp.mean(xf * xf, axis=-1, keepdims=True) + EPS) * g


def _softplus(x):
    return jnp.maximum(x, 0.0) + jnp.log1p(jnp.exp(-jnp.abs(x)))


def _log_sigmoid(x):
    return jnp.minimum(x, 0.0) - jnp.log1p(jnp.exp(-jnp.abs(x)))


def _sigmoid(x):
    return 1.0 / (1.0 + jnp.exp(-x))


def _silu(x):
    return x * _sigmoid(x)


def _iota2(n, axis):
    return lax.broadcasted_iota(jnp.int32, (n, n), axis)


def _const_spec(shape):
    return pl.BlockSpec(shape, lambda *_: (0,) * len(shape))


def _layer_spec(shape, layer, row_block=0):
    return pl.BlockSpec((None,) + tuple(shape), lambda *_: (layer, row_block, 0))


def _ffn_kernel(n_pre, *refs):
    x_ref = refs[0]
    pre = refs[1:1 + 2 * n_pre]
    g_ref, wgu_ref, wd_ref, o_ref = refs[1 + 2 * n_pre:]
    x = x_ref[...]
    for p in range(n_pre):
        x = x + jnp.dot(pre[2 * p][...], pre[2 * p + 1][...], preferred_element_type=F32)
    h = _rms(x, g_ref[...]).astype(BF16)
    gate = jnp.dot(h, wgu_ref[:, :D_FF], preferred_element_type=F32)
    up = jnp.dot(h, wgu_ref[:, D_FF:], preferred_element_type=F32)
    act = (_silu(gate) * up).astype(BF16)
    o_ref[...] = x + 0.5 * jnp.dot(act, wd_ref[...], preferred_element_type=F32)


def _ffn_call(x, pre, norm_g, wgu_all, wd_all, layer):
    t = x.shape[0]
    row = lambda i: (i, 0)
    in_specs = [pl.BlockSpec((FFN_ROWS, D_MODEL), row)]
    args = [x]
    for o, w_all, j, rb in pre:
        in_specs += [pl.BlockSpec((FFN_ROWS, o.shape[1]), row),
                     _layer_spec((o.shape[1], D_MODEL), j, rb)]
        args += [o, w_all]
    in_specs += [_const_spec((1, D_MODEL)), _layer_spec(wgu_all.shape[1:], layer),
                 _layer_spec(wd_all.shape[1:], layer)]
    args += [norm_g.reshape(1, D_MODEL), wgu_all, wd_all]
    return pl.pallas_call(
        functools.partial(_ffn_kernel, len(pre)),
        grid=(t // FFN_ROWS,),
        in_specs=in_specs,
        out_specs=pl.BlockSpec((FFN_ROWS, D_MODEL), row),
        out_shape=jax.ShapeDtypeStruct((t, D_MODEL), F32),
        compiler_params=pltpu.CompilerParams(
            dimension_semantics=("parallel",), vmem_limit_bytes=VMEM_LIMIT),
        name="ffn",
    )(*args)


def _inproj_even_kernel(x_ref, g_ref, w_ref, qg_ref, kg_ref,
                        qkv_ref, dgate_ref, fq_ref, fk_ref, fv_ref, fgate_ref, small_ref):
    h = _rms(x_ref[...], g_ref[...]).astype(BF16)
    c = 0

    def proj(width):
        nonlocal c
        y = jnp.dot(h, w_ref[:, c:c + width], preferred_element_type=F32)
        c += width
        return y

    qkv_ref[...] = proj(3 * D_DN)
    dgate_ref[...] = proj(D_DN)
    fq = proj(D_FOX)
    fk = proj(D_FOX)
    for hd in range(N_FOX_HEADS):
        sl = slice(hd * HEAD_DIM, (hd + 1) * HEAD_DIM)
        fq_ref[:, sl] = _rms(fq[:, sl], qg_ref[...]).astype(BF16)
        fk_ref[:, sl] = _rms(fk[:, sl], kg_ref[...]).astype(BF16)
    fv_ref[...] = proj(D_FOX).astype(BF16)
    fgate_ref[...] = proj(D_FOX)
    small_ref[...] = proj(LANES)


def _inproj_even_call(x, norm_g, w_all, layer, q_g, k_g):
    t = x.shape[0]
    row = lambda i: (i, 0)
    widths = [(3 * D_DN, F32), (D_DN, F32), (D_FOX, BF16), (D_FOX, BF16), (D_FOX, BF16),
              (D_FOX, F32), (LANES, F32)]
    return pl.pallas_call(
        _inproj_even_kernel,
        grid=(t // FFN_ROWS,),
        in_specs=[pl.BlockSpec((FFN_ROWS, D_MODEL), row), _const_spec((1, D_MODEL)),
                  _layer_spec(w_all.shape[1:], layer), _const_spec((1, HEAD_DIM)),
                  _const_spec((1, HEAD_DIM))],
        out_specs=[pl.BlockSpec((FFN_ROWS, n), row) for n, _ in widths],
        out_shape=[jax.ShapeDtypeStruct((t, n), dt) for n, dt in widths],
        compiler_params=pltpu.CompilerParams(
            dimension_semantics=("parallel",), vmem_limit_bytes=VMEM_LIMIT),
        name="inproj_even",
    )(x, norm_g.reshape(1, D_MODEL), w_all, q_g.reshape(1, HEAD_DIM), k_g.reshape(1, HEAD_DIM))


def _inproj_odd_kernel(x_ref, g_ref, w_ref, o_ref):
    h = _rms(x_ref[...], g_ref[...]).astype(BF16)
    o_ref[...] = jnp.dot(h, w_ref[...], preferred_element_type=F32).astype(BF16)


def _inproj_odd_call(x, norm_g, w_all, layer):
    t = x.shape[0]
    n = w_all.shape[2]
    row = lambda i: (i, 0)
    return pl.pallas_call(
        _inproj_odd_kernel,
        grid=(t // FFN_ROWS,),
        in_specs=[pl.BlockSpec((FFN_ROWS, D_MODEL), row), _const_spec((1, D_MODEL)),
                  _layer_spec(w_all.shape[1:], layer)],
        out_specs=pl.BlockSpec((FFN_ROWS, n), row),
        out_shape=jax.ShapeDtypeStruct((t, n), BF16),
        compiler_params=pltpu.CompilerParams(
            dimension_semantics=("parallel",), vmem_limit_bytes=VMEM_LIMIT),
        name="inproj_odd",
    )(x, norm_g.reshape(1, D_MODEL), w_all)


def _deltanet_kernel(qkv_ref, halo_ref, gate_ref, small_ref, convw_ref, alog_ref, dtb_ref,
                     ng_ref, o_ref, xx_ref, state_ref, sums_ref):
    i = pl.program_id(1)
    n = DN_ROWS
    g = DN_GROUP
    heads = range(N_DN_HEADS)
    groups = range(n // g)
    combos = [(gi, hd) for gi in groups for hd in heads]

    row = _iota2(g, 0)
    col = _iota2(g, 1)
    diff = row ^ col
    same = (diff >> (DN_CHUNK.bit_length() - 1)) == 0
    incl = same & (col <= row)

    @pl.when(i == 0)
    def _():
        state_ref[...] = jnp.zeros_like(state_ref)
        sums_ref[0:g, :] = jnp.where(incl, 1.0, 0.0).astype(BF16)
        sums_ref[g:2 * g, :] = jnp.where(same, 1.0, 0.0).astype(BF16)

    halo = halo_ref[...]
    xx_ref[0:SUBLANES, :] = jnp.where(i == 0, jnp.zeros_like(halo), halo)
    xx_ref[SUBLANES:SUBLANES + n, :] = qkv_ref[...]
    xx = xx_ref[...]
    y = xx[SUBLANES:] * convw_ref[CONV_WIDTH - 1:CONV_WIDTH, :]
    for d in range(1, CONV_WIDTH):
        tap = convw_ref[CONV_WIDTH - 1 - d:CONV_WIDTH - d, :]
        y = y + pltpu.roll(xx, d, axis=0)[SUBLANES:] * tap
    y = _silu(y)

    sm = small_ref[...]
    beta_all = _sigmoid(sm)
    g_all = -jnp.exp(alog_ref[...]) * _softplus(sm + dtb_ref[...])
    ones_sq = jnp.ones((HEAD_DIM, HEAD_DIM), BF16)

    def l2n(xf):
        return xf * lax.rsqrt(_dot_exact_rhs(xf * xf, ones_sq, 2) + EPS)

    hsl = lambda base_col, hd: slice(base_col + hd * HEAD_DIM, base_col + (hd + 1) * HEAD_DIM)
    rs = lambda gi: slice(gi * g, (gi + 1) * g)
    q_all = [l2n(y[:, hsl(0, hd)]) * SCALE for hd in heads]
    k_all = [l2n(y[:, hsl(D_DN, hd)]) for hd in heads]

    g_sums = [_dot_exact_lhs(sums_ref[...], g_all[rs(gi)], 3) for gi in groups]
    gc_all = [s[0:g] for s in g_sums]
    gl_all = [s[g:2 * g] for s in g_sums]
    gc_all_t = [c.T for c in gc_all]

    lane = lambda arr, hd: arr[:, LANE_DECAY + hd:LANE_DECAY + hd + 1]
    q = {(gi, hd): q_all[hd][rs(gi)] for gi, hd in combos}
    k = {(gi, hd): k_all[hd][rs(gi)] for gi, hd in combos}
    beta = {(gi, hd): beta_all[rs(gi), LANE_BETA + hd:LANE_BETA + hd + 1] for gi, hd in combos}
    gc = {(gi, hd): lane(gc_all[gi], hd) for gi, hd in combos}
    gl = {(gi, hd): lane(gl_all[gi], hd) for gi, hd in combos}
    egc = {c: jnp.exp(gc[c]) for c in combos}
    decay = {(gi, hd): jnp.where(
        incl, jnp.exp(gc[gi, hd] - gc_all_t[gi][LANE_DECAY + hd:LANE_DECAY + hd + 1, :]), 0.0)
        for gi, hd in combos}
    kb = {c: k[c] * beta[c] for c in combos}
    rhs = {(gi, hd): jnp.concatenate(
        [y[rs(gi), hsl(2 * D_DN, hd)] * beta[gi, hd], kb[gi, hd] * egc[gi, hd]], axis=1)
        for gi, hd in combos}
    on_diag = diff == 0
    a = {c: jnp.where(on_diag, 0.0, _dot_nt(kb[c], k[c]) * decay[c]) for c in combos}

    pair = diff == 1
    x = {c: jnp.where(on_diag, 1.0, jnp.where(pair, -a[c], 0.0)) for c in combos}
    b = 2
    while b < DN_CHUNK:
        off = (diff >> (b.bit_length() - 1)) == 1
        t = {c: _dot(x[c], jnp.where(off, a[c], 0.0)) for c in combos}
        x = {c: x[c] - _dot(t[c], x[c]) for c in combos}
        b *= 2
    sol = {c: _dot(x[c], rhs[c]) for c in combos}
    attn = {c: _dot_nt(q[c], k[c]) * decay[c] for c in combos}
    qd = {c: q[c] * egc[c] for c in combos}
    kd = {c: k[c] * jnp.exp(gl[c] - gc[c]) for c in combos}
    egl = {c: jnp.exp(gl[c]) for c in combos}

    state = [state_ref[hd] for hd in heads]
    for gi in groups:
        outs = [[] for _ in heads]
        for ch in range(g // DN_CHUNK):
            r = slice(ch * DN_CHUNK, (ch + 1) * DN_CHUNK)
            for hd in heads:
                c = (gi, hd)
                v_new = sol[c][r, :HEAD_DIM] - _dot(sol[c][r, HEAD_DIM:], state[hd])
                outs[hd].append(_dot(qd[c][r], state[hd]) + _dot(attn[c][r, r], v_new))
                state[hd] = (state[hd] * egl[c][ch * DN_CHUNK:ch * DN_CHUNK + 1, :]
                             + _dot_tn(kd[c][r], v_new))
        for hd in heads:
            o = jnp.concatenate(outs[hd], axis=0)
            sl = hsl(0, hd)
            o_ref[rs(gi), sl] = (_rms(o, ng_ref[...]) * _silu(gate_ref[rs(gi), sl])).astype(BF16)
    for hd in heads:
        state_ref[hd] = state[hd]


def _deltanet_call(qkv, gate, small, conv_w, a_log, dt_bias, norm_g):
    b, s, _ = qkv.shape
    per_blk = DN_ROWS // SUBLANES
    lane_vec = lambda v: jnp.zeros((1, LANES), F32).at[0, LANE_DECAY:LANE_DECAY + N_DN_HEADS].set(v)
    blk = lambda w: pl.BlockSpec((None, DN_ROWS, w), lambda bi, i: (bi, i, 0))
    return pl.pallas_call(
        _deltanet_kernel,
        grid=(b, s // DN_ROWS),
        in_specs=[blk(3 * D_DN),
                  pl.BlockSpec((None, SUBLANES, 3 * D_DN),
                               lambda bi, i: (bi, jnp.maximum(i * per_blk - 1, 0), 0)),
                  blk(D_DN), blk(LANES), _const_spec((CONV_WIDTH, 3 * D_DN)),
                  _const_spec((1, LANES)), _const_spec((1, LANES)), _const_spec((1, HEAD_DIM))],
        out_specs=blk(D_DN),
        out_shape=jax.ShapeDtypeStruct((b, s, D_DN), BF16),
        scratch_shapes=[pltpu.VMEM((SUBLANES + DN_ROWS, 3 * D_DN), F32),
                        pltpu.VMEM((N_DN_HEADS, HEAD_DIM, HEAD_DIM), F32),
                        pltpu.VMEM((2 * DN_GROUP, DN_GROUP), BF16)],
        compiler_params=pltpu.CompilerParams(
            dimension_semantics=("parallel", "arbitrary"), vmem_limit_bytes=VMEM_LIMIT),
        name="deltanet",
    )(qkv, qkv, gate, small, conv_w, lane_vec(a_log), lane_vec(dt_bias),
      norm_g.reshape(1, HEAD_DIM))


def _fox_kernel(q_ref, k_ref, v_ref, small_ref, fbias_ref, gate_ref, o_ref, crow_ref, vaug_ref):
    i = pl.program_id(1)
    n = FOX_ROWS
    s = k_ref.shape[0]
    row = _iota2(n, 0)
    col = _iota2(n, 1)
    causal = col <= row

    @pl.when(i == 0)
    def _():
        tri = jnp.where(causal, 1.0, 0.0).astype(BF16)
        carry = jnp.zeros((1, LANES), F32)
        for blk in range(s // n):
            r = slice(blk * n, (blk + 1) * n)
            lf = _log_sigmoid(small_ref[r, :] + fbias_ref[...])
            c = _dot_exact_lhs(tri, lf, 3) + carry
            carry = c[n - 1:n, :]
            crow_ref[:, r] = c.T
        for hd in range(N_FOX_HEADS):
            vaug_ref[:, 2 * hd * HEAD_DIM:(2 * hd + 1) * HEAD_DIM] = (
                v_ref[:, hd * HEAD_DIM:(hd + 1) * HEAD_DIM])
            vaug_ref[:, (2 * hd + 1) * HEAD_DIM:(2 * hd + 2) * HEAD_DIM] = (
                jnp.ones((s, HEAD_DIM), BF16))

    def head_step(hd, m, acc, k0, masked):
        sl = slice(hd * HEAD_DIM, (hd + 1) * HEAD_DIM)
        vsl = slice(2 * hd * HEAD_DIM, (2 * hd + 2) * HEAD_DIM)
        lane = LANE_FORGET + hd
        ck = crow_ref[lane:lane + 1, pl.ds(k0, n)]
        logits = _dot_nt(q_ref[:, sl], k_ref[pl.ds(k0, n), sl]) * SCALE - ck
        if masked:
            logits = jnp.where(causal, logits, -1e30)
        m_new = jnp.maximum(m, jnp.max(logits, axis=-1, keepdims=True))
        p = jnp.exp(logits - m_new)
        acc = jnp.exp(m - m_new) * acc + _dot(p, vaug_ref[pl.ds(k0, n), vsl])
        return m_new, acc

    for h0 in range(0, N_FOX_HEADS, FOX_HEADS_PER_LOOP):
        hds = range(h0, h0 + FOX_HEADS_PER_LOOP)

        def step(carry, k0, masked):
            out = []
            for idx, hd in enumerate(hds):
                out += head_step(hd, carry[2 * idx], carry[2 * idx + 1], k0, masked)
            return tuple(out)

        init = (jnp.full((n, 1), -1e30, F32),
                jnp.zeros((n, 2 * HEAD_DIM), F32)) * FOX_HEADS_PER_LOOP
        carry = lax.fori_loop(0, i, lambda j, c: step(c, pl.multiple_of(j * n, n), False), init)
        carry = step(carry, pl.multiple_of(i * n, n), True)
        for idx, hd in enumerate(hds):
            sl = slice(hd * HEAD_DIM, (hd + 1) * HEAD_DIM)
            acc = carry[2 * idx + 1]
            o = acc[:, :HEAD_DIM] / acc[:, HEAD_DIM:]
            o_ref[:, sl] = (o * _sigmoid(gate_ref[:, sl])).astype(BF16)


def _fox_call(q, k, v, small, f_bias, gate):
    b, s, _ = q.shape
    blk = lambda w: pl.BlockSpec((None, FOX_ROWS, w), lambda bi, i: (bi, i, 0))
    full = lambda w: pl.BlockSpec((None, s, w), lambda bi, i: (bi, 0, 0))
    fb = jnp.zeros((1, LANES), F32).at[0, LANE_FORGET:LANE_FORGET + N_FOX_HEADS].set(f_bias)
    return pl.pallas_call(
        _fox_kernel,
        grid=(b, s // FOX_ROWS),
        in_specs=[blk(D_FOX), full(D_FOX), full(D_FOX), full(LANES), _const_spec((1, LANES)),
                  blk(D_FOX)],
        out_specs=blk(D_FOX),
        out_shape=jax.ShapeDtypeStruct((b, s, D_FOX), BF16),
        scratch_shapes=[pltpu.VMEM((LANES, s), F32), pltpu.VMEM((s, 2 * D_FOX), BF16)],
        compiler_params=pltpu.CompilerParams(
            dimension_semantics=("parallel", "arbitrary"), vmem_limit_bytes=VMEM_LIMIT),
        name="fox_attention",
    )(q, k, v, small, fb, gate)


SB_CUTOFF = -105.0


def _sb_kernel(q_ref, k_ref, v_ref, o_ref):
    n = SB_ROWS
    s = k_ref.shape[0]
    row = _iota2(n, 0)
    col = _iota2(n, 1)
    before = col < row
    later = jnp.concatenate([jnp.where(row > col, 1.0, 0.0).astype(BF16),
                             jnp.ones((n, LANES), BF16)], axis=1)
    later2 = jnp.concatenate([later, later], axis=0)

    def block(q, k0, rest, masked):
        z = _dot_nt(q, k_ref[pl.ds(k0, n), :]) * SCALE
        soft = jnp.log(1.0 + jnp.exp(-jnp.abs(z)))
        lsz = jnp.minimum(z, 0.0) - soft
        lsn = lsz - z
        lg = jnp.where(before, lsn, 0.0) if masked else lsn
        hi = lg.astype(BF16)
        lo = (lg - hi.astype(F32)).astype(BF16)
        sums = jnp.dot(jnp.concatenate([hi, lo], axis=1), later2, preferred_element_type=F32)
        tail = sums[:, :n] + jnp.concatenate([rest] * (n // LANES), axis=1)
        a = jnp.exp(lsz + tail)
        if masked:
            a = jnp.where(before, a, 0.0)
        return _dot(a, v_ref[pl.ds(k0, n), :]), rest + sums[:, n:]

    def live(rest):
        return (jnp.max(rest) > SB_CUTOFF).astype(jnp.int32)

    def qblock(i, _):
        q0 = pl.multiple_of(i * n, n)
        q = q_ref[pl.ds(q0, n), :]
        acc, rest = block(q, q0, jnp.zeros((n, LANES), F32), True)

        def cond(c):
            return (c[0] < i) & (c[3] > 0)

        def body(c):
            t, acc, rest, _ = c
            k0 = pl.multiple_of((i - 1 - t) * n, n)
            pv, rest = block(q, k0, rest, False)
            return t + 1, acc + pv, rest, live(rest)

        _, acc, _, _ = lax.while_loop(cond, body, (jnp.int32(0), acc, rest, live(rest)))
        o_ref[pl.ds(q0, n), :] = acc.astype(BF16)
        return 0

    lax.fori_loop(0, s // n, qblock, 0)


def _sb_call(qkv):
    b, s, _ = qkv.shape
    nh = N_SB_HEADS
    head = lambda off: pl.BlockSpec((None, s, HEAD_DIM), lambda bi, h: (bi, 0, off + h))
    return pl.pallas_call(
        _sb_kernel,
        grid=(b, nh),
        in_specs=[head(0), head(nh), head(2 * nh)],
        out_specs=head(0),
        out_shape=jax.ShapeDtypeStruct((b, s, nh * HEAD_DIM), BF16),
        compiler_params=pltpu.CompilerParams(
            dimension_semantics=("parallel", "parallel"), vmem_limit_bytes=VMEM_LIMIT),
        name="stickbreaking_attention",
    )(qkv, qkv, qkv)


def _even_w_in(w):
    sizes = (3 * D_DN, D_DN, N_DN_HEADS, N_DN_HEADS, D_FOX, D_FOX, D_FOX, D_FOX, N_FOX_HEADS)
    parts, off = [], 0
    for sz in sizes:
        parts.append(w[..., off:off + sz])
        off += sz
    qkv, dgate, dbeta, ddecay, fq, fk, fv, fgate, fpre = parts
    pad = jnp.zeros(w.shape[:-1] + (LANES - 2 * N_DN_HEADS - N_FOX_HEADS,), w.dtype)
    small = jnp.concatenate([dbeta, ddecay, fpre, pad], axis=-1)
    return jnp.concatenate([qkv, dgate, fq, fk, fv, fgate, small], axis=-1)


def kernel(x, norm_ffn1, ffn1_w_gu, ffn1_w_down, norm_mix, w_in_even, dn_conv_w, dn_a_log,
           dn_dt_bias, dn_norm_g, fox_q_norm_g, fox_k_norm_g, fox_f_bias, w_out_even,
           w_in_odd, w_out_odd, norm_ffn2, ffn2_w_gu, ffn2_w_down):
    b, s, d = x.shape
    depth = norm_ffn1.shape[0]
    t = b * s
    bf = lambda a: a.astype(BF16)
    wgu1, wd1, wgu2, wd2 = bf(ffn1_w_gu), bf(ffn1_w_down), bf(ffn2_w_gu), bf(ffn2_w_down)
    w_in_e, w_out_e = bf(_even_w_in(w_in_even)), bf(w_out_even)
    w_in_o, w_out_o = bf(w_in_odd), bf(w_out_odd)
    x = x.reshape(t, d)
    for l in range(depth):
        j = l // 2
        x = _ffn_call(x, [], norm_ffn1[l], wgu1, wd1, l)
        if l % 2 == 0:
            qkv, dgate, fq, fk, fv, fgate, small = _inproj_even_call(
                x, norm_mix[l], w_in_e, j, fox_q_norm_g[j], fox_k_norm_g[j])
            r3 = lambda a: a.reshape(b, s, a.shape[-1])
            o_dn = _deltanet_call(r3(qkv), r3(dgate), r3(small), dn_conv_w[j], dn_a_log[j],
                                  dn_dt_bias[j], dn_norm_g[j])
            o_fox = _fox_call(r3(fq), r3(fk), r3(fv), r3(small), fox_f_bias[j], r3(fgate))
            pre = [(o_dn.reshape(t, D_DN), w_out_e, j, 0), (o_fox.reshape(t, D_FOX), w_out_e, j, 1)]
        else:
            qkv = _inproj_odd_call(x, norm_mix[l], w_in_o, j)
            o = _sb_call(qkv.reshape(b, s, qkv.shape[-1]))
            pre = [(o.reshape(t, o.shape[-1]), w_out_o, j, 0)]
        x = _ffn_call(x, pre, norm_ffn2[l], wgu2, wd2, l)
    return x.reshape(b, s, d)
```

```python
import functools

import jax
import jax.numpy as jnp
from jax import lax
from jax.experimental import pallas as pl
from jax.experimental.pallas import tpu as pltpu

F32 = jnp.float32
BF16 = jnp.bfloat16

D_MODEL = 1024
HEAD_DIM = 128
N_DN_HEADS = 4
N_FOX_HEADS = 4
N_SB_HEADS = 8
D_DN = N_DN_HEADS * HEAD_DIM
D_FOX = N_FOX_HEADS * HEAD_DIM
CONV_WIDTH = 4
D_FF = 2816
EPS = 1e-6
SCALE = HEAD_DIM ** -0.5

LANES = 128
SUBLANES = 8
VMEM_LIMIT = 56 * 1024 * 1024

FFN_ROWS = 512
SB_ROWS = 256
FOX_ROWS = 512
FOX_HEADS_PER_LOOP = 4
DN_ROWS = 512
DN_GROUP = 256
DN_CHUNK = 64

LANE_BETA = 0
LANE_DECAY = N_DN_HEADS
LANE_FORGET = 2 * N_DN_HEADS


def _dot(a, b):
    return jnp.dot(a.astype(BF16), b.astype(BF16), preferred_element_type=F32)


def _dot_nt(a, b):
    return lax.dot_general(a.astype(BF16), b.astype(BF16), (((1,), (1,)), ((), ())),
                           preferred_element_type=F32)


def _dot_tn(a, b):
    return lax.dot_general(a.astype(BF16), b.astype(BF16), (((0,), (0,)), ((), ())),
                           preferred_element_type=F32)


def _split_bf16(x, parts):
    out = []
    r = x
    for _ in range(parts - 1):
        p = r.astype(BF16)
        out.append(p)
        r = r - p.astype(F32)
    out.append(r.astype(BF16))
    return out


def _dot_exact_lhs(m_bf16, x, parts):
    acc = None
    for p in _split_bf16(x, parts):
        t = jnp.dot(m_bf16, p, preferred_element_type=F32)
        acc = t if acc is None else acc + t
    return acc


def _dot_exact_rhs(x, m_bf16, parts):
    acc = None
    for p in _split_bf16(x, parts):
        t = jnp.dot(p, m_bf16, preferred_element_type=F32)
        acc = t if acc is None else acc + t
    return acc


def _rms(xf, g):
    return xf * lax.rsqrt(jnp.mean(xf * xf, axis=-1, keepdims=True) + EPS) * g


def _softplus(x):
    return jnp.maximum(x, 0.0) + jnp.log1p(jnp.exp(-jnp.abs(x)))


def _log_sigmoid(x):
    return jnp.minimum(x, 0.0) - jnp.log1p(jnp.exp(-jnp.abs(x)))


def _sigmoid(x):
    return 1.0 / (1.0 + jnp.exp(-x))


def _silu(x):
    return x * _sigmoid(x)


def _iota2(n, axis):
    return lax.broadcasted_iota(jnp.int32, (n, n), axis)


def _const_spec(shape):
    return pl.BlockSpec(shape, lambda *_: (0,) * len(shape))


def _layer_spec(shape, layer, row_block=0):
    return pl.BlockSpec((None,) + tuple(shape), lambda *_: (layer, row_block, 0))


def _ffn_kernel(n_pre, *refs):
    x_ref = refs[0]
    pre = refs[1:1 + 2 * n_pre]
    g_ref, wgu_ref, wd_ref, o_ref = refs[1 + 2 * n_pre:]
    x = x_ref[...]
    for p in range(n_pre):
        x = x + jnp.dot(pre[2 * p][...], pre[2 * p + 1][...], preferred_element_type=F32)
    h = _rms(x, g_ref[...]).astype(BF16)
    gate = jnp.dot(h, wgu_ref[:, :D_FF], preferred_element_type=F32)
    up = jnp.dot(h, wgu_ref[:, D_FF:], preferred_element_type=F32)
    act = (_silu(gate) * up).astype(BF16)
    o_ref[...] = x + 0.5 * jnp.dot(act, wd_ref[...], preferred_element_type=F32)


def _ffn_call(x, pre, norm_g, wgu_all, wd_all, layer):
    t = x.shape[0]
    row = lambda i: (i, 0)
    in_specs = [pl.BlockSpec((FFN_ROWS, D_MODEL), row)]
    args = [x]
    for o, w_all, j, rb in pre:
        in_specs += [pl.BlockSpec((FFN_ROWS, o.shape[1]), row),
                     _layer_spec((o.shape[1], D_MODEL), j, rb)]
        args += [o, w_all]
    in_specs += [_const_spec((1, D_MODEL)), _layer_spec(wgu_all.shape[1:], layer),
                 _layer_spec(wd_all.shape[1:], layer)]
    args += [norm_g.reshape(1, D_MODEL), wgu_all, wd_all]
    return pl.pallas_call(
        functools.partial(_ffn_kernel, len(pre)),
        grid=(t // FFN_ROWS,),
        in_specs=in_specs,
        out_specs=pl.BlockSpec((FFN_ROWS, D_MODEL), row),
        out_shape=jax.ShapeDtypeStruct((t, D_MODEL), F32),
        compiler_params=pltpu.CompilerParams(
            dimension_semantics=("parallel",), vmem_limit_bytes=VMEM_LIMIT),
        name="ffn",
    )(*args)


def _inproj_even_kernel(x_ref, g_ref, w_ref, qg_ref, kg_ref,
                        qkv_ref, dgate_ref, fq_ref, fk_ref, fv_ref, fgate_ref, small_ref):
    h = _rms(x_ref[...], g_ref[...]).astype(BF16)
    c = 0

    def proj(width):
        nonlocal c
        y = jnp.dot(h, w_ref[:, c:c + width], preferred_element_type=F32)
        c += width
        return y

    qkv_ref[...] = proj(3 * D_DN)
    dgate_ref[...] = proj(D_DN)
    fq = proj(D_FOX)
    fk = proj(D_FOX)
    for hd in range(N_FOX_HEADS):
        sl = slice(hd * HEAD_DIM, (hd + 1) * HEAD_DIM)
        fq_ref[:, sl] = _rms(fq[:, sl], qg_ref[...]).astype(BF16)
        fk_ref[:, sl] = _rms(fk[:, sl], kg_ref[...]).astype(BF16)
    fv_ref[...] = proj(D_FOX).astype(BF16)
    fgate_ref[...] = proj(D_FOX)
    small_ref[...] = proj(LANES)


def _inproj_even_call(x, norm_g, w_all, layer, q_g, k_g):
    t = x.shape[0]
    row = lambda i: (i, 0)
    widths = [(3 * D_DN, F32), (D_DN, F32), (D_FOX, BF16), (D_FOX, BF16), (D_FOX, BF16),
              (D_FOX, F32), (LANES, F32)]
    return pl.pallas_call(
        _inproj_even_kernel,
        grid=(t // FFN_ROWS,),
        in_specs=[pl.BlockSpec((FFN_ROWS, D_MODEL), row), _const_spec((1, D_MODEL)),
                  _layer_spec(w_all.shape[1:], layer), _const_spec((1, HEAD_DIM)),
                  _const_spec((1, HEAD_DIM))],
        out_specs=[pl.BlockSpec((FFN_ROWS, n), row) for n, _ in widths],
        out_shape=[jax.ShapeDtypeStruct((t, n), dt) for n, dt in widths],
        compiler_params=pltpu.CompilerParams(
            dimension_semantics=("parallel",), vmem_limit_bytes=VMEM_LIMIT),
        name="inproj_even",
    )(x, norm_g.reshape(1, D_MODEL), w_all, q_g.reshape(1, HEAD_DIM), k_g.reshape(1, HEAD_DIM))


def _inproj_odd_kernel(x_ref, g_ref, w_ref, o_ref):
    h = _rms(x_ref[...], g_ref[...]).astype(BF16)
    o_ref[...] = jnp.dot(h, w_ref[...], preferred_element_type=F32).astype(BF16)


def _inproj_odd_call(x, norm_g, w_all, layer):
    t = x.shape[0]
    n = w_all.shape[2]
    row = lambda i: (i, 0)
    return pl.pallas_call(
        _inproj_odd_kernel,
        grid=(t // FFN_ROWS,),
        in_specs=[pl.BlockSpec((FFN_ROWS, D_MODEL), row), _const_spec((1, D_MODEL)),
                  _layer_spec(w_all.shape[1:], layer)],
        out_specs=pl.BlockSpec((FFN_ROWS, n), row),
        out_shape=jax.ShapeDtypeStruct((t, n), BF16),
        compiler_params=pltpu.CompilerParams(
            dimension_semantics=("parallel",), vmem_limit_bytes=VMEM_LIMIT),
        name="inproj_odd",
    )(x, norm_g.reshape(1, D_MODEL), w_all)


def _deltanet_kernel(qkv_ref, halo_ref, gate_ref, small_ref, convw_ref, alog_ref, dtb_ref,
                     ng_ref, o_ref, xx_ref, state_ref, sums_ref):
    i = pl.program_id(1)
    n = DN_ROWS
    g = DN_GROUP
    heads = range(N_DN_HEADS)
    groups = range(n // g)
    combos = [(gi, hd) for gi in groups for hd in heads]

    row = _iota2(g, 0)
    col = _iota2(g, 1)
    diff = row ^ col
    same = (diff >> (DN_CHUNK.bit_length() - 1)) == 0
    incl = same & (col <= row)

    @pl.when(i == 0)
    def _():
        state_ref[...] = jnp.zeros_like(state_ref)
        sums_ref[0:g, :] = jnp.where(incl, 1.0, 0.0).astype(BF16)
        sums_ref[g:2 * g, :] = jnp.where(same, 1.0, 0.0).astype(BF16)

    halo = halo_ref[...]
    xx_ref[0:SUBLANES, :] = jnp.where(i == 0, jnp.zeros_like(halo), halo)
    xx_ref[SUBLANES:SUBLANES + n, :] = qkv_ref[...]
    xx = xx_ref[...]
    y = xx[SUBLANES:] * convw_ref[CONV_WIDTH - 1:CONV_WIDTH, :]
    for d in range(1, CONV_WIDTH):
        tap = convw_ref[CONV_WIDTH - 1 - d:CONV_WIDTH - d, :]
        y = y + pltpu.roll(xx, d, axis=0)[SUBLANES:] * tap
    y = _silu(y)

    sm = small_ref[...]
    beta_all = _sigmoid(sm)
    g_all = -jnp.exp(alog_ref[...]) * _softplus(sm + dtb_ref[...])
    ones_sq = jnp.ones((HEAD_DIM, HEAD_DIM), BF16)

    def l2n(xf):
        return xf * lax.rsqrt(_dot_exact_rhs(xf * xf, ones_sq, 2) + EPS)

    hsl = lambda base_col, hd: slice(base_col + hd * HEAD_DIM, base_col + (hd + 1) * HEAD_DIM)
    rs = lambda gi: slice(gi * g, (gi + 1) * g)
    q_all = [l2n(y[:, hsl(0, hd)]) * SCALE for hd in heads]
    k_all = [l2n(y[:, hsl(D_DN, hd)]) for hd in heads]

    g_sums = [_dot_exact_lhs(sums_ref[...], g_all[rs(gi)], 3) for gi in groups]
    gc_all = [s[0:g] for s in g_sums]
    gl_all = [s[g:2 * g] for s in g_sums]
    gc_all_t = [c.T for c in gc_all]

    lane = lambda arr, hd: arr[:, LANE_DECAY + hd:LANE_DECAY + hd + 1]
    q = {(gi, hd): q_all[hd][rs(gi)] for gi, hd in combos}
    k = {(gi, hd): k_all[hd][rs(gi)] for gi, hd in combos}
    beta = {(gi, hd): beta_all[rs(gi), LANE_BETA + hd:LANE_BETA + hd + 1] for gi, hd in combos}
    gc = {(gi, hd): lane(gc_all[gi], hd) for gi, hd in combos}
    gl = {(gi, hd): lane(gl_all[gi], hd) for gi, hd in combos}
    egc = {c: jnp.exp(gc[c]) for c in combos}
    decay = {(gi, hd): jnp.where(
        incl, jnp.exp(gc[gi, hd] - gc_all_t[gi][LANE_DECAY + hd:LANE_DECAY + hd + 1, :]), 0.0)
        for gi, hd in combos}
    kb = {c: k[c] * beta[c] for c in combos}
    rhs = {(gi, hd): jnp.concatenate(
        [y[rs(gi), hsl(2 * D_DN, hd)] * beta[gi, hd], kb[gi, hd] * egc[gi, hd]], axis=1)
        for gi, hd in combos}
    on_diag = diff == 0
    a = {c: jnp.where(on_diag, 0.0, _dot_nt(kb[c], k[c]) * decay[c]) for c in combos}

    pair = diff == 1
    x = {c: jnp.where(on_diag, 1.0, jnp.where(pair, -a[c], 0.0)) for c in combos}
    b = 2
    while b < DN_CHUNK:
        off = (diff >> (b.bit_length() - 1)) == 1
        t = {c: _dot(x[c], jnp.where(off, a[c], 0.0)) for c in combos}
        x = {c: x[c] - _dot(t[c], x[c]) for c in combos}
        b *= 2
    sol = {c: _dot(x[c], rhs[c]) for c in combos}
    attn = {c: _dot_nt(q[c], k[c]) * decay[c] for c in combos}
    qd = {c: q[c] * egc[c] for c in combos}
    kd = {c: k[c] * jnp.exp(gl[c] - gc[c]) for c in combos}
    egl = {c: jnp.exp(gl[c]) for c in combos}

    state = [state_ref[hd] for hd in heads]
    for gi in groups:
        outs = [[] for _ in heads]
        for ch in range(g // DN_CHUNK):
            r = slice(ch * DN_CHUNK, (ch + 1) * DN_CHUNK)
            for hd in heads:
                c = (gi, hd)
                v_new = sol[c][r, :HEAD_DIM] - _dot(sol[c][r, HEAD_DIM:], state[hd])
                outs[hd].append(_dot(qd[c][r], state[hd]) + _dot(attn[c][r, r], v_new))
                state[hd] = (state[hd] * egl[c][ch * DN_CHUNK:ch * DN_CHUNK + 1, :]
                             + _dot_tn(kd[c][r], v_new))
        for hd in heads:
            o = jnp.concatenate(outs[hd], axis=0)
            sl = hsl(0, hd)
            o_ref[rs(gi), sl] = (_rms(o, ng_ref[...]) * _silu(gate_ref[rs(gi), sl])).astype(BF16)
    for hd in heads:
        state_ref[hd] = state[hd]


def _deltanet_call(qkv, gate, small, conv_w, a_log, dt_bias, norm_g):
    b, s, _ = qkv.shape
    per_blk = DN_ROWS // SUBLANES
    lane_vec = lambda v: jnp.zeros((1, LANES), F32).at[0, LANE_DECAY:LANE_DECAY + N_DN_HEADS].set(v)
    blk = lambda w: pl.BlockSpec((None, DN_ROWS, w), lambda bi, i: (bi, i, 0))
    return pl.pallas_call(
        _deltanet_kernel,
        grid=(b, s // DN_ROWS),
        in_specs=[blk(3 * D_DN),
                  pl.BlockSpec((None, SUBLANES, 3 * D_DN),
                               lambda bi, i: (bi, jnp.maximum(i * per_blk - 1, 0), 0)),
                  blk(D_DN), blk(LANES), _const_spec((CONV_WIDTH, 3 * D_DN)),
                  _const_spec((1, LANES)), _const_spec((1, LANES)), _const_spec((1, HEAD_DIM))],
        out_specs=blk(D_DN),
        out_shape=jax.ShapeDtypeStruct((b, s, D_DN), BF16),
        scratch_shapes=[pltpu.VMEM((SUBLANES + DN_ROWS, 3 * D_DN), F32),
                        pltpu.VMEM((N_DN_HEADS, HEAD_DIM, HEAD_DIM), F32),
                        pltpu.VMEM((2 * DN_GROUP, DN_GROUP), BF16)],
        compiler_params=pltpu.CompilerParams(
            dimension_semantics=("parallel", "arbitrary"), vmem_limit_bytes=VMEM_LIMIT),
        name="deltanet",
    )(qkv, qkv, gate, small, conv_w, lane_vec(a_log), lane_vec(dt_bias),
      norm_g.reshape(1, HEAD_DIM))


def _fox_kernel(q_ref, k_ref, v_ref, small_ref, fbias_ref, gate_ref, o_ref, crow_ref, vaug_ref):
    i = pl.program_id(1)
    n = FOX_ROWS
    s = k_ref.shape[0]
    row = _iota2(n, 0)
    col = _iota2(n, 1)
    causal = col <= row

    @pl.when(i == 0)
    def _():
        tri = jnp.where(causal, 1.0, 0.0).astype(BF16)
        carry = jnp.zeros((1, LANES), F32)
        for blk in range(s // n):
            r = slice(blk * n, (blk + 1) * n)
            lf = _log_sigmoid(small_ref[r, :] + fbias_ref[...])
            c = _dot_exact_lhs(tri, lf, 3) + carry
            carry = c[n - 1:n, :]
            crow_ref[:, r] = c.T
        for hd in range(N_FOX_HEADS):
            vaug_ref[:, 2 * hd * HEAD_DIM:(2 * hd + 1) * HEAD_DIM] = (
                v_ref[:, hd * HEAD_DIM:(hd + 1) * HEAD_DIM])
            vaug_ref[:, (2 * hd + 1) * HEAD_DIM:(2 * hd + 2) * HEAD_DIM] = (
                jnp.ones((s, HEAD_DIM), BF16))

    def head_step(hd, m, acc, k0, masked):
        sl = slice(hd * HEAD_DIM, (hd + 1) * HEAD_DIM)
        vsl = slice(2 * hd * HEAD_DIM, (2 * hd + 2) * HEAD_DIM)
        lane = LANE_FORGET + hd
        ck = crow_ref[lane:lane + 1, pl.ds(k0, n)]
        logits = _dot_nt(q_ref[:, sl], k_ref[pl.ds(k0, n), sl]) * SCALE - ck
        if masked:
            logits = jnp.where(causal, logits, -1e30)
        m_new = jnp.maximum(m, jnp.max(logits, axis=-1, keepdims=True))
        p = jnp.exp(logits - m_new)
        acc = jnp.exp(m - m_new) * acc + _dot(p, vaug_ref[pl.ds(k0, n), vsl])
        return m_new, acc

    for h0 in range(0, N_FOX_HEADS, FOX_HEADS_PER_LOOP):
        hds = range(h0, h0 + FOX_HEADS_PER_LOOP)

        def step(carry, k0, masked):
            out = []
            for idx, hd in enumerate(hds):
                out += head_step(hd, carry[2 * idx], carry[2 * idx + 1], k0, masked)
            return tuple(out)

        init = (jnp.full((n, 1), -1e30, F32),
                jnp.zeros((n, 2 * HEAD_DIM), F32)) * FOX_HEADS_PER_LOOP
        carry = lax.fori_loop(0, i, lambda j, c: step(c, pl.multiple_of(j * n, n), False), init)
        carry = step(carry, pl.multiple_of(i * n, n), True)
        for idx, hd in enumerate(hds):
            sl = slice(hd * HEAD_DIM, (hd + 1) * HEAD_DIM)
            acc = carry[2 * idx + 1]
            o = acc[:, :HEAD_DIM] / acc[:, HEAD_DIM:]
            o_ref[:, sl] = (o * _sigmoid(gate_ref[:, sl])).astype(BF16)


def _fox_call(q, k, v, small, f_bias, gate):
    b, s, _ = q.shape
    blk = lambda w: pl.BlockSpec((None, FOX_ROWS, w), lambda bi, i: (bi, i, 0))
    full = lambda w: pl.BlockSpec((None, s, w), lambda bi, i: (bi, 0, 0))
    fb = jnp.zeros((1, LANES), F32).at[0, LANE_FORGET:LANE_FORGET + N_FOX_HEADS].set(f_bias)
    return pl.pallas_call(
        _fox_kernel,
        grid=(b, s // FOX_ROWS),
        in_specs=[blk(D_FOX), full(D_FOX), full(D_FOX), full(LANES), _const_spec((1, LANES)),
                  blk(D_FOX)],
        out_specs=blk(D_FOX),
        out_shape=jax.ShapeDtypeStruct((b, s, D_FOX), BF16),
        scratch_shapes=[pltpu.VMEM((LANES, s), F32), pltpu.VMEM((s, 2 * D_FOX), BF16)],
        compiler_params=pltpu.CompilerParams(
            dimension_semantics=("parallel", "arbitrary"), vmem_limit_bytes=VMEM_LIMIT),
        name="fox_attention",
    )(q, k, v, small, fb, gate)


SB_CUTOFF = -105.0


def _sb_kernel(q_ref, k_ref, v_ref, o_ref):
    n = SB_ROWS
    s = k_ref.shape[0]
    row = _iota2(n, 0)
    col = _iota2(n, 1)
    before = col < row
    later = jnp.concatenate([jnp.where(row > col, 1.0, 0.0).astype(BF16),
                             jnp.ones((n, LANES), BF16)], axis=1)
    later2 = jnp.concatenate([later, later], axis=0)

    def block(q, k0, rest, masked):
        z = _dot_nt(q, k_ref[pl.ds(k0, n), :]) * SCALE
        soft = jnp.log(1.0 + jnp.exp(-jnp.abs(z)))
        lsz = jnp.minimum(z, 0.0) - soft
        lsn = lsz - z
        lg = jnp.where(before, lsn, 0.0) if masked else lsn
        hi = lg.astype(BF16)
        lo = (lg - hi.astype(F32)).astype(BF16)
        sums = jnp.dot(jnp.concatenate([hi, lo], axis=1), later2, preferred_element_type=F32)
        tail = sums[:, :n] + jnp.concatenate([rest] * (n // LANES), axis=1)
        a = jnp.exp(lsz + tail)
        if masked:
            a = jnp.where(before, a, 0.0)
        return _dot(a, v_ref[pl.ds(k0, n), :]), rest + sums[:, n:]

    def pair(q, k_lo, rest, masked):
        z = _dot_nt(q, k_ref[pl.ds(k_lo, 2 * n), :]) * SCALE
        soft = jnp.log(1.0 + jnp.exp(-jnp.abs(z)))
        lsz = jnp.minimum(z, 0.0) - soft
        lsn = lsz - z
        lg_r = jnp.where(before, lsn[:, n:], 0.0) if masked else lsn[:, n:]
        lg = jnp.concatenate([lg_r, lsn[:, :n]], axis=0)
        hi = lg.astype(BF16)
        lo = (lg - hi.astype(F32)).astype(BF16)
        sums = jnp.dot(jnp.concatenate([hi, lo], axis=1), later2, preferred_element_type=F32)
        rest_mid = rest + sums[:n, n:]
        rep = lambda r: jnp.concatenate([r] * (n // LANES), axis=1)
        tail = jnp.concatenate([sums[n:, :n] + rep(rest_mid), sums[:n, :n] + rep(rest)], axis=1)
        a = jnp.exp(lsz + tail)
        if masked:
            a = jnp.concatenate([a[:, :n], jnp.where(before, a[:, n:], 0.0)], axis=1)
        return _dot(a, v_ref[pl.ds(k_lo, 2 * n), :]), rest_mid + sums[n:, n:]

    def live(rest):
        return (jnp.max(rest) > SB_CUTOFF).astype(jnp.int32)

    acc0, _ = block(q_ref[0:n, :], 0, jnp.zeros((n, LANES), F32), True)
    o_ref[0:n, :] = acc0.astype(BF16)

    def qblock(i, _):
        q0 = pl.multiple_of(i * n, n)
        q = q_ref[pl.ds(q0, n), :]
        acc, rest = pair(q, pl.multiple_of(q0 - n, n), jnp.zeros((n, LANES), F32), True)
        n_pairs = (i - 1) // 2

        def cond(c):
            return (c[0] < n_pairs) & (c[3] > 0)

        def body(c):
            t, acc, rest, _ = c
            k_lo = pl.multiple_of((i - 3 - 2 * t) * n, n)
            pv, rest = pair(q, k_lo, rest, False)
            return t + 1, acc + pv, rest, live(rest)

        _, acc, rest, alive = lax.while_loop(cond, body, (jnp.int32(0), acc, rest, live(rest)))
        acc = lax.cond(((i - 1) % 2 == 1) & (alive > 0),
                       lambda: acc + block(q, 0, rest, False)[0], lambda: acc)
        o_ref[pl.ds(q0, n), :] = acc.astype(BF16)
        return 0

    lax.fori_loop(1, s // n, qblock, 0)


def _sb_call(qkv):
    b, s, _ = qkv.shape
    nh = N_SB_HEADS
    head = lambda off: pl.BlockSpec((None, s, HEAD_DIM), lambda bi, h: (bi, 0, off + h))
    return pl.pallas_call(
        _sb_kernel,
        grid=(b, nh),
        in_specs=[head(0), head(nh), head(2 * nh)],
        out_specs=head(0),
        out_shape=jax.ShapeDtypeStruct((b, s, nh * HEAD_DIM), BF16),
        compiler_params=pltpu.CompilerParams(
            dimension_semantics=("parallel", "parallel"), vmem_limit_bytes=VMEM_LIMIT),
        name="stickbreaking_attention",
    )(qkv, qkv, qkv)


def _even_w_in(w):
    sizes = (3 * D_DN, D_DN, N_DN_HEADS, N_DN_HEADS, D_FOX, D_FOX, D_FOX, D_FOX, N_FOX_HEADS)
    parts, off = [], 0
    for sz in sizes:
        parts.append(w[..., off:off + sz])
        off += sz
    qkv, dgate, dbeta, ddecay, fq, fk, fv, fgate, fpre = parts
    pad = jnp.zeros(w.shape[:-1] + (LANES - 2 * N_DN_HEADS - N_FOX_HEADS,), w.dtype)
    small = jnp.concatenate([dbeta, ddecay, fpre, pad], axis=-1)
    return jnp.concatenate([qkv, dgate, fq, fk, fv, fgate, small], axis=-1)


def kernel(x, norm_ffn1, ffn1_w_gu, ffn1_w_down, norm_mix, w_in_even, dn_conv_w, dn_a_log,
           dn_dt_bias, dn_norm_g, fox_q_norm_g, fox_k_norm_g, fox_f_bias, w_out_even,
           w_in_odd, w_out_odd, norm_ffn2, ffn2_w_gu, ffn2_w_down):
    b, s, d = x.shape
    depth = norm_ffn1.shape[0]
    t = b * s
    bf = lambda a: a.astype(BF16)
    wgu1, wd1, wgu2, wd2 = bf(ffn1_w_gu), bf(ffn1_w_down), bf(ffn2_w_gu), bf(ffn2_w_down)
    w_in_e, w_out_e = bf(_even_w_in(w_in_even)), bf(w_out_even)
    w_in_o, w_out_o = bf(w_in_odd), bf(w_out_odd)
    x = x.reshape(t, d)
    for l in range(depth):
        j = l // 2
        x = _ffn_call(x, [], norm_ffn1[l], wgu1, wd1, l)
        if l % 2 == 0:
            qkv, dgate, fq, fk, fv, fgate, small = _inproj_even_call(
                x, norm_mix[l], w_in_e, j, fox_q_norm_g[j], fox_k_norm_g[j])
            r3 = lambda a: a.reshape(b, s, a.shape[-1])
            o_dn = _deltanet_call(r3(qkv), r3(dgate), r3(small), dn_conv_w[j], dn_a_log[j],
                                  dn_dt_bias[j], dn_norm_g[j])
            o_fox = _fox_call(r3(fq), r3(fk), r3(fv), r3(small), fox_f_bias[j], r3(fgate))
            pre = [(o_dn.reshape(t, D_DN), w_out_e, j, 0), (o_fox.reshape(t, D_FOX), w_out_e, j, 1)]
        else:
            qkv = _inproj_odd_call(x, norm_mix[l], w_in_o, j)
            o = _sb_call(qkv.reshape(b, s, qkv.shape[-1]))
            pre = [(o.reshape(t, o.shape[-1]), w_out_o, j, 0)]
        x = _ffn_call(x, pre, norm_ffn2[l], wgu2, wd2, l)
    return x.reshape(b, s, d)
```

```python
import functools

import jax
import jax.numpy as jnp
from jax import lax
from jax.experimental import pallas as pl
from jax.experimental.pallas import tpu as pltpu

F32 = jnp.float32
BF16 = jnp.bfloat16

D_MODEL = 1024
HEAD_DIM = 128
N_DN_HEADS = 4
N_FOX_HEADS = 4
N_SB_HEADS = 8
D_DN = N_DN_HEADS * HEAD_DIM
D_FOX = N_FOX_HEADS * HEAD_DIM
CONV_WIDTH = 4
D_FF = 2816
EPS = 1e-6
SCALE = HEAD_DIM ** -0.5

LANES = 128
SUBLANES = 8
VMEM_LIMIT = 56 * 1024 * 1024

FFN_ROWS = 512
SB_ROWS = 256
FOX_ROWS = 512
FOX_HEADS_PER_LOOP = 4
DN_ROWS = 512
DN_GROUP = 256
DN_CHUNK = 64

LANE_BETA = 0
LANE_DECAY = N_DN_HEADS
LANE_FORGET = 2 * N_DN_HEADS


def _dot(a, b):
    return jnp.dot(a.astype(BF16), b.astype(BF16), preferred_element_type=F32)


def _dot_nt(a, b):
    return lax.dot_general(a.astype(BF16), b.astype(BF16), (((1,), (1,)), ((), ())),
                           preferred_element_type=F32)


def _dot_tn(a, b):
    return lax.dot_general(a.astype(BF16), b.astype(BF16), (((0,), (0,)), ((), ())),
                           preferred_element_type=F32)


def _split_bf16(x, parts):
    out = []
    r = x
    for _ in range(parts - 1):
        p = r.astype(BF16)
        out.append(p)
        r = r - p.astype(F32)
    out.append(r.astype(BF16))
    return out


def _dot_exact_lhs(m_bf16, x, parts):
    acc = None
    for p in _split_bf16(x, parts):
        t = jnp.dot(m_bf16, p, preferred_element_type=F32)
        acc = t if acc is None else acc + t
    return acc


def _dot_exact_rhs(x, m_bf16, parts):
    acc = None
    for p in _split_bf16(x, parts):
        t = jnp.dot(p, m_bf16, preferred_element_type=F32)
        acc = t if acc is None else acc + t
    return acc


def _rms(xf, g):
    return xf * lax.rsqrt(jnp.mean(xf * xf, axis=-1, keepdims=True) + EPS) * g


def _softplus(x):
    return jnp.maximum(x, 0.0) + jnp.log1p(jnp.exp(-jnp.abs(x)))


def _log_sigmoid(x):
    return jnp.minimum(x, 0.0) - jnp.log1p(jnp.exp(-jnp.abs(x)))


def _sigmoid(x):
    return 1.0 / (1.0 + jnp.exp(-x))


def _silu(x):
    return x * _sigmoid(x)


def _iota2(n, axis):
    return lax.broadcasted_iota(jnp.int32, (n, n), axis)


def _const_spec(shape):
    return pl.BlockSpec(shape, lambda *_: (0,) * len(shape))


def _layer_spec(shape, layer, row_block=0):
    return pl.BlockSpec((None,) + tuple(shape), lambda *_: (layer, row_block, 0))


def _ffn_kernel(n_pre, *refs):
    x_ref = refs[0]
    pre = refs[1:1 + 2 * n_pre]
    g_ref, wgu_ref, wd_ref, o_ref = refs[1 + 2 * n_pre:]
    x = x_ref[...]
    for p in range(n_pre):
        x = x + jnp.dot(pre[2 * p][...], pre[2 * p + 1][...], preferred_element_type=F32)
    h = _rms(x, g_ref[...]).astype(BF16)
    gate = jnp.dot(h, wgu_ref[:, :D_FF], preferred_element_type=F32)
    up = jnp.dot(h, wgu_ref[:, D_FF:], preferred_element_type=F32)
    act = (_silu(gate) * up).astype(BF16)
    o_ref[...] = x + 0.5 * jnp.dot(act, wd_ref[...], preferred_element_type=F32)


def _ffn_call(x, pre, norm_g, wgu_all, wd_all, layer):
    t = x.shape[0]
    row = lambda i: (i, 0)
    in_specs = [pl.BlockSpec((FFN_ROWS, D_MODEL), row)]
    args = [x]
    for o, w_all, j, rb in pre:
        in_specs += [pl.BlockSpec((FFN_ROWS, o.shape[1]), row),
                     _layer_spec((o.shape[1], D_MODEL), j, rb)]
        args += [o, w_all]
    in_specs += [_const_spec((1, D_MODEL)), _layer_spec(wgu_all.shape[1:], layer),
                 _layer_spec(wd_all.shape[1:], layer)]
    args += [norm_g.reshape(1, D_MODEL), wgu_all, wd_all]
    return pl.pallas_call(
        functools.partial(_ffn_kernel, len(pre)),
        grid=(t // FFN_ROWS,),
        in_specs=in_specs,
        out_specs=pl.BlockSpec((FFN_ROWS, D_MODEL), row),
        out_shape=jax.ShapeDtypeStruct((t, D_MODEL), F32),
        compiler_params=pltpu.CompilerParams(
            dimension_semantics=("parallel",), vmem_limit_bytes=VMEM_LIMIT),
        name="ffn",
    )(*args)


def _inproj_even_kernel(tiles_per_seq, x_ref, g_ref, w_ref, qg_ref, kg_ref, convw_ref,
                        qkv_ref, dgate_ref, fq_ref, fk_ref, fv_ref, fgate_ref, small_ref,
                        xx_ref):
    i = pl.program_id(0)
    n = FFN_ROWS
    h = _rms(x_ref[...], g_ref[...]).astype(BF16)
    c = 0

    def proj(width):
        nonlocal c
        y = jnp.dot(h, w_ref[:, c:c + width], preferred_element_type=F32)
        c += width
        return y

    @pl.when(i % tiles_per_seq == 0)
    def _():
        xx_ref[n:n + SUBLANES, :] = jnp.zeros((SUBLANES, 3 * D_DN), F32)

    xx_ref[0:SUBLANES, :] = xx_ref[n:n + SUBLANES, :]
    xx_ref[SUBLANES:SUBLANES + n, :] = proj(3 * D_DN)
    xx = xx_ref[...]
    y = xx[SUBLANES:] * convw_ref[CONV_WIDTH - 1:CONV_WIDTH, :]
    for d in range(1, CONV_WIDTH):
        tap = convw_ref[CONV_WIDTH - 1 - d:CONV_WIDTH - d, :]
        y = y + pltpu.roll(xx, d, axis=0)[SUBLANES:] * tap
    y = _silu(y)
    ones_sq = jnp.ones((HEAD_DIM, HEAD_DIM), BF16)
    for hd in range(N_DN_HEADS):
        for base_col, mult in ((0, SCALE), (D_DN, 1.0)):
            sl = slice(base_col + hd * HEAD_DIM, base_col + (hd + 1) * HEAD_DIM)
            yf = y[:, sl]
            ss = _dot_exact_rhs(yf * yf, ones_sq, 2)
            qkv_ref[:, sl] = yf * (lax.rsqrt(ss + EPS) * mult)
    qkv_ref[:, 2 * D_DN:] = y[:, 2 * D_DN:]
    dgate_ref[...] = proj(D_DN)
    fq = proj(D_FOX)
    fk = proj(D_FOX)
    for hd in range(N_FOX_HEADS):
        sl = slice(hd * HEAD_DIM, (hd + 1) * HEAD_DIM)
        fq_ref[:, sl] = _rms(fq[:, sl], qg_ref[...]).astype(BF16)
        fk_ref[:, sl] = _rms(fk[:, sl], kg_ref[...]).astype(BF16)
    fv_ref[...] = proj(D_FOX).astype(BF16)
    fgate_ref[...] = proj(D_FOX)
    small_ref[...] = proj(LANES)


def _inproj_even_call(x, seq_len, norm_g, w_all, layer, q_g, k_g, conv_w):
    t = x.shape[0]
    row = lambda i: (i, 0)
    widths = [(3 * D_DN, F32), (D_DN, F32), (D_FOX, BF16), (D_FOX, BF16), (D_FOX, BF16),
              (D_FOX, F32), (LANES, F32)]
    return pl.pallas_call(
        functools.partial(_inproj_even_kernel, seq_len // FFN_ROWS),
        grid=(t // FFN_ROWS,),
        in_specs=[pl.BlockSpec((FFN_ROWS, D_MODEL), row), _const_spec((1, D_MODEL)),
                  _layer_spec(w_all.shape[1:], layer), _const_spec((1, HEAD_DIM)),
                  _const_spec((1, HEAD_DIM)), _const_spec((CONV_WIDTH, 3 * D_DN))],
        out_specs=[pl.BlockSpec((FFN_ROWS, n), row) for n, _ in widths],
        out_shape=[jax.ShapeDtypeStruct((t, n), dt) for n, dt in widths],
        scratch_shapes=[pltpu.VMEM((SUBLANES + FFN_ROWS, 3 * D_DN), F32)],
        compiler_params=pltpu.CompilerParams(
            dimension_semantics=("arbitrary",), vmem_limit_bytes=VMEM_LIMIT),
        name="inproj_even",
    )(x, norm_g.reshape(1, D_MODEL), w_all, q_g.reshape(1, HEAD_DIM), k_g.reshape(1, HEAD_DIM),
      conv_w)


def _inproj_odd_kernel(x_ref, g_ref, w_ref, o_ref):
    h = _rms(x_ref[...], g_ref[...]).astype(BF16)
    o_ref[...] = jnp.dot(h, w_ref[...], preferred_element_type=F32).astype(BF16)


def _inproj_odd_call(x, norm_g, w_all, layer):
    t = x.shape[0]
    n = w_all.shape[2]
    row = lambda i: (i, 0)
    return pl.pallas_call(
        _inproj_odd_kernel,
        grid=(t // FFN_ROWS,),
        in_specs=[pl.BlockSpec((FFN_ROWS, D_MODEL), row), _const_spec((1, D_MODEL)),
                  _layer_spec(w_all.shape[1:], layer)],
        out_specs=pl.BlockSpec((FFN_ROWS, n), row),
        out_shape=jax.ShapeDtypeStruct((t, n), BF16),
        compiler_params=pltpu.CompilerParams(
            dimension_semantics=("parallel",), vmem_limit_bytes=VMEM_LIMIT),
        name="inproj_odd",
    )(x, norm_g.reshape(1, D_MODEL), w_all)


def _deltanet_kernel(qkv_ref, gate_ref, small_ref, alog_ref, dtb_ref, ng_ref, o_ref,
                     state_ref, sums_ref):
    i = pl.program_id(1)
    n = DN_ROWS
    g = DN_GROUP
    heads = range(N_DN_HEADS)
    groups = range(n // g)
    combos = [(gi, hd) for gi in groups for hd in heads]

    row = _iota2(g, 0)
    col = _iota2(g, 1)
    diff = row ^ col
    same = (diff >> (DN_CHUNK.bit_length() - 1)) == 0
    incl = same & (col <= row)

    @pl.when(i == 0)
    def _():
        state_ref[...] = jnp.zeros_like(state_ref)
        sums_ref[0:g, :] = jnp.where(incl, 1.0, 0.0).astype(BF16)
        sums_ref[g:2 * g, :] = jnp.where(same, 1.0, 0.0).astype(BF16)

    sm = small_ref[...]
    beta_all = _sigmoid(sm)
    g_all = -jnp.exp(alog_ref[...]) * _softplus(sm + dtb_ref[...])

    hsl = lambda base_col, hd: slice(base_col + hd * HEAD_DIM, base_col + (hd + 1) * HEAD_DIM)
    rs = lambda gi: slice(gi * g, (gi + 1) * g)

    g_sums = [_dot_exact_lhs(sums_ref[...], g_all[rs(gi)], 3) for gi in groups]
    gc_all = [s[0:g] for s in g_sums]
    gl_all = [s[g:2 * g] for s in g_sums]
    gc_all_t = [c.T for c in gc_all]

    lane = lambda arr, hd: arr[:, LANE_DECAY + hd:LANE_DECAY + hd + 1]
    q = {(gi, hd): qkv_ref[rs(gi), hsl(0, hd)] for gi, hd in combos}
    k = {(gi, hd): qkv_ref[rs(gi), hsl(D_DN, hd)] for gi, hd in combos}
    beta = {(gi, hd): beta_all[rs(gi), LANE_BETA + hd:LANE_BETA + hd + 1] for gi, hd in combos}
    gc = {(gi, hd): lane(gc_all[gi], hd) for gi, hd in combos}
    gl = {(gi, hd): lane(gl_all[gi], hd) for gi, hd in combos}
    egc = {c: jnp.exp(gc[c]) for c in combos}
    decay = {(gi, hd): jnp.where(
        incl, jnp.exp(gc[gi, hd] - gc_all_t[gi][LANE_DECAY + hd:LANE_DECAY + hd + 1, :]), 0.0)
        for gi, hd in combos}
    kb = {c: k[c] * beta[c] for c in combos}
    rhs = {(gi, hd): jnp.concatenate(
        [qkv_ref[rs(gi), hsl(2 * D_DN, hd)] * beta[gi, hd], kb[gi, hd] * egc[gi, hd]], axis=1)
        for gi, hd in combos}
    on_diag = diff == 0
    a = {c: jnp.where(on_diag, 0.0, _dot_nt(kb[c], k[c]) * decay[c]) for c in combos}

    pair = diff == 1
    x = {c: jnp.where(on_diag, 1.0, jnp.where(pair, -a[c], 0.0)) for c in combos}
    b = 2
    while b < DN_CHUNK:
        off = (diff >> (b.bit_length() - 1)) == 1
        t = {c: _dot(x[c], jnp.where(off, a[c], 0.0)) for c in combos}
        x = {c: x[c] - _dot(t[c], x[c]) for c in combos}
        b *= 2
    sol = {c: _dot(x[c], rhs[c]) for c in combos}
    attn = {c: _dot_nt(q[c], k[c]) * decay[c] for c in combos}
    qd = {c: q[c] * egc[c] for c in combos}
    kd = {c: k[c] * jnp.exp(gl[c] - gc[c]) for c in combos}
    egl = {c: jnp.exp(gl[c]) for c in combos}

    state = [state_ref[hd] for hd in heads]
    for gi in groups:
        outs = [[] for _ in heads]
        for ch in range(g // DN_CHUNK):
            r = slice(ch * DN_CHUNK, (ch + 1) * DN_CHUNK)
            for hd in heads:
                c = (gi, hd)
                v_new = sol[c][r, :HEAD_DIM] - _dot(sol[c][r, HEAD_DIM:], state[hd])
                outs[hd].append(_dot(qd[c][r], state[hd]) + _dot(attn[c][r, r], v_new))
                state[hd] = (state[hd] * egl[c][ch * DN_CHUNK:ch * DN_CHUNK + 1, :]
                             + _dot_tn(kd[c][r], v_new))
        for hd in heads:
            o = jnp.concatenate(outs[hd], axis=0)
            sl = hsl(0, hd)
            o_ref[rs(gi), sl] = (_rms(o, ng_ref[...]) * _silu(gate_ref[rs(gi), sl])).astype(BF16)
    for hd in heads:
        state_ref[hd] = state[hd]


def _deltanet_call(qkv, gate, small, a_log, dt_bias, norm_g):
    b, s, _ = qkv.shape
    lane_vec = lambda v: jnp.zeros((1, LANES), F32).at[0, LANE_DECAY:LANE_DECAY + N_DN_HEADS].set(v)
    blk = lambda w: pl.BlockSpec((None, DN_ROWS, w), lambda bi, i: (bi, i, 0))
    return pl.pallas_call(
        _deltanet_kernel,
        grid=(b, s // DN_ROWS),
        in_specs=[blk(3 * D_DN), blk(D_DN), blk(LANES),
                  _const_spec((1, LANES)), _const_spec((1, LANES)), _const_spec((1, HEAD_DIM))],
        out_specs=blk(D_DN),
        out_shape=jax.ShapeDtypeStruct((b, s, D_DN), BF16),
        scratch_shapes=[pltpu.VMEM((N_DN_HEADS, HEAD_DIM, HEAD_DIM), F32),
                        pltpu.VMEM((2 * DN_GROUP, DN_GROUP), BF16)],
        compiler_params=pltpu.CompilerParams(
            dimension_semantics=("parallel", "arbitrary"), vmem_limit_bytes=VMEM_LIMIT),
        name="deltanet",
    )(qkv, gate, small, lane_vec(a_log), lane_vec(dt_bias), norm_g.reshape(1, HEAD_DIM))


def _fox_kernel(q_ref, k_ref, v_ref, small_ref, fbias_ref, gate_ref, o_ref, crow_ref, vaug_ref):
    i = pl.program_id(1)
    n = FOX_ROWS
    s = k_ref.shape[0]
    row = _iota2(n, 0)
    col = _iota2(n, 1)
    causal = col <= row

    @pl.when(i == 0)
    def _():
        tri = jnp.where(causal, 1.0, 0.0).astype(BF16)
        carry = jnp.zeros((1, LANES), F32)
        for blk in range(s // n):
            r = slice(blk * n, (blk + 1) * n)
            lf = _log_sigmoid(small_ref[r, :] + fbias_ref[...])
            c = _dot_exact_lhs(tri, lf, 3) + carry
            carry = c[n - 1:n, :]
            crow_ref[:, r] = c.T
        for hd in range(N_FOX_HEADS):
            vaug_ref[:, 2 * hd * HEAD_DIM:(2 * hd + 1) * HEAD_DIM] = (
                v_ref[:, hd * HEAD_DIM:(hd + 1) * HEAD_DIM])
            vaug_ref[:, (2 * hd + 1) * HEAD_DIM:(2 * hd + 2) * HEAD_DIM] = (
                jnp.ones((s, HEAD_DIM), BF16))

    hsl = lambda hd: slice(hd * HEAD_DIM, (hd + 1) * HEAD_DIM)

    for h0 in range(0, N_FOX_HEADS, FOX_HEADS_PER_LOOP):
        hds = range(h0, h0 + FOX_HEADS_PER_LOOP)

        def step(carry, k0, masked):
            ms = carry[0::2]
            accs = carry[1::2]
            logits = [_dot_nt(q_ref[:, hsl(hd)], k_ref[pl.ds(k0, n), hsl(hd)]) * SCALE
                      - crow_ref[LANE_FORGET + hd:LANE_FORGET + hd + 1, pl.ds(k0, n)]
                      for hd in hds]
            if masked:
                logits = [jnp.where(causal, lg, -1e30) for lg in logits]
            m_new = [jnp.maximum(m, jnp.max(lg, axis=-1, keepdims=True))
                     for m, lg in zip(ms, logits)]
            p = [jnp.exp(lg - mn) for lg, mn in zip(logits, m_new)]
            pv = [_dot(pp, vaug_ref[pl.ds(k0, n), 2 * hd * HEAD_DIM:(2 * hd + 2) * HEAD_DIM])
                  for pp, hd in zip(p, hds)]
            accs = [jnp.exp(m - mn) * acc + x for m, mn, acc, x in zip(ms, m_new, accs, pv)]
            out = []
            for mn, acc in zip(m_new, accs):
                out += [mn, acc]
            return tuple(out)

        init = (jnp.full((n, 1), -1e30, F32),
                jnp.zeros((n, 2 * HEAD_DIM), F32)) * FOX_HEADS_PER_LOOP
        carry = lax.fori_loop(0, i, lambda j, c: step(c, pl.multiple_of(j * n, n), False), init)
        carry = step(carry, pl.multiple_of(i * n, n), True)
        for idx, hd in enumerate(hds):
            sl = slice(hd * HEAD_DIM, (hd + 1) * HEAD_DIM)
            acc = carry[2 * idx + 1]
            o = acc[:, :HEAD_DIM] / acc[:, HEAD_DIM:]
            o_ref[:, sl] = (o * _sigmoid(gate_ref[:, sl])).astype(BF16)


def _fox_call(q, k, v, small, f_bias, gate):
    b, s, _ = q.shape
    blk = lambda w: pl.BlockSpec((None, FOX_ROWS, w), lambda bi, i: (bi, i, 0))
    full = lambda w: pl.BlockSpec((None, s, w), lambda bi, i: (bi, 0, 0))
    fb = jnp.zeros((1, LANES), F32).at[0, LANE_FORGET:LANE_FORGET + N_FOX_HEADS].set(f_bias)
    return pl.pallas_call(
        _fox_kernel,
        grid=(b, s // FOX_ROWS),
        in_specs=[blk(D_FOX), full(D_FOX), full(D_FOX), full(LANES), _const_spec((1, LANES)),
                  blk(D_FOX)],
        out_specs=blk(D_FOX),
        out_shape=jax.ShapeDtypeStruct((b, s, D_FOX), BF16),
        scratch_shapes=[pltpu.VMEM((LANES, s), F32), pltpu.VMEM((s, 2 * D_FOX), BF16)],
        compiler_params=pltpu.CompilerParams(
            dimension_semantics=("parallel", "arbitrary"), vmem_limit_bytes=VMEM_LIMIT),
        name="fox_attention",
    )(q, k, v, small, fb, gate)


SB_CUTOFF = -105.0


def _sb_kernel(q_ref, k_ref, v_ref, o_ref):
    n = SB_ROWS
    s = k_ref.shape[0]
    row = _iota2(n, 0)
    col = _iota2(n, 1)
    before = col < row
    later = jnp.concatenate([jnp.where(row > col, 1.0, 0.0).astype(BF16),
                             jnp.ones((n, LANES), BF16)], axis=1)
    later2 = jnp.concatenate([later, later], axis=0)

    def block(q, k0, rest, masked):
        z = _dot_nt(q, k_ref[pl.ds(k0, n), :]) * SCALE
        soft = jnp.log(1.0 + jnp.exp(-jnp.abs(z)))
        lsz = jnp.minimum(z, 0.0) - soft
        lsn = lsz - z
        lg = jnp.where(before, lsn, 0.0) if masked else lsn
        hi = lg.astype(BF16)
        lo = (lg - hi.astype(F32)).astype(BF16)
        sums = jnp.dot(jnp.concatenate([hi, lo], axis=1), later2, preferred_element_type=F32)
        tail = sums[:, :n] + jnp.concatenate([rest] * (n // LANES), axis=1)
        a = jnp.exp(lsz + tail)
        if masked:
            a = jnp.where(before, a, 0.0)
        return _dot(a, v_ref[pl.ds(k0, n), :]), rest + sums[:, n:]

    def pairs(qs, k_los, rests, masked):
        rep = lambda r: jnp.concatenate([r] * (n // LANES), axis=1)
        z = [_dot_nt(q, k_ref[pl.ds(k_lo, 2 * n), :]) * SCALE for q, k_lo in zip(qs, k_los)]
        lsz = [jnp.minimum(zz, 0.0) - jnp.log(1.0 + jnp.exp(-jnp.abs(zz))) for zz in z]
        lsn = [l - zz for l, zz in zip(lsz, z)]
        lg = [jnp.concatenate([jnp.where(before, l[:, n:], 0.0) if masked else l[:, n:],
                               l[:, :n]], axis=0) for l in lsn]
        hi = [l.astype(BF16) for l in lg]
        lo = [(l - h.astype(F32)).astype(BF16) for l, h in zip(lg, hi)]
        sums = [jnp.dot(jnp.concatenate([h, l], axis=1), later2, preferred_element_type=F32)
                for h, l in zip(hi, lo)]
        rest_mid = [r + sm[:n, n:] for r, sm in zip(rests, sums)]
        tail = [jnp.concatenate([sm[n:, :n] + rep(rm), sm[:n, :n] + rep(r)], axis=1)
                for sm, rm, r in zip(sums, rest_mid, rests)]
        a = [jnp.exp(l + t) for l, t in zip(lsz, tail)]
        if masked:
            a = [jnp.concatenate([aa[:, :n], jnp.where(before, aa[:, n:], 0.0)], axis=1) for aa in a]
        pv = [_dot(aa, v_ref[pl.ds(k_lo, 2 * n), :]) for aa, k_lo in zip(a, k_los)]
        return pv, [rm + sm[n:, n:] for rm, sm in zip(rest_mid, sums)]

    def pair(q, k_lo, rest, masked):
        pv, rest = pairs([q], [k_lo], [rest], masked)
        return pv[0], rest[0]

    def live(rest):
        return (jnp.max(rest) > SB_CUTOFF).astype(jnp.int32)

    acc0, _ = block(q_ref[0:n, :], 0, jnp.zeros((n, LANES), F32), True)
    o_ref[0:n, :] = acc0.astype(BF16)

    def first(idx):
        q0 = [pl.multiple_of(i * n, n) for i in idx]
        qs = [q_ref[pl.ds(s0, n), :] for s0 in q0]
        accs, rests = pairs(qs, [pl.multiple_of(s0 - n, n) for s0 in q0],
                            [jnp.zeros((n, LANES), F32)] * len(idx), True)
        return list(zip(qs, accs, rests))

    def finish(i, q, acc, rest):
        n_pairs = (i - 1) // 2

        def cond(c):
            return (c[0] < n_pairs) & (c[3] > 0)

        def body(c):
            t, acc, rest, _ = c
            k_lo = pl.multiple_of((i - 3 - 2 * t) * n, n)
            pv, rest = pair(q, k_lo, rest, False)
            return t + 1, acc + pv, rest, live(rest)

        _, acc, rest, alive = lax.while_loop(cond, body, (jnp.int32(0), acc, rest, live(rest)))
        acc = lax.cond(((i - 1) % 2 == 1) & (alive > 0),
                       lambda: acc + block(q, 0, rest, False)[0], lambda: acc)
        o_ref[pl.ds(pl.multiple_of(i * n, n), n), :] = acc.astype(BF16)

    def two_qblocks(p, _):
        idx = [1 + 2 * p, 2 + 2 * p]
        for i, start in zip(idx, first(idx)):
            finish(i, *start)
        return 0

    nq = s // n
    lax.fori_loop(0, (nq - 1) // 2, two_qblocks, 0)
    if (nq - 1) % 2 == 1:
        last = jnp.int32(nq - 1)
        finish(last, *first([last])[0])


def _sb_call(qkv):
    b, s, _ = qkv.shape
    nh = N_SB_HEADS
    head = lambda off: pl.BlockSpec((None, s, HEAD_DIM), lambda bi, h: (bi, 0, off + h))
    return pl.pallas_call(
        _sb_kernel,
        grid=(b, nh),
        in_specs=[head(0), head(nh), head(2 * nh)],
        out_specs=head(0),
        out_shape=jax.ShapeDtypeStruct((b, s, nh * HEAD_DIM), BF16),
        compiler_params=pltpu.CompilerParams(
            dimension_semantics=("parallel", "parallel"), vmem_limit_bytes=VMEM_LIMIT),
        name="stickbreaking_attention",
    )(qkv, qkv, qkv)


def _even_w_in(w):
    sizes = (3 * D_DN, D_DN, N_DN_HEADS, N_DN_HEADS, D_FOX, D_FOX, D_FOX, D_FOX, N_FOX_HEADS)
    parts, off = [], 0
    for sz in sizes:
        parts.append(w[..., off:off + sz])
        off += sz
    qkv, dgate, dbeta, ddecay, fq, fk, fv, fgate, fpre = parts
    pad = jnp.zeros(w.shape[:-1] + (LANES - 2 * N_DN_HEADS - N_FOX_HEADS,), w.dtype)
    small = jnp.concatenate([dbeta, ddecay, fpre, pad], axis=-1)
    return jnp.concatenate([qkv, dgate, fq, fk, fv, fgate, small], axis=-1)


def kernel(x, norm_ffn1, ffn1_w_gu, ffn1_w_down, norm_mix, w_in_even, dn_conv_w, dn_a_log,
           dn_dt_bias, dn_norm_g, fox_q_norm_g, fox_k_norm_g, fox_f_bias, w_out_even,
           w_in_odd, w_out_odd, norm_ffn2, ffn2_w_gu, ffn2_w_down):
    b, s, d = x.shape
    depth = norm_ffn1.shape[0]
    t = b * s
    bf = lambda a: a.astype(BF16)
    wgu1, wd1, wgu2, wd2 = bf(ffn1_w_gu), bf(ffn1_w_down), bf(ffn2_w_gu), bf(ffn2_w_down)
    w_in_e, w_out_e = bf(_even_w_in(w_in_even)), bf(w_out_even)
    w_in_o, w_out_o = bf(w_in_odd), bf(w_out_odd)
    x = x.reshape(t, d)
    for l in range(depth):
        j = l // 2
        x = _ffn_call(x, [], norm_ffn1[l], wgu1, wd1, l)
        if l % 2 == 0:
            qkv, dgate, fq, fk, fv, fgate, small = _inproj_even_call(
                x, s, norm_mix[l], w_in_e, j, fox_q_norm_g[j], fox_k_norm_g[j], dn_conv_w[j])
            r3 = lambda a: a.reshape(b, s, a.shape[-1])
            o_dn = _deltanet_call(r3(qkv), r3(dgate), r3(small), dn_a_log[j], dn_dt_bias[j],
                                  dn_norm_g[j])
            o_fox = _fox_call(r3(fq), r3(fk), r3(fv), r3(small), fox_f_bias[j], r3(fgate))
            pre = [(o_dn.reshape(t, D_DN), w_out_e, j, 0), (o_fox.reshape(t, D_FOX), w_out_e, j, 1)]
        else:
            qkv = _inproj_odd_call(x, norm_mix[l], w_in_o, j)
            o = _sb_call(qkv.reshape(b, s, qkv.shape[-1]))
            pre = [(o.reshape(t, o.shape[-1]), w_out_o, j, 0)]
        x = _ffn_call(x, pre, norm_ffn2[l], wgu2, wd2, l)
    return x.reshape(b, s, d)
```

```python
import functools

import jax
import jax.numpy as jnp
from jax import lax
from jax.experimental import pallas as pl
from jax.experimental.pallas import tpu as pltpu

F32 = jnp.float32
BF16 = jnp.bfloat16

D_MODEL = 1024
HEAD_DIM = 128
N_DN_HEADS = 4
N_FOX_HEADS = 4
N_SB_HEADS = 8
D_DN = N_DN_HEADS * HEAD_DIM
D_FOX = N_FOX_HEADS * HEAD_DIM
CONV_WIDTH = 4
D_FF = 2816
EPS = 1e-6
SCALE = HEAD_DIM ** -0.5

LANES = 128
SUBLANES = 8
VMEM_LIMIT = 56 * 1024 * 1024

FFN_ROWS = 512
SB_ROWS = 256
FOX_ROWS = 512
FOX_HEADS_PER_LOOP = 4
DN_ROWS = 512
DN_GROUP = 256
DN_CHUNK = 256

LANE_BETA = 0
LANE_DECAY = N_DN_HEADS
LANE_FORGET = 2 * N_DN_HEADS


def _dot(a, b):
    return jnp.dot(a.astype(BF16), b.astype(BF16), preferred_element_type=F32)


def _dot_nt(a, b):
    return lax.dot_general(a.astype(BF16), b.astype(BF16), (((1,), (1,)), ((), ())),
                           preferred_element_type=F32)


def _dot_tn(a, b):
    return lax.dot_general(a.astype(BF16), b.astype(BF16), (((0,), (0,)), ((), ())),
                           preferred_element_type=F32)


def _split_bf16(x, parts):
    out = []
    r = x
    for _ in range(parts - 1):
        p = r.astype(BF16)
        out.append(p)
        r = r - p.astype(F32)
    out.append(r.astype(BF16))
    return out


def _dot_exact_lhs(m_bf16, x, parts):
    acc = None
    for p in _split_bf16(x, parts):
        t = jnp.dot(m_bf16, p, preferred_element_type=F32)
        acc = t if acc is None else acc + t
    return acc


def _dot_exact_rhs(x, m_bf16, parts):
    acc = None
    for p in _split_bf16(x, parts):
        t = jnp.dot(p, m_bf16, preferred_element_type=F32)
        acc = t if acc is None else acc + t
    return acc


def _rms(xf, g):
    return xf * lax.rsqrt(jnp.mean(xf * xf, axis=-1, keepdims=True) + EPS) * g


def _softplus(x):
    return jnp.maximum(x, 0.0) + jnp.log1p(jnp.exp(-jnp.abs(x)))


def _log_sigmoid(x):
    return jnp.minimum(x, 0.0) - jnp.log1p(jnp.exp(-jnp.abs(x)))


def _sigmoid(x):
    return 1.0 / (1.0 + jnp.exp(-x))


def _silu(x):
    return x * _sigmoid(x)


def _iota2(n, axis):
    return lax.broadcasted_iota(jnp.int32, (n, n), axis)


def _const_spec(shape):
    return pl.BlockSpec(shape, lambda *_: (0,) * len(shape))


def _layer_spec(shape, layer, row_block=0):
    return pl.BlockSpec((None,) + tuple(shape), lambda *_: (layer, row_block, 0))


def _ffn_kernel(n_pre, *refs):
    x_ref = refs[0]
    pre = refs[1:1 + 2 * n_pre]
    g_ref, wgu_ref, wd_ref, o_ref = refs[1 + 2 * n_pre:]
    x = x_ref[...]
    for p in range(n_pre):
        x = x + jnp.dot(pre[2 * p][...], pre[2 * p + 1][...], preferred_element_type=F32)
    h = _rms(x, g_ref[...]).astype(BF16)
    gate = jnp.dot(h, wgu_ref[:, :D_FF], preferred_element_type=F32)
    up = jnp.dot(h, wgu_ref[:, D_FF:], preferred_element_type=F32)
    act = (_silu(gate) * up).astype(BF16)
    o_ref[...] = x + 0.5 * jnp.dot(act, wd_ref[...], preferred_element_type=F32)


def _ffn_call(x, pre, norm_g, wgu_all, wd_all, layer):
    t = x.shape[0]
    row = lambda i: (i, 0)
    in_specs = [pl.BlockSpec((FFN_ROWS, D_MODEL), row)]
    args = [x]
    for o, w_all, j, rb in pre:
        in_specs += [pl.BlockSpec((FFN_ROWS, o.shape[1]), row),
                     _layer_spec((o.shape[1], D_MODEL), j, rb)]
        args += [o, w_all]
    in_specs += [_const_spec((1, D_MODEL)), _layer_spec(wgu_all.shape[1:], layer),
                 _layer_spec(wd_all.shape[1:], layer)]
    args += [norm_g.reshape(1, D_MODEL), wgu_all, wd_all]
    return pl.pallas_call(
        functools.partial(_ffn_kernel, len(pre)),
        grid=(t // FFN_ROWS,),
        in_specs=in_specs,
        out_specs=pl.BlockSpec((FFN_ROWS, D_MODEL), row),
        out_shape=jax.ShapeDtypeStruct((t, D_MODEL), F32),
        compiler_params=pltpu.CompilerParams(
            dimension_semantics=("parallel",), vmem_limit_bytes=VMEM_LIMIT),
        name="ffn",
    )(*args)


def _inproj_even_kernel(tiles_per_seq, x_ref, g_ref, w_ref, qg_ref, kg_ref, convw_ref,
                        qkv_ref, dgate_ref, fq_ref, fk_ref, fv_ref, fgate_ref, small_ref,
                        xx_ref):
    i = pl.program_id(0)
    n = FFN_ROWS
    h = _rms(x_ref[...], g_ref[...]).astype(BF16)
    c = 0

    def proj(width):
        nonlocal c
        y = jnp.dot(h, w_ref[:, c:c + width], preferred_element_type=F32)
        c += width
        return y

    @pl.when(i % tiles_per_seq == 0)
    def _():
        xx_ref[n:n + SUBLANES, :] = jnp.zeros((SUBLANES, 3 * D_DN), F32)

    xx_ref[0:SUBLANES, :] = xx_ref[n:n + SUBLANES, :]
    xx_ref[SUBLANES:SUBLANES + n, :] = proj(3 * D_DN)
    ones_sq = jnp.ones((HEAD_DIM, HEAD_DIM), BF16)

    def conv_tile(tile):
        sl = slice(tile * HEAD_DIM, (tile + 1) * HEAD_DIM)
        xx = xx_ref[:, sl]
        y = xx[SUBLANES:] * convw_ref[CONV_WIDTH - 1:CONV_WIDTH, sl]
        for d in range(1, CONV_WIDTH):
            tap = convw_ref[CONV_WIDTH - 1 - d:CONV_WIDTH - d, sl]
            y = y + pltpu.roll(xx, d, axis=0)[SUBLANES:] * tap
        y = _silu(y)
        if tile < 2 * N_DN_HEADS:
            ss = _dot_exact_rhs(y * y, ones_sq, 2)
            y = y * (lax.rsqrt(ss + EPS) * (SCALE if tile < N_DN_HEADS else 1.0))
        qkv_ref[:, sl] = y

    def fox_norm(val, gain_ref, out_ref):
        for hd in range(N_FOX_HEADS):
            sl = slice(hd * HEAD_DIM, (hd + 1) * HEAD_DIM)
            out_ref[:, sl] = _rms(val[:, sl], gain_ref[...]).astype(BF16)

    dgate_ref[...] = proj(D_DN)
    conv_tile(0)
    conv_tile(1)
    fq = proj(D_FOX)
    conv_tile(2)
    conv_tile(3)
    fk = proj(D_FOX)
    fox_norm(fq, qg_ref, fq_ref)
    conv_tile(4)
    conv_tile(5)
    fv_ref[...] = proj(D_FOX).astype(BF16)
    fox_norm(fk, kg_ref, fk_ref)
    conv_tile(6)
    conv_tile(7)
    fgate_ref[...] = proj(D_FOX)
    conv_tile(8)
    conv_tile(9)
    small_ref[...] = proj(LANES)
    conv_tile(10)
    conv_tile(11)


def _inproj_even_call(x, seq_len, norm_g, w_all, layer, q_g, k_g, conv_w):
    t = x.shape[0]
    row = lambda i: (i, 0)
    widths = [(3 * D_DN, F32), (D_DN, F32), (D_FOX, BF16), (D_FOX, BF16), (D_FOX, BF16),
              (D_FOX, F32), (LANES, F32)]
    return pl.pallas_call(
        functools.partial(_inproj_even_kernel, seq_len // FFN_ROWS),
        grid=(t // FFN_ROWS,),
        in_specs=[pl.BlockSpec((FFN_ROWS, D_MODEL), row), _const_spec((1, D_MODEL)),
                  _layer_spec(w_all.shape[1:], layer), _const_spec((1, HEAD_DIM)),
                  _const_spec((1, HEAD_DIM)), _const_spec((CONV_WIDTH, 3 * D_DN))],
        out_specs=[pl.BlockSpec((FFN_ROWS, n), row) for n, _ in widths],
        out_shape=[jax.ShapeDtypeStruct((t, n), dt) for n, dt in widths],
        scratch_shapes=[pltpu.VMEM((SUBLANES + FFN_ROWS, 3 * D_DN), F32)],
        compiler_params=pltpu.CompilerParams(
            dimension_semantics=("arbitrary",), vmem_limit_bytes=VMEM_LIMIT),
        name="inproj_even",
    )(x, norm_g.reshape(1, D_MODEL), w_all, q_g.reshape(1, HEAD_DIM), k_g.reshape(1, HEAD_DIM),
      conv_w)


def _inproj_odd_kernel(x_ref, g_ref, w_ref, o_ref):
    h = _rms(x_ref[...], g_ref[...]).astype(BF16)
    o_ref[...] = jnp.dot(h, w_ref[...], preferred_element_type=F32).astype(BF16)


def _inproj_odd_call(x, norm_g, w_all, layer):
    t = x.shape[0]
    n = w_all.shape[2]
    row = lambda i: (i, 0)
    return pl.pallas_call(
        _inproj_odd_kernel,
        grid=(t // FFN_ROWS,),
        in_specs=[pl.BlockSpec((FFN_ROWS, D_MODEL), row), _const_spec((1, D_MODEL)),
                  _layer_spec(w_all.shape[1:], layer)],
        out_specs=pl.BlockSpec((FFN_ROWS, n), row),
        out_shape=jax.ShapeDtypeStruct((t, n), BF16),
        compiler_params=pltpu.CompilerParams(
            dimension_semantics=("parallel",), vmem_limit_bytes=VMEM_LIMIT),
        name="inproj_odd",
    )(x, norm_g.reshape(1, D_MODEL), w_all)


def _deltanet_kernel(qkv_ref, gate_ref, small_ref, alog_ref, dtb_ref, ng_ref, o_ref,
                     state_ref, sums_ref):
    i = pl.program_id(1)
    n = DN_ROWS
    g = DN_GROUP
    heads = range(N_DN_HEADS)
    groups = range(n // g)

    row = _iota2(g, 0)
    col = _iota2(g, 1)
    diff = row ^ col
    same = (diff >> (DN_CHUNK.bit_length() - 1)) == 0
    incl = same & (col <= row)

    @pl.when(i == 0)
    def _():
        state_ref[...] = jnp.zeros_like(state_ref)
        sums_ref[0:g, :] = jnp.where(incl, 1.0, 0.0).astype(BF16)
        sums_ref[g:2 * g, :] = jnp.where(same, 1.0, 0.0).astype(BF16)

    sm = small_ref[...]
    beta_all = _sigmoid(sm)
    g_all = -jnp.exp(alog_ref[...]) * _softplus(sm + dtb_ref[...])

    hsl = lambda base_col, hd: slice(base_col + hd * HEAD_DIM, base_col + (hd + 1) * HEAD_DIM)
    rs = lambda gi: slice(gi * g, (gi + 1) * g)

    g_sums = [_dot_exact_lhs(sums_ref[...], g_all[rs(gi)], 3) for gi in groups]
    gc_all = [s[0:g] for s in g_sums]
    gl_all = [s[g:2 * g] for s in g_sums]
    gc_all_t = [c.T for c in gc_all]

    lane = lambda arr, hd: arr[:, LANE_DECAY + hd:LANE_DECAY + hd + 1]
    on_diag = diff == 0
    pair = diff == 1
    levels = []
    b = 2
    while b < DN_CHUNK:
        levels.append(b.bit_length() - 1)
        b *= 2
    n_chunks = g // DN_CHUNK
    q, k, gc, gl, egc, decay, rhs, a, x = {}, {}, {}, {}, {}, {}, {}, {}, {}
    sol, attn, qd, kd, egl = {}, {}, {}, {}, {}
    state = [state_ref[hd] for hd in heads]
    outs = {gi: [[] for _ in heads] for gi in groups}

    def prep(gi, hd):
        c = (gi, hd)
        q[c] = qkv_ref[rs(gi), hsl(0, hd)]
        k[c] = qkv_ref[rs(gi), hsl(D_DN, hd)]
        beta = beta_all[rs(gi), LANE_BETA + hd:LANE_BETA + hd + 1]
        gc[c] = lane(gc_all[gi], hd)
        gl[c] = lane(gl_all[gi], hd)
        egc[c] = jnp.exp(gc[c])
        decay[c] = jnp.where(
            incl, jnp.exp(gc[c] - gc_all_t[gi][LANE_DECAY + hd:LANE_DECAY + hd + 1, :]), 0.0)
        kb = k[c] * beta
        rhs[c] = jnp.concatenate(
            [qkv_ref[rs(gi), hsl(2 * D_DN, hd)] * beta, kb * egc[c]], axis=1)
        a[c] = jnp.where(on_diag, 0.0, _dot_nt(kb, k[c]) * decay[c])
        x[c] = jnp.where(on_diag, 1.0, jnp.where(pair, -a[c], 0.0))

    def invert_level(gi, sh):
        off = (diff >> sh) == 1
        t = {hd: _dot(x[gi, hd], jnp.where(off, a[gi, hd], 0.0)) for hd in heads}
        for hd in heads:
            x[gi, hd] = x[gi, hd] - _dot(t[hd], x[gi, hd])

    def solve(gi):
        for hd in heads:
            c = (gi, hd)
            sol[c] = _dot(x[c], rhs[c])
            attn[c] = _dot_nt(q[c], k[c]) * decay[c]
            qd[c] = q[c] * egc[c]
            kd[c] = k[c] * jnp.exp(gl[c] - gc[c])
            egl[c] = jnp.exp(gl[c])

    def scan_chunk(gi, ch):
        r = slice(ch * DN_CHUNK, (ch + 1) * DN_CHUNK)
        for hd in heads:
            c = (gi, hd)
            v_new = sol[c][r, :HEAD_DIM] - _dot(sol[c][r, HEAD_DIM:], state[hd])
            outs[gi][hd].append(_dot(qd[c][r], state[hd]) + _dot(attn[c][r, r], v_new))
            state[hd] = (state[hd] * egl[c][ch * DN_CHUNK:ch * DN_CHUNK + 1, :]
                         + _dot_tn(kd[c][r], v_new))

    def emit(gi):
        for hd in heads:
            o = jnp.concatenate(outs[gi][hd], axis=0)
            sl = hsl(0, hd)
            o_ref[rs(gi), sl] = (_rms(o, ng_ref[...]) * _silu(gate_ref[rs(gi), sl])).astype(BF16)

    for gi in groups:
        for hd in heads:
            prep(gi, hd)
    for sh in levels:
        for gi in groups:
            invert_level(gi, sh)
    for gi in groups:
        solve(gi)
    for gi in groups:
        for ch in range(n_chunks):
            scan_chunk(gi, ch)
        emit(gi)
    for hd in heads:
        state_ref[hd] = state[hd]


def _deltanet_call(qkv, gate, small, a_log, dt_bias, norm_g):
    b, s, _ = qkv.shape
    lane_vec = lambda v: jnp.zeros((1, LANES), F32).at[0, LANE_DECAY:LANE_DECAY + N_DN_HEADS].set(v)
    blk = lambda w: pl.BlockSpec((None, DN_ROWS, w), lambda bi, i: (bi, i, 0))
    return pl.pallas_call(
        _deltanet_kernel,
        grid=(b, s // DN_ROWS),
        in_specs=[blk(3 * D_DN), blk(D_DN), blk(LANES),
                  _const_spec((1, LANES)), _const_spec((1, LANES)), _const_spec((1, HEAD_DIM))],
        out_specs=blk(D_DN),
        out_shape=jax.ShapeDtypeStruct((b, s, D_DN), BF16),
        scratch_shapes=[pltpu.VMEM((N_DN_HEADS, HEAD_DIM, HEAD_DIM), F32),
                        pltpu.VMEM((2 * DN_GROUP, DN_GROUP), BF16)],
        compiler_params=pltpu.CompilerParams(
            dimension_semantics=("parallel", "arbitrary"), vmem_limit_bytes=VMEM_LIMIT),
        name="deltanet",
    )(qkv, gate, small, lane_vec(a_log), lane_vec(dt_bias), norm_g.reshape(1, HEAD_DIM))


def _fox_kernel(q_ref, k_ref, v_ref, small_ref, fbias_ref, gate_ref, o_ref, crow_ref, vaug_ref):
    i = pl.program_id(1)
    n = FOX_ROWS
    s = k_ref.shape[0]
    row = _iota2(n, 0)
    col = _iota2(n, 1)
    causal = col <= row

    @pl.when(i == 0)
    def _():
        tri = jnp.where(causal, 1.0, 0.0).astype(BF16)
        carry = jnp.zeros((1, LANES), F32)
        for blk in range(s // n):
            r = slice(blk * n, (blk + 1) * n)
            lf = _log_sigmoid(small_ref[r, :] + fbias_ref[...])
            c = _dot_exact_lhs(tri, lf, 3) + carry
            carry = c[n - 1:n, :]
            crow_ref[:, r] = c.T
        for hd in range(N_FOX_HEADS):
            vaug_ref[:, 2 * hd * HEAD_DIM:(2 * hd + 1) * HEAD_DIM] = (
                v_ref[:, hd * HEAD_DIM:(hd + 1) * HEAD_DIM])
            vaug_ref[:, (2 * hd + 1) * HEAD_DIM:(2 * hd + 2) * HEAD_DIM] = (
                jnp.ones((s, HEAD_DIM), BF16))

    hsl = lambda hd: slice(hd * HEAD_DIM, (hd + 1) * HEAD_DIM)

    for h0 in range(0, N_FOX_HEADS, FOX_HEADS_PER_LOOP):
        hds = range(h0, h0 + FOX_HEADS_PER_LOOP)

        def step(carry, k0, masked):
            ms = carry[0::2]
            accs = carry[1::2]
            logits = [_dot_nt(q_ref[:, hsl(hd)], k_ref[pl.ds(k0, n), hsl(hd)]) * SCALE
                      - crow_ref[LANE_FORGET + hd:LANE_FORGET + hd + 1, pl.ds(k0, n)]
                      for hd in hds]
            if masked:
                logits = [jnp.where(causal, lg, -1e30) for lg in logits]
            m_new = [jnp.maximum(m, jnp.max(lg, axis=-1, keepdims=True))
                     for m, lg in zip(ms, logits)]
            p = [jnp.exp(lg - mn) for lg, mn in zip(logits, m_new)]
            pv = [_dot(pp, vaug_ref[pl.ds(k0, n), 2 * hd * HEAD_DIM:(2 * hd + 2) * HEAD_DIM])
                  for pp, hd in zip(p, hds)]
            accs = [jnp.exp(m - mn) * acc + x for m, mn, acc, x in zip(ms, m_new, accs, pv)]
            out = []
            for mn, acc in zip(m_new, accs):
                out += [mn, acc]
            return tuple(out)

        init = (jnp.full((n, 1), -1e30, F32),
                jnp.zeros((n, 2 * HEAD_DIM), F32)) * FOX_HEADS_PER_LOOP
        carry = lax.fori_loop(0, i, lambda j, c: step(c, pl.multiple_of(j * n, n), False), init)
        carry = step(carry, pl.multiple_of(i * n, n), True)
        for idx, hd in enumerate(hds):
            sl = slice(hd * HEAD_DIM, (hd + 1) * HEAD_DIM)
            acc = carry[2 * idx + 1]
            o = acc[:, :HEAD_DIM] / acc[:, HEAD_DIM:]
            o_ref[:, sl] = (o * _sigmoid(gate_ref[:, sl])).astype(BF16)


def _fox_call(q, k, v, small, f_bias, gate):
    b, s, _ = q.shape
    blk = lambda w: pl.BlockSpec((None, FOX_ROWS, w), lambda bi, i: (bi, i, 0))
    full = lambda w: pl.BlockSpec((None, s, w), lambda bi, i: (bi, 0, 0))
    fb = jnp.zeros((1, LANES), F32).at[0, LANE_FORGET:LANE_FORGET + N_FOX_HEADS].set(f_bias)
    return pl.pallas_call(
        _fox_kernel,
        grid=(b, s // FOX_ROWS),
        in_specs=[blk(D_FOX), full(D_FOX), full(D_FOX), full(LANES), _const_spec((1, LANES)),
                  blk(D_FOX)],
        out_specs=blk(D_FOX),
        out_shape=jax.ShapeDtypeStruct((b, s, D_FOX), BF16),
        scratch_shapes=[pltpu.VMEM((LANES, s), F32), pltpu.VMEM((s, 2 * D_FOX), BF16)],
        compiler_params=pltpu.CompilerParams(
            dimension_semantics=("parallel", "arbitrary"), vmem_limit_bytes=VMEM_LIMIT),
        name="fox_attention",
    )(q, k, v, small, fb, gate)


SB_CUTOFF = -105.0


def _sb_kernel(q_ref, k_ref, v_ref, o_ref):
    n = SB_ROWS
    s = k_ref.shape[0]
    row = _iota2(n, 0)
    col = _iota2(n, 1)
    before = col < row
    later = jnp.concatenate([jnp.where(row > col, 1.0, 0.0).astype(BF16),
                             jnp.ones((n, LANES), BF16)], axis=1)
    later2 = jnp.concatenate([later, later], axis=0)

    def block(q, k0, rest, masked):
        z = _dot_nt(q, k_ref[pl.ds(k0, n), :]) * SCALE
        soft = jnp.log(1.0 + jnp.exp(-jnp.abs(z)))
        lsz = jnp.minimum(z, 0.0) - soft
        lsn = lsz - z
        lg = jnp.where(before, lsn, 0.0) if masked else lsn
        hi = lg.astype(BF16)
        lo = (lg - hi.astype(F32)).astype(BF16)
        sums = jnp.dot(jnp.concatenate([hi, lo], axis=1), later2, preferred_element_type=F32)
        tail = sums[:, :n] + jnp.concatenate([rest] * (n // LANES), axis=1)
        a = jnp.exp(lsz + tail)
        if masked:
            a = jnp.where(before, a, 0.0)
        return _dot(a, v_ref[pl.ds(k0, n), :]), rest + sums[:, n:]

    def pairs(qs, k_los, rests, masked):
        rep = lambda r: jnp.concatenate([r] * (n // LANES), axis=1)
        z = [_dot_nt(q, k_ref[pl.ds(k_lo, 2 * n), :]) * SCALE for q, k_lo in zip(qs, k_los)]
        lsz = [jnp.minimum(zz, 0.0) - jnp.log(1.0 + jnp.exp(-jnp.abs(zz))) for zz in z]
        lsn = [l - zz for l, zz in zip(lsz, z)]
        lg = [jnp.concatenate([jnp.where(before, l[:, n:], 0.0) if masked else l[:, n:],
                               l[:, :n]], axis=0) for l in lsn]
        hi = [l.astype(BF16) for l in lg]
        lo = [(l - h.astype(F32)).astype(BF16) for l, h in zip(lg, hi)]
        sums = [jnp.dot(jnp.concatenate([h, l], axis=1), later2, preferred_element_type=F32)
                for h, l in zip(hi, lo)]
        rest_mid = [r + sm[:n, n:] for r, sm in zip(rests, sums)]
        tail = [jnp.concatenate([sm[n:, :n] + rep(rm), sm[:n, :n] + rep(r)], axis=1)
                for sm, rm, r in zip(sums, rest_mid, rests)]
        a = [jnp.exp(l + t) for l, t in zip(lsz, tail)]
        if masked:
            a = [jnp.concatenate([aa[:, :n], jnp.where(before, aa[:, n:], 0.0)], axis=1) for aa in a]
        pv = [_dot(aa, v_ref[pl.ds(k_lo, 2 * n), :]) for aa, k_lo in zip(a, k_los)]
        return pv, [rm + sm[n:, n:] for rm, sm in zip(rest_mid, sums)]

    def pair(q, k_lo, rest, masked):
        pv, rest = pairs([q], [k_lo], [rest], masked)
        return pv[0], rest[0]

    def live(rest):
        return (jnp.max(rest) > SB_CUTOFF).astype(jnp.int32)

    acc0, _ = block(q_ref[0:n, :], 0, jnp.zeros((n, LANES), F32), True)
    o_ref[0:n, :] = acc0.astype(BF16)

    def first(idx):
        q0 = [pl.multiple_of(i * n, n) for i in idx]
        qs = [q_ref[pl.ds(s0, n), :] for s0 in q0]
        accs, rests = pairs(qs, [pl.multiple_of(s0 - n, n) for s0 in q0],
                            [jnp.zeros((n, LANES), F32)] * len(idx), True)
        return list(zip(qs, accs, rests))

    def finish(i, q, acc, rest):
        n_pairs = (i - 1) // 2

        def cond(c):
            return (c[0] < n_pairs) & (c[3] > 0)

        def body(c):
            t, acc, rest, _ = c
            k_lo = pl.multiple_of((i - 3 - 2 * t) * n, n)
            pv, rest = pair(q, k_lo, rest, False)
            return t + 1, acc + pv, rest, live(rest)

        _, acc, rest, alive = lax.while_loop(cond, body, (jnp.int32(0), acc, rest, live(rest)))
        acc = lax.cond(((i - 1) % 2 == 1) & (alive > 0),
                       lambda: acc + block(q, 0, rest, False)[0], lambda: acc)
        o_ref[pl.ds(pl.multiple_of(i * n, n), n), :] = acc.astype(BF16)

    def two_qblocks(p, _):
        idx = [1 + 2 * p, 2 + 2 * p]
        for i, start in zip(idx, first(idx)):
            finish(i, *start)
        return 0

    nq = s // n
    lax.fori_loop(0, (nq - 1) // 2, two_qblocks, 0)
    if (nq - 1) % 2 == 1:
        last = jnp.int32(nq - 1)
        finish(last, *first([last])[0])


def _sb_call(qkv):
    b, s, _ = qkv.shape
    nh = N_SB_HEADS
    head = lambda off: pl.BlockSpec((None, s, HEAD_DIM), lambda bi, h: (bi, 0, off + h))
    return pl.pallas_call(
        _sb_kernel,
        grid=(b, nh),
        in_specs=[head(0), head(nh), head(2 * nh)],
        out_specs=head(0),
        out_shape=jax.ShapeDtypeStruct((b, s, nh * HEAD_DIM), BF16),
        compiler_params=pltpu.CompilerParams(
            dimension_semantics=("parallel", "parallel"), vmem_limit_bytes=VMEM_LIMIT),
        name="stickbreaking_attention",
    )(qkv, qkv, qkv)


def _even_w_in(w):
    sizes = (3 * D_DN, D_DN, N_DN_HEADS, N_DN_HEADS, D_FOX, D_FOX, D_FOX, D_FOX, N_FOX_HEADS)
    parts, off = [], 0
    for sz in sizes:
        parts.append(w[..., off:off + sz])
        off += sz
    qkv, dgate, dbeta, ddecay, fq, fk, fv, fgate, fpre = parts
    pad = jnp.zeros(w.shape[:-1] + (LANES - 2 * N_DN_HEADS - N_FOX_HEADS,), w.dtype)
    small = jnp.concatenate([dbeta, ddecay, fpre, pad], axis=-1)
    return jnp.concatenate([qkv, dgate, fq, fk, fv, fgate, small], axis=-1)


def kernel(x, norm_ffn1, ffn1_w_gu, ffn1_w_down, norm_mix, w_in_even, dn_conv_w, dn_a_log,
           dn_dt_bias, dn_norm_g, fox_q_norm_g, fox_k_norm_g, fox_f_bias, w_out_even,
           w_in_odd, w_out_odd, norm_ffn2, ffn2_w_gu, ffn2_w_down):
    b, s, d = x.shape
    depth = norm_ffn1.shape[0]
    t = b * s
    bf = lambda a: a.astype(BF16)
    wgu1, wd1, wgu2, wd2 = bf(ffn1_w_gu), bf(ffn1_w_down), bf(ffn2_w_gu), bf(ffn2_w_down)
    w_in_e, w_out_e = bf(_even_w_in(w_in_even)), bf(w_out_even)
    w_in_o, w_out_o = bf(w_in_odd), bf(w_out_odd)
    x = x.reshape(t, d)
    for l in range(depth):
        j = l // 2
        x = _ffn_call(x, [], norm_ffn1[l], wgu1, wd1, l)
        if l % 2 == 0:
            qkv, dgate, fq, fk, fv, fgate, small = _inproj_even_call(
                x, s, norm_mix[l], w_in_e, j, fox_q_norm_g[j], fox_k_norm_g[j], dn_conv_w[j])
            r3 = lambda a: a.reshape(b, s, a.shape[-1])
            o_dn = _deltanet_call(r3(qkv), r3(dgate), r3(small), dn_a_log[j], dn_dt_bias[j],
                                  dn_norm_g[j])
            o_fox = _fox_call(r3(fq), r3(fk), r3(fv), r3(small), fox_f_bias[j], r3(fgate))
            pre = [(o_dn.reshape(t, D_DN), w_out_e, j, 0), (o_fox.reshape(t, D_FOX), w_out_e, j, 1)]
        else:
            qkv = _inproj_odd_call(x, norm_mix[l], w_in_o, j)
            o = _sb_call(qkv.reshape(b, s, qkv.shape[-1]))
            pre = [(o.reshape(t, o.shape[-1]), w_out_o, j, 0)]
        x = _ffn_call(x, pre, norm_ffn2[l], wgu2, wd2, l)
    return x.reshape(b, s, d)
```

```python
import functools

import jax
import jax.numpy as jnp
from jax import lax
from jax.experimental import pallas as pl
from jax.experimental.pallas import tpu as pltpu

F32 = jnp.float32
BF16 = jnp.bfloat16

D_MODEL = 1024
HEAD_DIM = 128
N_DN_HEADS = 4
N_FOX_HEADS = 4
N_SB_HEADS = 8
D_DN = N_DN_HEADS * HEAD_DIM
D_FOX = N_FOX_HEADS * HEAD_DIM
CONV_WIDTH = 4
D_FF = 2816
EPS = 1e-6
SCALE = HEAD_DIM ** -0.5

LANES = 128
SUBLANES = 8
VMEM_LIMIT = 56 * 1024 * 1024

FFN_ROWS = 512
SB_ROWS = 256
FOX_ROWS = 512
FOX_HEADS_PER_LOOP = 4
DN_ROWS = 512
DN_GROUP = 256
DN_CHUNK = 256

LANE_BETA = 0
LANE_DECAY = N_DN_HEADS
LANE_FORGET = 2 * N_DN_HEADS


def _dot(a, b):
    return jnp.dot(a.astype(BF16), b.astype(BF16), preferred_element_type=F32)


def _dot_nt(a, b):
    return lax.dot_general(a.astype(BF16), b.astype(BF16), (((1,), (1,)), ((), ())),
                           preferred_element_type=F32)


def _dot_tn(a, b):
    return lax.dot_general(a.astype(BF16), b.astype(BF16), (((0,), (0,)), ((), ())),
                           preferred_element_type=F32)


def _split_bf16(x, parts):
    out = []
    r = x
    for _ in range(parts - 1):
        p = r.astype(BF16)
        out.append(p)
        r = r - p.astype(F32)
    out.append(r.astype(BF16))
    return out


def _dot_exact_lhs(m_bf16, x, parts):
    acc = None
    for p in _split_bf16(x, parts):
        t = jnp.dot(m_bf16, p, preferred_element_type=F32)
        acc = t if acc is None else acc + t
    return acc


def _dot_exact_rhs(x, m_bf16, parts):
    acc = None
    for p in _split_bf16(x, parts):
        t = jnp.dot(p, m_bf16, preferred_element_type=F32)
        acc = t if acc is None else acc + t
    return acc


def _rms(xf, g):
    return xf * lax.rsqrt(jnp.mean(xf * xf, axis=-1, keepdims=True) + EPS) * g


def _softplus(x):
    return jnp.maximum(x, 0.0) + jnp.log1p(jnp.exp(-jnp.abs(x)))


def _log_sigmoid(x):
    return jnp.minimum(x, 0.0) - jnp.log1p(jnp.exp(-jnp.abs(x)))


def _sigmoid(x):
    return 1.0 / (1.0 + jnp.exp(-x))


def _silu(x):
    return x * _sigmoid(x)


def _iota2(n, axis):
    return lax.broadcasted_iota(jnp.int32, (n, n), axis)


def _const_spec(shape):
    return pl.BlockSpec(shape, lambda *_: (0,) * len(shape))


def _layer_spec(shape, layer, row_block=0):
    return pl.BlockSpec((None,) + tuple(shape), lambda *_: (layer, row_block, 0))


def _ffn_kernel(n_pre, *refs):
    x_ref = refs[0]
    pre = refs[1:1 + 2 * n_pre]
    g_ref, wgu_ref, wd_ref, o_ref = refs[1 + 2 * n_pre:]
    x = x_ref[...]
    for p in range(n_pre):
        x = x + jnp.dot(pre[2 * p][...], pre[2 * p + 1][...], preferred_element_type=F32)
    h = _rms(x, g_ref[...]).astype(BF16)
    gate = jnp.dot(h, wgu_ref[:, :D_FF], preferred_element_type=F32)
    up = jnp.dot(h, wgu_ref[:, D_FF:], preferred_element_type=F32)
    act = (_silu(gate) * up).astype(BF16)
    o_ref[...] = x + 0.5 * jnp.dot(act, wd_ref[...], preferred_element_type=F32)


def _ffn_call(x, pre, norm_g, wgu_all, wd_all, layer):
    t = x.shape[0]
    row = lambda i: (i, 0)
    in_specs = [pl.BlockSpec((FFN_ROWS, D_MODEL), row)]
    args = [x]
    for o, w_all, j, rb in pre:
        in_specs += [pl.BlockSpec((FFN_ROWS, o.shape[1]), row),
                     _layer_spec((o.shape[1], D_MODEL), j, rb)]
        args += [o, w_all]
    in_specs += [_const_spec((1, D_MODEL)), _layer_spec(wgu_all.shape[1:], layer),
                 _layer_spec(wd_all.shape[1:], layer)]
    args += [norm_g.reshape(1, D_MODEL), wgu_all, wd_all]
    return pl.pallas_call(
        functools.partial(_ffn_kernel, len(pre)),
        grid=(t // FFN_ROWS,),
        in_specs=in_specs,
        out_specs=pl.BlockSpec((FFN_ROWS, D_MODEL), row),
        out_shape=jax.ShapeDtypeStruct((t, D_MODEL), F32),
        compiler_params=pltpu.CompilerParams(
            dimension_semantics=("parallel",), vmem_limit_bytes=VMEM_LIMIT),
        name="ffn",
    )(*args)


def _inproj_even_kernel(tiles_per_seq, x_ref, g_ref, w_ref, qg_ref, kg_ref, convw_ref,
                        qkv_ref, dgate_ref, fq_ref, fk_ref, fv_ref, fgate_ref, small_ref,
                        xx_ref):
    i = pl.program_id(0)
    n = FFN_ROWS
    h = _rms(x_ref[...], g_ref[...]).astype(BF16)
    c = 0

    def proj(width):
        nonlocal c
        y = jnp.dot(h, w_ref[:, c:c + width], preferred_element_type=F32)
        c += width
        return y

    @pl.when(i % tiles_per_seq == 0)
    def _():
        xx_ref[n:n + SUBLANES, :] = jnp.zeros((SUBLANES, 3 * D_DN), F32)

    xx_ref[0:SUBLANES, :] = xx_ref[n:n + SUBLANES, :]
    xx_ref[SUBLANES:SUBLANES + n, :] = proj(3 * D_DN)
    ones_sq = jnp.ones((HEAD_DIM, HEAD_DIM), BF16)

    def conv_tile(tile):
        sl = slice(tile * HEAD_DIM, (tile + 1) * HEAD_DIM)
        xx = xx_ref[:, sl]
        y = xx[SUBLANES:] * convw_ref[CONV_WIDTH - 1:CONV_WIDTH, sl]
        for d in range(1, CONV_WIDTH):
            tap = convw_ref[CONV_WIDTH - 1 - d:CONV_WIDTH - d, sl]
            y = y + pltpu.roll(xx, d, axis=0)[SUBLANES:] * tap
        y = _silu(y)
        if tile < 2 * N_DN_HEADS:
            ss = _dot_exact_rhs(y * y, ones_sq, 2)
            y = y * (lax.rsqrt(ss + EPS) * (SCALE if tile < N_DN_HEADS else 1.0))
        qkv_ref[:, sl] = y

    def fox_norm(val, gain_ref, out_ref):
        for hd in range(N_FOX_HEADS):
            sl = slice(hd * HEAD_DIM, (hd + 1) * HEAD_DIM)
            out_ref[:, sl] = _rms(val[:, sl], gain_ref[...]).astype(BF16)

    dgate_ref[...] = proj(D_DN)
    conv_tile(0)
    conv_tile(1)
    fq = proj(D_FOX)
    conv_tile(2)
    conv_tile(3)
    fk = proj(D_FOX)
    fox_norm(fq, qg_ref, fq_ref)
    conv_tile(4)
    conv_tile(5)
    fv_ref[...] = proj(D_FOX).astype(BF16)
    fox_norm(fk, kg_ref, fk_ref)
    conv_tile(6)
    conv_tile(7)
    fgate_ref[...] = proj(D_FOX)
    conv_tile(8)
    conv_tile(9)
    small_ref[...] = proj(LANES)
    conv_tile(10)
    conv_tile(11)


def _inproj_even_call(x, seq_len, norm_g, w_all, layer, q_g, k_g, conv_w):
    t = x.shape[0]
    row = lambda i: (i, 0)
    widths = [(3 * D_DN, F32), (D_DN, F32), (D_FOX, BF16), (D_FOX, BF16), (D_FOX, BF16),
              (D_FOX, F32), (LANES, F32)]
    return pl.pallas_call(
        functools.partial(_inproj_even_kernel, seq_len // FFN_ROWS),
        grid=(t // FFN_ROWS,),
        in_specs=[pl.BlockSpec((FFN_ROWS, D_MODEL), row), _const_spec((1, D_MODEL)),
                  _layer_spec(w_all.shape[1:], layer), _const_spec((1, HEAD_DIM)),
                  _const_spec((1, HEAD_DIM)), _const_spec((CONV_WIDTH, 3 * D_DN))],
        out_specs=[pl.BlockSpec((FFN_ROWS, n), row) for n, _ in widths],
        out_shape=[jax.ShapeDtypeStruct((t, n), dt) for n, dt in widths],
        scratch_shapes=[pltpu.VMEM((SUBLANES + FFN_ROWS, 3 * D_DN), F32)],
        compiler_params=pltpu.CompilerParams(
            dimension_semantics=("arbitrary",), vmem_limit_bytes=VMEM_LIMIT),
        name="inproj_even",
    )(x, norm_g.reshape(1, D_MODEL), w_all, q_g.reshape(1, HEAD_DIM), k_g.reshape(1, HEAD_DIM),
      conv_w)


def _inproj_odd_kernel(x_ref, g_ref, w_ref, o_ref):
    h = _rms(x_ref[...], g_ref[...]).astype(BF16)
    o_ref[...] = jnp.dot(h, w_ref[...], preferred_element_type=F32).astype(BF16)


def _inproj_odd_call(x, norm_g, w_all, layer):
    t = x.shape[0]
    n = w_all.shape[2]
    row = lambda i: (i, 0)
    return pl.pallas_call(
        _inproj_odd_kernel,
        grid=(t // FFN_ROWS,),
        in_specs=[pl.BlockSpec((FFN_ROWS, D_MODEL), row), _const_spec((1, D_MODEL)),
                  _layer_spec(w_all.shape[1:], layer)],
        out_specs=pl.BlockSpec((FFN_ROWS, n), row),
        out_shape=jax.ShapeDtypeStruct((t, n), BF16),
        compiler_params=pltpu.CompilerParams(
            dimension_semantics=("parallel",), vmem_limit_bytes=VMEM_LIMIT),
        name="inproj_odd",
    )(x, norm_g.reshape(1, D_MODEL), w_all)


def _deltanet_kernel(qkv_ref, gate_ref, small_ref, alog_ref, dtb_ref, ng_ref, o_ref,
                     state_ref, sums_ref):
    i = pl.program_id(1)
    n = DN_ROWS
    g = DN_GROUP
    heads = range(N_DN_HEADS)
    groups = range(n // g)

    row = _iota2(g, 0)
    col = _iota2(g, 1)
    diff = row ^ col
    same = (diff >> (DN_CHUNK.bit_length() - 1)) == 0
    incl = same & (col <= row)

    @pl.when(i == 0)
    def _():
        state_ref[...] = jnp.zeros_like(state_ref)
        sums_ref[0:g, :] = jnp.where(incl, 1.0, 0.0).astype(BF16)
        sums_ref[g:2 * g, :] = jnp.where(same, 1.0, 0.0).astype(BF16)

    sm = small_ref[...]
    beta_all = _sigmoid(sm)
    g_all = -jnp.exp(alog_ref[...]) * _softplus(sm + dtb_ref[...])

    hsl = lambda base_col, hd: slice(base_col + hd * HEAD_DIM, base_col + (hd + 1) * HEAD_DIM)
    rs = lambda gi: slice(gi * g, (gi + 1) * g)

    g_sums = [_dot_exact_lhs(sums_ref[...], g_all[rs(gi)], 3) for gi in groups]
    gc_all = [s[0:g] for s in g_sums]
    gl_all = [s[g:2 * g] for s in g_sums]
    gc_all_t = [c.T for c in gc_all]

    lane = lambda arr, hd: arr[:, LANE_DECAY + hd:LANE_DECAY + hd + 1]
    on_diag = diff == 0
    pair = diff == 1
    levels = []
    b = 2
    while b < DN_CHUNK:
        levels.append(b.bit_length() - 1)
        b *= 2
    n_chunks = g // DN_CHUNK
    q, k, gc, gl, egc, decay, rhs, a, x = {}, {}, {}, {}, {}, {}, {}, {}, {}
    sol, attn, qd, kd, egl = {}, {}, {}, {}, {}
    state = [state_ref[hd] for hd in heads]
    outs = {gi: [[] for _ in heads] for gi in groups}

    def prep(gi, hd):
        c = (gi, hd)
        q[c] = qkv_ref[rs(gi), hsl(0, hd)]
        k[c] = qkv_ref[rs(gi), hsl(D_DN, hd)]
        beta = beta_all[rs(gi), LANE_BETA + hd:LANE_BETA + hd + 1]
        gc[c] = lane(gc_all[gi], hd)
        gl[c] = lane(gl_all[gi], hd)
        egc[c] = jnp.exp(gc[c])
        decay[c] = jnp.where(
            incl, jnp.exp(gc[c] - gc_all_t[gi][LANE_DECAY + hd:LANE_DECAY + hd + 1, :]), 0.0)
        kb = k[c] * beta
        rhs[c] = jnp.concatenate(
            [qkv_ref[rs(gi), hsl(2 * D_DN, hd)] * beta, kb * egc[c]], axis=1)
        a[c] = jnp.where(on_diag, 0.0, _dot_nt(kb, k[c]) * decay[c])
        x[c] = jnp.where(on_diag, 1.0, jnp.where(pair, -a[c], 0.0))

    def invert_level(gi, sh):
        off = (diff >> sh) == 1
        t = {hd: _dot(x[gi, hd], jnp.where(off, a[gi, hd], 0.0)) for hd in heads}
        for hd in heads:
            x[gi, hd] = x[gi, hd] - _dot(t[hd], x[gi, hd])

    def solve(gi):
        for hd in heads:
            c = (gi, hd)
            sol[c] = _dot(x[c], rhs[c])
            attn[c] = _dot_nt(q[c], k[c]) * decay[c]
            qd[c] = q[c] * egc[c]
            kd[c] = k[c] * jnp.exp(gl[c] - gc[c])
            egl[c] = jnp.exp(gl[c])

    def scan_chunk(gi, ch):
        r = slice(ch * DN_CHUNK, (ch + 1) * DN_CHUNK)
        for hd in heads:
            c = (gi, hd)
            v_new = sol[c][r, :HEAD_DIM] - _dot(sol[c][r, HEAD_DIM:], state[hd])
            outs[gi][hd].append(_dot(qd[c][r], state[hd]) + _dot(attn[c][r, r], v_new))
            state[hd] = (state[hd] * egl[c][ch * DN_CHUNK:ch * DN_CHUNK + 1, :]
                         + _dot_tn(kd[c][r], v_new))

    def emit(gi):
        for hd in heads:
            o = jnp.concatenate(outs[gi][hd], axis=0)
            sl = hsl(0, hd)
            o_ref[rs(gi), sl] = (_rms(o, ng_ref[...]) * _silu(gate_ref[rs(gi), sl])).astype(BF16)

    for gi in groups:
        for hd in heads:
            prep(gi, hd)
    for sh in levels:
        for gi in groups:
            invert_level(gi, sh)
    for gi in groups:
        solve(gi)
    for gi in groups:
        for ch in range(n_chunks):
            scan_chunk(gi, ch)
        emit(gi)
    for hd in heads:
        state_ref[hd] = state[hd]


def _deltanet_call(qkv, gate, small, a_log, dt_bias, norm_g):
    b, s, _ = qkv.shape
    lane_vec = lambda v: jnp.zeros((1, LANES), F32).at[0, LANE_DECAY:LANE_DECAY + N_DN_HEADS].set(v)
    blk = lambda w: pl.BlockSpec((None, DN_ROWS, w), lambda bi, i: (bi, i, 0))
    return pl.pallas_call(
        _deltanet_kernel,
        grid=(b, s // DN_ROWS),
        in_specs=[blk(3 * D_DN), blk(D_DN), blk(LANES),
                  _const_spec((1, LANES)), _const_spec((1, LANES)), _const_spec((1, HEAD_DIM))],
        out_specs=blk(D_DN),
        out_shape=jax.ShapeDtypeStruct((b, s, D_DN), BF16),
        scratch_shapes=[pltpu.VMEM((N_DN_HEADS, HEAD_DIM, HEAD_DIM), F32),
                        pltpu.VMEM((2 * DN_GROUP, DN_GROUP), BF16)],
        compiler_params=pltpu.CompilerParams(
            dimension_semantics=("parallel", "arbitrary"), vmem_limit_bytes=VMEM_LIMIT),
        name="deltanet",
    )(qkv, gate, small, lane_vec(a_log), lane_vec(dt_bias), norm_g.reshape(1, HEAD_DIM))


def _fox_kernel(q_ref, k_ref, v_ref, small_ref, fbias_ref, gate_ref, o_ref, crow_ref, vaug_ref):
    i = pl.program_id(1)
    n = FOX_ROWS
    s = k_ref.shape[0]
    row = _iota2(n, 0)
    col = _iota2(n, 1)
    causal = col <= row

    @pl.when(i == 0)
    def _():
        tri = jnp.where(causal, 1.0, 0.0).astype(BF16)
        carry = jnp.zeros((1, LANES), F32)
        for blk in range(s // n):
            r = slice(blk * n, (blk + 1) * n)
            lf = _log_sigmoid(small_ref[r, :] + fbias_ref[...])
            c = _dot_exact_lhs(tri, lf, 3) + carry
            carry = c[n - 1:n, :]
            crow_ref[:, r] = c.T
        for hd in range(N_FOX_HEADS):
            vaug_ref[:, 2 * hd * HEAD_DIM:(2 * hd + 1) * HEAD_DIM] = (
                v_ref[:, hd * HEAD_DIM:(hd + 1) * HEAD_DIM])
            vaug_ref[:, (2 * hd + 1) * HEAD_DIM:(2 * hd + 2) * HEAD_DIM] = (
                jnp.ones((s, HEAD_DIM), BF16))

    hsl = lambda hd: slice(hd * HEAD_DIM, (hd + 1) * HEAD_DIM)

    for h0 in range(0, N_FOX_HEADS, FOX_HEADS_PER_LOOP):
        hds = range(h0, h0 + FOX_HEADS_PER_LOOP)

        def step(carry, k0, masked):
            ms = carry[0::2]
            accs = carry[1::2]
            logits = [_dot_nt(q_ref[:, hsl(hd)], k_ref[pl.ds(k0, n), hsl(hd)]) * SCALE
                      - crow_ref[LANE_FORGET + hd:LANE_FORGET + hd + 1, pl.ds(k0, n)]
                      for hd in hds]
            if masked:
                logits = [jnp.where(causal, lg, -1e30) for lg in logits]
            m_new = [jnp.maximum(m, jnp.max(lg, axis=-1, keepdims=True))
                     for m, lg in zip(ms, logits)]
            p = [jnp.exp(lg - mn) for lg, mn in zip(logits, m_new)]
            pv = [_dot(pp, vaug_ref[pl.ds(k0, n), 2 * hd * HEAD_DIM:(2 * hd + 2) * HEAD_DIM])
                  for pp, hd in zip(p, hds)]
            accs = [jnp.exp(m - mn) * acc + x for m, mn, acc, x in zip(ms, m_new, accs, pv)]
            out = []
            for mn, acc in zip(m_new, accs):
                out += [mn, acc]
            return tuple(out)

        init = (jnp.full((n, 1), -1e30, F32),
                jnp.zeros((n, 2 * HEAD_DIM), F32)) * FOX_HEADS_PER_LOOP
        carry = lax.fori_loop(0, i, lambda j, c: step(c, pl.multiple_of(j * n, n), False), init)
        carry = step(carry, pl.multiple_of(i * n, n), True)
        for idx, hd in enumerate(hds):
            sl = slice(hd * HEAD_DIM, (hd + 1) * HEAD_DIM)
            acc = carry[2 * idx + 1]
            o = acc[:, :HEAD_DIM] / acc[:, HEAD_DIM:]
            o_ref[:, sl] = (o * _sigmoid(gate_ref[:, sl])).astype(BF16)


def _fox_call(q, k, v, small, f_bias, gate):
    b, s, _ = q.shape
    blk = lambda w: pl.BlockSpec((None, FOX_ROWS, w), lambda bi, i: (bi, i, 0))
    full = lambda w: pl.BlockSpec((None, s, w), lambda bi, i: (bi, 0, 0))
    fb = jnp.zeros((1, LANES), F32).at[0, LANE_FORGET:LANE_FORGET + N_FOX_HEADS].set(f_bias)
    return pl.pallas_call(
        _fox_kernel,
        grid=(b, s // FOX_ROWS),
        in_specs=[blk(D_FOX), full(D_FOX), full(D_FOX), full(LANES), _const_spec((1, LANES)),
                  blk(D_FOX)],
        out_specs=blk(D_FOX),
        out_shape=jax.ShapeDtypeStruct((b, s, D_FOX), BF16),
        scratch_shapes=[pltpu.VMEM((LANES, s), F32), pltpu.VMEM((s, 2 * D_FOX), BF16)],
        compiler_params=pltpu.CompilerParams(
            dimension_semantics=("parallel", "arbitrary"), vmem_limit_bytes=VMEM_LIMIT),
        name="fox_attention",
    )(q, k, v, small, fb, gate)


SB_CUTOFF = -105.0


def _sb_kernel(q_ref, k_ref, v_ref, o_ref):
    n = SB_ROWS
    s = k_ref.shape[0]
    row = _iota2(n, 0)
    col = _iota2(n, 1)
    before = col < row
    later = jnp.where(row > col, 1.0, 0.0).astype(BF16)
    later2 = jnp.concatenate([later, later], axis=0)
    total = lambda lg: jnp.broadcast_to(jnp.sum(lg, axis=1, keepdims=True), (lg.shape[0], LANES))

    def block(q, k0, rest, masked):
        z = _dot_nt(q, k_ref[pl.ds(k0, n), :]) * SCALE
        soft = jnp.log(1.0 + jnp.exp(-jnp.abs(z)))
        lsz = jnp.minimum(z, 0.0) - soft
        lsn = lsz - z
        lg = jnp.where(before, lsn, 0.0) if masked else lsn
        hi = lg.astype(BF16)
        lo = (lg - hi.astype(F32)).astype(BF16)
        sums = jnp.dot(jnp.concatenate([hi, lo], axis=1), later2, preferred_element_type=F32)
        tail = sums + jnp.concatenate([rest] * (n // LANES), axis=1)
        a = jnp.exp(lsz + tail)
        if masked:
            a = jnp.where(before, a, 0.0)
        return _dot(a, v_ref[pl.ds(k0, n), :]), rest + total(lg)

    def pairs(qs, k_los, rests, masked):
        rep = lambda r: jnp.concatenate([r] * (n // LANES), axis=1)
        z = [_dot_nt(q, k_ref[pl.ds(k_lo, 2 * n), :]) * SCALE for q, k_lo in zip(qs, k_los)]
        lsz = [jnp.minimum(zz, 0.0) - jnp.log(1.0 + jnp.exp(-jnp.abs(zz))) for zz in z]
        lsn = [l - zz for l, zz in zip(lsz, z)]
        lg = [jnp.concatenate([jnp.where(before, l[:, n:], 0.0) if masked else l[:, n:],
                               l[:, :n]], axis=0) for l in lsn]
        hi = [l.astype(BF16) for l in lg]
        lo = [(l - h.astype(F32)).astype(BF16) for l, h in zip(lg, hi)]
        sums = [jnp.dot(jnp.concatenate([h, l], axis=1), later2, preferred_element_type=F32)
                for h, l in zip(hi, lo)]
        tot = [total(l) for l in lg]
        rest_mid = [r + t[:n] for r, t in zip(rests, tot)]
        tail = [jnp.concatenate([sm[n:] + rep(rm), sm[:n] + rep(r)], axis=1)
                for sm, rm, r in zip(sums, rest_mid, rests)]
        a = [jnp.exp(l + t) for l, t in zip(lsz, tail)]
        if masked:
            a = [jnp.concatenate([aa[:, :n], jnp.where(before, aa[:, n:], 0.0)], axis=1) for aa in a]
        pv = [_dot(aa, v_ref[pl.ds(k_lo, 2 * n), :]) for aa, k_lo in zip(a, k_los)]
        return pv, [rm + t[n:] for rm, t in zip(rest_mid, tot)]

    def pair(q, k_lo, rest, masked):
        pv, rest = pairs([q], [k_lo], [rest], masked)
        return pv[0], rest[0]

    def live(rest):
        return (jnp.max(rest) > SB_CUTOFF).astype(jnp.int32)

    acc0, _ = block(q_ref[0:n, :], 0, jnp.zeros((n, LANES), F32), True)
    o_ref[0:n, :] = acc0.astype(BF16)

    def first(idx):
        q0 = [pl.multiple_of(i * n, n) for i in idx]
        qs = [q_ref[pl.ds(s0, n), :] for s0 in q0]
        accs, rests = pairs(qs, [pl.multiple_of(s0 - n, n) for s0 in q0],
                            [jnp.zeros((n, LANES), F32)] * len(idx), True)
        return list(zip(qs, accs, rests))

    def finish(i, q, acc, rest):
        n_pairs = (i - 1) // 2

        def cond(c):
            return (c[0] < n_pairs) & (c[3] > 0)

        def body(c):
            t, acc, rest, _ = c
            k_lo = pl.multiple_of((i - 3 - 2 * t) * n, n)
            pv, rest = pair(q, k_lo, rest, False)
            return t + 1, acc + pv, rest, live(rest)

        _, acc, rest, alive = lax.while_loop(cond, body, (jnp.int32(0), acc, rest, live(rest)))
        acc = lax.cond(((i - 1) % 2 == 1) & (alive > 0),
                       lambda: acc + block(q, 0, rest, False)[0], lambda: acc)
        o_ref[pl.ds(pl.multiple_of(i * n, n), n), :] = acc.astype(BF16)

    def two_qblocks(p, _):
        idx = [1 + 2 * p, 2 + 2 * p]
        for i, start in zip(idx, first(idx)):
            finish(i, *start)
        return 0

    nq = s // n
    lax.fori_loop(0, (nq - 1) // 2, two_qblocks, 0)
    if (nq - 1) % 2 == 1:
        last = jnp.int32(nq - 1)
        finish(last, *first([last])[0])


def _sb_call(qkv):
    b, s, _ = qkv.shape
    nh = N_SB_HEADS
    head = lambda off: pl.BlockSpec((None, s, HEAD_DIM), lambda bi, h: (bi, 0, off + h))
    return pl.pallas_call(
        _sb_kernel,
        grid=(b, nh),
        in_specs=[head(0), head(nh), head(2 * nh)],
        out_specs=head(0),
        out_shape=jax.ShapeDtypeStruct((b, s, nh * HEAD_DIM), BF16),
        compiler_params=pltpu.CompilerParams(
            dimension_semantics=("parallel", "parallel"), vmem_limit_bytes=VMEM_LIMIT),
        name="stickbreaking_attention",
    )(qkv, qkv, qkv)


EVEN_W_ROWS = 256
EVEN_W_COLS = 4 * D_DN + 4 * D_FOX + LANES


def _even_w_in_kernel(wt_ref, o_ref):
    dn_wide = 4 * D_DN
    gates = 2 * N_DN_HEADS
    fox_wide = 4 * D_FOX
    wt = wt_ref[...]
    o_ref[:, :dn_wide] = wt[:dn_wide].T.astype(BF16)
    o_ref[:, dn_wide:dn_wide + fox_wide] = (
        wt[dn_wide + gates:dn_wide + gates + fox_wide].T.astype(BF16))
    small = jnp.concatenate(
        [wt[dn_wide:dn_wide + gates], wt[dn_wide + gates + fox_wide:],
         jnp.zeros((LANES - gates - N_FOX_HEADS, wt.shape[1]), F32)], axis=0)
    o_ref[:, dn_wide + fox_wide:] = small.T.astype(BF16)


def _even_w_in(w):
    layers, rows, cols = w.shape
    return pl.pallas_call(
        _even_w_in_kernel,
        grid=(layers, rows // EVEN_W_ROWS),
        in_specs=[pl.BlockSpec((None, cols, EVEN_W_ROWS), lambda l, r: (l, 0, r))],
        out_specs=pl.BlockSpec((None, EVEN_W_ROWS, EVEN_W_COLS), lambda l, r: (l, r, 0)),
        out_shape=jax.ShapeDtypeStruct((layers, rows, EVEN_W_COLS), BF16),
        compiler_params=pltpu.CompilerParams(
            dimension_semantics=("parallel", "parallel"), vmem_limit_bytes=VMEM_LIMIT),
        name="reorder_w_in_even",
    )(jnp.swapaxes(w, 1, 2))


def kernel(x, norm_ffn1, ffn1_w_gu, ffn1_w_down, norm_mix, w_in_even, dn_conv_w, dn_a_log,
           dn_dt_bias, dn_norm_g, fox_q_norm_g, fox_k_norm_g, fox_f_bias, w_out_even,
           w_in_odd, w_out_odd, norm_ffn2, ffn2_w_gu, ffn2_w_down):
    b, s, d = x.shape
    depth = norm_ffn1.shape[0]
    t = b * s
    bf = lambda a: a.astype(BF16)
    wgu1, wd1, wgu2, wd2 = bf(ffn1_w_gu), bf(ffn1_w_down), bf(ffn2_w_gu), bf(ffn2_w_down)
    w_in_e, w_out_e = _even_w_in(w_in_even), bf(w_out_even)
    w_in_o, w_out_o = bf(w_in_odd), bf(w_out_odd)
    x = x.reshape(t, d)
    for l in range(depth):
        j = l // 2
        x = _ffn_call(x, [], norm_ffn1[l], wgu1, wd1, l)
        if l % 2 == 0:
            qkv, dgate, fq, fk, fv, fgate, small = _inproj_even_call(
                x, s, norm_mix[l], w_in_e, j, fox_q_norm_g[j], fox_k_norm_g[j], dn_conv_w[j])
            r3 = lambda a: a.reshape(b, s, a.shape[-1])
            o_dn = _deltanet_call(r3(qkv), r3(dgate), r3(small), dn_a_log[j], dn_dt_bias[j],
                                  dn_norm_g[j])
            o_fox = _fox_call(r3(fq), r3(fk), r3(fv), r3(small), fox_f_bias[j], r3(fgate))
            pre = [(o_dn.reshape(t, D_DN), w_out_e, j, 0), (o_fox.reshape(t, D_FOX), w_out_e, j, 1)]
        else:
            qkv = _inproj_odd_call(x, norm_mix[l], w_in_o, j)
            o = _sb_call(qkv.reshape(b, s, qkv.shape[-1]))
            pre = [(o.reshape(t, o.shape[-1]), w_out_o, j, 0)]
        x = _ffn_call(x, pre, norm_ffn2[l], wgu2, wd2, l)
    return x.reshape(b, s, d)
```

```python
import functools

import jax
import jax.numpy as jnp
from jax import lax
from jax.experimental import pallas as pl
from jax.experimental.pallas import tpu as pltpu

F32 = jnp.float32
BF16 = jnp.bfloat16

D_MODEL = 1024
HEAD_DIM = 128
N_DN_HEADS = 4
N_FOX_HEADS = 4
N_SB_HEADS = 8
D_DN = N_DN_HEADS * HEAD_DIM
D_FOX = N_FOX_HEADS * HEAD_DIM
CONV_WIDTH = 4
D_FF = 2816
EPS = 1e-6
SCALE = HEAD_DIM ** -0.5

LANES = 128
SUBLANES = 8
VMEM_LIMIT = 56 * 1024 * 1024

FFN_ROWS = 512
SB_ROWS = 256
SB_QBLOCKS_PER_STEP = 5
FOX_ROWS = 512
FOX_HEADS_PER_LOOP = 4
DN_ROWS = 512
DN_GROUP = 256
DN_CHUNK = 256

LANE_BETA = 0
LANE_DECAY = N_DN_HEADS
LANE_FORGET = 2 * N_DN_HEADS


def _dot(a, b):
    return jnp.dot(a.astype(BF16), b.astype(BF16), preferred_element_type=F32)


def _dot_nt(a, b):
    return lax.dot_general(a.astype(BF16), b.astype(BF16), (((1,), (1,)), ((), ())),
                           preferred_element_type=F32)


def _dot_tn(a, b):
    return lax.dot_general(a.astype(BF16), b.astype(BF16), (((0,), (0,)), ((), ())),
                           preferred_element_type=F32)


def _split_bf16(x, parts):
    out = []
    r = x
    for _ in range(parts - 1):
        p = r.astype(BF16)
        out.append(p)
        r = r - p.astype(F32)
    out.append(r.astype(BF16))
    return out


def _dot_exact_lhs(m_bf16, x, parts):
    acc = None
    for p in _split_bf16(x, parts):
        t = jnp.dot(m_bf16, p, preferred_element_type=F32)
        acc = t if acc is None else acc + t
    return acc


def _dot_exact_rhs(x, m_bf16, parts):
    acc = None
    for p in _split_bf16(x, parts):
        t = jnp.dot(p, m_bf16, preferred_element_type=F32)
        acc = t if acc is None else acc + t
    return acc


def _rms(xf, g):
    return xf * lax.rsqrt(jnp.mean(xf * xf, axis=-1, keepdims=True) + EPS) * g


def _softplus(x):
    return jnp.maximum(x, 0.0) + jnp.log1p(jnp.exp(-jnp.abs(x)))


def _log_sigmoid(x):
    return jnp.minimum(x, 0.0) - jnp.log1p(jnp.exp(-jnp.abs(x)))


def _sigmoid(x):
    return 1.0 / (1.0 + jnp.exp(-x))


def _silu(x):
    return x * _sigmoid(x)


def _iota2(n, axis):
    return lax.broadcasted_iota(jnp.int32, (n, n), axis)


def _const_spec(shape):
    return pl.BlockSpec(shape, lambda *_: (0,) * len(shape))


def _layer_spec(shape, layer, row_block=0):
    return pl.BlockSpec((None,) + tuple(shape), lambda *_: (layer, row_block, 0))


def _ffn_kernel(n_pre, *refs):
    x_ref = refs[0]
    pre = refs[1:1 + 2 * n_pre]
    g_ref, wgu_ref, wd_ref, o_ref = refs[1 + 2 * n_pre:]
    x = x_ref[...]
    for p in range(n_pre):
        x = x + jnp.dot(pre[2 * p][...], pre[2 * p + 1][...], preferred_element_type=F32)
    h = _rms(x, g_ref[...]).astype(BF16)
    gate = jnp.dot(h, wgu_ref[:, :D_FF], preferred_element_type=F32)
    up = jnp.dot(h, wgu_ref[:, D_FF:], preferred_element_type=F32)
    act = (_silu(gate) * up).astype(BF16)
    o_ref[...] = x + 0.5 * jnp.dot(act, wd_ref[...], preferred_element_type=F32)


def _ffn_call(x, pre, norm_g, wgu_all, wd_all, layer):
    t = x.shape[0]
    row = lambda i: (i, 0)
    in_specs = [pl.BlockSpec((FFN_ROWS, D_MODEL), row)]
    args = [x]
    for o, w_all, j, rb in pre:
        in_specs += [pl.BlockSpec((FFN_ROWS, o.shape[1]), row),
                     _layer_spec((o.shape[1], D_MODEL), j, rb)]
        args += [o, w_all]
    in_specs += [_const_spec((1, D_MODEL)), _layer_spec(wgu_all.shape[1:], layer),
                 _layer_spec(wd_all.shape[1:], layer)]
    args += [norm_g.reshape(1, D_MODEL), wgu_all, wd_all]
    return pl.pallas_call(
        functools.partial(_ffn_kernel, len(pre)),
        grid=(t // FFN_ROWS,),
        in_specs=in_specs,
        out_specs=pl.BlockSpec((FFN_ROWS, D_MODEL), row),
        out_shape=jax.ShapeDtypeStruct((t, D_MODEL), F32),
        compiler_params=pltpu.CompilerParams(
            dimension_semantics=("parallel",), vmem_limit_bytes=VMEM_LIMIT),
        name="ffn",
    )(*args)


def _inproj_even_kernel(tiles_per_seq, x_ref, g_ref, w_ref, qg_ref, kg_ref, convw_ref,
                        qkv_ref, dgate_ref, fq_ref, fk_ref, fv_ref, fgate_ref, small_ref,
                        xx_ref):
    i = pl.program_id(0)
    n = FFN_ROWS
    h = _rms(x_ref[...], g_ref[...]).astype(BF16)
    c = 0

    def proj(width):
        nonlocal c
        y = jnp.dot(h, w_ref[:, c:c + width], preferred_element_type=F32)
        c += width
        return y

    @pl.when(i % tiles_per_seq == 0)
    def _():
        xx_ref[n:n + SUBLANES, :] = jnp.zeros((SUBLANES, 3 * D_DN), F32)

    xx_ref[0:SUBLANES, :] = xx_ref[n:n + SUBLANES, :]
    xx_ref[SUBLANES:SUBLANES + n, :] = proj(3 * D_DN)
    ones_sq = jnp.ones((HEAD_DIM, HEAD_DIM), BF16)

    def conv_tile(tile):
        sl = slice(tile * HEAD_DIM, (tile + 1) * HEAD_DIM)
        xx = xx_ref[:, sl]
        y = xx[SUBLANES:] * convw_ref[CONV_WIDTH - 1:CONV_WIDTH, sl]
        for d in range(1, CONV_WIDTH):
            tap = convw_ref[CONV_WIDTH - 1 - d:CONV_WIDTH - d, sl]
            y = y + pltpu.roll(xx, d, axis=0)[SUBLANES:] * tap
        y = _silu(y)
        if tile < 2 * N_DN_HEADS:
            ss = _dot_exact_rhs(y * y, ones_sq, 2)
            y = y * (lax.rsqrt(ss + EPS) * (SCALE if tile < N_DN_HEADS else 1.0))
        qkv_ref[:, sl] = y

    def fox_norm(val, gain_ref, out_ref):
        for hd in range(N_FOX_HEADS):
            sl = slice(hd * HEAD_DIM, (hd + 1) * HEAD_DIM)
            out_ref[:, sl] = _rms(val[:, sl], gain_ref[...]).astype(BF16)

    dgate_ref[...] = proj(D_DN)
    conv_tile(0)
    conv_tile(1)
    fq = proj(D_FOX)
    conv_tile(2)
    conv_tile(3)
    fk = proj(D_FOX)
    fox_norm(fq, qg_ref, fq_ref)
    conv_tile(4)
    conv_tile(5)
    fv_ref[...] = proj(D_FOX).astype(BF16)
    fox_norm(fk, kg_ref, fk_ref)
    conv_tile(6)
    conv_tile(7)
    fgate_ref[...] = proj(D_FOX)
    conv_tile(8)
    conv_tile(9)
    small_ref[...] = proj(LANES)
    conv_tile(10)
    conv_tile(11)


def _inproj_even_call(x, seq_len, norm_g, w_all, layer, q_g, k_g, conv_w):
    t = x.shape[0]
    row = lambda i: (i, 0)
    widths = [(3 * D_DN, F32), (D_DN, F32), (D_FOX, BF16), (D_FOX, BF16), (D_FOX, BF16),
              (D_FOX, F32), (LANES, F32)]
    return pl.pallas_call(
        functools.partial(_inproj_even_kernel, seq_len // FFN_ROWS),
        grid=(t // FFN_ROWS,),
        in_specs=[pl.BlockSpec((FFN_ROWS, D_MODEL), row), _const_spec((1, D_MODEL)),
                  _layer_spec(w_all.shape[1:], layer), _const_spec((1, HEAD_DIM)),
                  _const_spec((1, HEAD_DIM)), _const_spec((CONV_WIDTH, 3 * D_DN))],
        out_specs=[pl.BlockSpec((FFN_ROWS, n), row) for n, _ in widths],
        out_shape=[jax.ShapeDtypeStruct((t, n), dt) for n, dt in widths],
        scratch_shapes=[pltpu.VMEM((SUBLANES + FFN_ROWS, 3 * D_DN), F32)],
        compiler_params=pltpu.CompilerParams(
            dimension_semantics=("arbitrary",), vmem_limit_bytes=VMEM_LIMIT),
        name="inproj_even",
    )(x, norm_g.reshape(1, D_MODEL), w_all, q_g.reshape(1, HEAD_DIM), k_g.reshape(1, HEAD_DIM),
      conv_w)


def _inproj_odd_kernel(x_ref, g_ref, w_ref, o_ref):
    h = _rms(x_ref[...], g_ref[...]).astype(BF16)
    o_ref[...] = jnp.dot(h, w_ref[...], preferred_element_type=F32).astype(BF16)


def _inproj_odd_call(x, norm_g, w_all, layer):
    t = x.shape[0]
    n = w_all.shape[2]
    row = lambda i: (i, 0)
    return pl.pallas_call(
        _inproj_odd_kernel,
        grid=(t // FFN_ROWS,),
        in_specs=[pl.BlockSpec((FFN_ROWS, D_MODEL), row), _const_spec((1, D_MODEL)),
                  _layer_spec(w_all.shape[1:], layer)],
        out_specs=pl.BlockSpec((FFN_ROWS, n), row),
        out_shape=jax.ShapeDtypeStruct((t, n), BF16),
        compiler_params=pltpu.CompilerParams(
            dimension_semantics=("parallel",), vmem_limit_bytes=VMEM_LIMIT),
        name="inproj_odd",
    )(x, norm_g.reshape(1, D_MODEL), w_all)


def _deltanet_kernel(qkv_ref, gate_ref, small_ref, alog_ref, dtb_ref, ng_ref, o_ref,
                     state_ref, sums_ref):
    i = pl.program_id(1)
    n = DN_ROWS
    g = DN_GROUP
    heads = range(N_DN_HEADS)
    groups = range(n // g)

    row = _iota2(g, 0)
    col = _iota2(g, 1)
    diff = row ^ col
    same = (diff >> (DN_CHUNK.bit_length() - 1)) == 0
    incl = same & (col <= row)

    @pl.when(i == 0)
    def _():
        state_ref[...] = jnp.zeros_like(state_ref)
        sums_ref[0:g, :] = jnp.where(incl, 1.0, 0.0).astype(BF16)
        sums_ref[g:2 * g, :] = jnp.where(same, 1.0, 0.0).astype(BF16)

    sm = small_ref[...]
    beta_all = _sigmoid(sm)
    g_all = -jnp.exp(alog_ref[...]) * _softplus(sm + dtb_ref[...])

    hsl = lambda base_col, hd: slice(base_col + hd * HEAD_DIM, base_col + (hd + 1) * HEAD_DIM)
    rs = lambda gi: slice(gi * g, (gi + 1) * g)

    g_sums = [_dot_exact_lhs(sums_ref[...], g_all[rs(gi)], 3) for gi in groups]
    gc_all = [s[0:g] for s in g_sums]
    gl_all = [s[g:2 * g] for s in g_sums]
    gc_all_t = [c.T for c in gc_all]

    lane = lambda arr, hd: arr[:, LANE_DECAY + hd:LANE_DECAY + hd + 1]
    on_diag = diff == 0
    pair = diff == 1
    levels = []
    b = 2
    while b < DN_CHUNK:
        levels.append(b.bit_length() - 1)
        b *= 2
    n_chunks = g // DN_CHUNK
    q, k, gc, gl, egc, decay, rhs, a, x = {}, {}, {}, {}, {}, {}, {}, {}, {}
    sol, attn, qd, kd, egl = {}, {}, {}, {}, {}
    state = [state_ref[hd] for hd in heads]
    outs = {gi: [[] for _ in heads] for gi in groups}

    def prep(gi, hd):
        c = (gi, hd)
        q[c] = qkv_ref[rs(gi), hsl(0, hd)]
        k[c] = qkv_ref[rs(gi), hsl(D_DN, hd)]
        beta = beta_all[rs(gi), LANE_BETA + hd:LANE_BETA + hd + 1]
        gc[c] = lane(gc_all[gi], hd)
        gl[c] = lane(gl_all[gi], hd)
        egc[c] = jnp.exp(gc[c])
        decay[c] = jnp.where(
            incl, jnp.exp(gc[c] - gc_all_t[gi][LANE_DECAY + hd:LANE_DECAY + hd + 1, :]), 0.0)
        kb = k[c] * beta
        rhs[c] = jnp.concatenate(
            [qkv_ref[rs(gi), hsl(2 * D_DN, hd)] * beta, kb * egc[c]], axis=1)
        a[c] = jnp.where(on_diag, 0.0, _dot_nt(kb, k[c]) * decay[c])
        x[c] = jnp.where(on_diag, 1.0, jnp.where(pair, -a[c], 0.0))

    def invert_level(gi, sh):
        off = (diff >> sh) == 1
        t = {hd: _dot(x[gi, hd], jnp.where(off, a[gi, hd], 0.0)) for hd in heads}
        for hd in heads:
            x[gi, hd] = x[gi, hd] - _dot(t[hd], x[gi, hd])

    def solve(gi):
        for hd in heads:
            c = (gi, hd)
            sol[c] = _dot(x[c], rhs[c])
            attn[c] = _dot_nt(q[c], k[c]) * decay[c]
            qd[c] = q[c] * egc[c]
            kd[c] = k[c] * jnp.exp(gl[c] - gc[c])
            egl[c] = jnp.exp(gl[c])

    def scan_chunk(gi, ch):
        r = slice(ch * DN_CHUNK, (ch + 1) * DN_CHUNK)
        for hd in heads:
            c = (gi, hd)
            v_new = sol[c][r, :HEAD_DIM] - _dot(sol[c][r, HEAD_DIM:], state[hd])
            outs[gi][hd].append(_dot(qd[c][r], state[hd]) + _dot(attn[c][r, r], v_new))
            state[hd] = (state[hd] * egl[c][ch * DN_CHUNK:ch * DN_CHUNK + 1, :]
                         + _dot_tn(kd[c][r], v_new))

    def emit(gi):
        for hd in heads:
            o = jnp.concatenate(outs[gi][hd], axis=0)
            sl = hsl(0, hd)
            o_ref[rs(gi), sl] = (_rms(o, ng_ref[...]) * _silu(gate_ref[rs(gi), sl])).astype(BF16)

    for gi in groups:
        for hd in heads:
            prep(gi, hd)
    for sh in levels:
        for gi in groups:
            invert_level(gi, sh)
    for gi in groups:
        solve(gi)
    for gi in groups:
        for ch in range(n_chunks):
            scan_chunk(gi, ch)
        emit(gi)
    for hd in heads:
        state_ref[hd] = state[hd]


def _deltanet_call(qkv, gate, small, a_log, dt_bias, norm_g):
    b, s, _ = qkv.shape
    lane_vec = lambda v: jnp.zeros((1, LANES), F32).at[0, LANE_DECAY:LANE_DECAY + N_DN_HEADS].set(v)
    blk = lambda w: pl.BlockSpec((None, DN_ROWS, w), lambda bi, i: (bi, i, 0))
    return pl.pallas_call(
        _deltanet_kernel,
        grid=(b, s // DN_ROWS),
        in_specs=[blk(3 * D_DN), blk(D_DN), blk(LANES),
                  _const_spec((1, LANES)), _const_spec((1, LANES)), _const_spec((1, HEAD_DIM))],
        out_specs=blk(D_DN),
        out_shape=jax.ShapeDtypeStruct((b, s, D_DN), BF16),
        scratch_shapes=[pltpu.VMEM((N_DN_HEADS, HEAD_DIM, HEAD_DIM), F32),
                        pltpu.VMEM((2 * DN_GROUP, DN_GROUP), BF16)],
        compiler_params=pltpu.CompilerParams(
            dimension_semantics=("parallel", "arbitrary"), vmem_limit_bytes=VMEM_LIMIT),
        name="deltanet",
    )(qkv, gate, small, lane_vec(a_log), lane_vec(dt_bias), norm_g.reshape(1, HEAD_DIM))


def _fox_kernel(q_ref, k_ref, v_ref, small_ref, fbias_ref, gate_ref, o_ref, crow_ref, vaug_ref):
    i = pl.program_id(1)
    n = FOX_ROWS
    s = k_ref.shape[0]
    row = _iota2(n, 0)
    col = _iota2(n, 1)
    causal = col <= row

    @pl.when(i == 0)
    def _():
        tri = jnp.where(causal, 1.0, 0.0).astype(BF16)
        carry = jnp.zeros((1, LANES), F32)
        for blk in range(s // n):
            r = slice(blk * n, (blk + 1) * n)
            lf = _log_sigmoid(small_ref[r, :] + fbias_ref[...])
            c = _dot_exact_lhs(tri, lf, 3) + carry
            carry = c[n - 1:n, :]
            crow_ref[:, r] = c.T
        for hd in range(N_FOX_HEADS):
            vaug_ref[:, 2 * hd * HEAD_DIM:(2 * hd + 1) * HEAD_DIM] = (
                v_ref[:, hd * HEAD_DIM:(hd + 1) * HEAD_DIM])
            vaug_ref[:, (2 * hd + 1) * HEAD_DIM:(2 * hd + 2) * HEAD_DIM] = (
                jnp.ones((s, HEAD_DIM), BF16))

    hsl = lambda hd: slice(hd * HEAD_DIM, (hd + 1) * HEAD_DIM)

    for h0 in range(0, N_FOX_HEADS, FOX_HEADS_PER_LOOP):
        hds = range(h0, h0 + FOX_HEADS_PER_LOOP)

        def step(carry, k0, masked):
            ms = carry[0::2]
            accs = carry[1::2]
            logits = [_dot_nt(q_ref[:, hsl(hd)], k_ref[pl.ds(k0, n), hsl(hd)]) * SCALE
                      - crow_ref[LANE_FORGET + hd:LANE_FORGET + hd + 1, pl.ds(k0, n)]
                      for hd in hds]
            if masked:
                logits = [jnp.where(causal, lg, -1e30) for lg in logits]
            m_new = [jnp.maximum(m, jnp.max(lg, axis=-1, keepdims=True))
                     for m, lg in zip(ms, logits)]
            p = [jnp.exp(lg - mn) for lg, mn in zip(logits, m_new)]
            pv = [_dot(pp, vaug_ref[pl.ds(k0, n), 2 * hd * HEAD_DIM:(2 * hd + 2) * HEAD_DIM])
                  for pp, hd in zip(p, hds)]
            accs = [jnp.exp(m - mn) * acc + x for m, mn, acc, x in zip(ms, m_new, accs, pv)]
            out = []
            for mn, acc in zip(m_new, accs):
                out += [mn, acc]
            return tuple(out)

        init = (jnp.full((n, 1), -1e30, F32),
                jnp.zeros((n, 2 * HEAD_DIM), F32)) * FOX_HEADS_PER_LOOP
        carry = lax.fori_loop(0, i, lambda j, c: step(c, pl.multiple_of(j * n, n), False), init)
        carry = step(carry, pl.multiple_of(i * n, n), True)
        for idx, hd in enumerate(hds):
            sl = slice(hd * HEAD_DIM, (hd + 1) * HEAD_DIM)
            acc = carry[2 * idx + 1]
            o = acc[:, :HEAD_DIM] / acc[:, HEAD_DIM:]
            o_ref[:, sl] = (o * _sigmoid(gate_ref[:, sl])).astype(BF16)


def _fox_call(q, k, v, small, f_bias, gate):
    b, s, _ = q.shape
    blk = lambda w: pl.BlockSpec((None, FOX_ROWS, w), lambda bi, i: (bi, i, 0))
    full = lambda w: pl.BlockSpec((None, s, w), lambda bi, i: (bi, 0, 0))
    fb = jnp.zeros((1, LANES), F32).at[0, LANE_FORGET:LANE_FORGET + N_FOX_HEADS].set(f_bias)
    return pl.pallas_call(
        _fox_kernel,
        grid=(b, s // FOX_ROWS),
        in_specs=[blk(D_FOX), full(D_FOX), full(D_FOX), full(LANES), _const_spec((1, LANES)),
                  blk(D_FOX)],
        out_specs=blk(D_FOX),
        out_shape=jax.ShapeDtypeStruct((b, s, D_FOX), BF16),
        scratch_shapes=[pltpu.VMEM((LANES, s), F32), pltpu.VMEM((s, 2 * D_FOX), BF16)],
        compiler_params=pltpu.CompilerParams(
            dimension_semantics=("parallel", "arbitrary"), vmem_limit_bytes=VMEM_LIMIT),
        name="fox_attention",
    )(q, k, v, small, fb, gate)


SB_CUTOFF = -105.0


def _sb_kernel(q_ref, k_ref, v_ref, o_ref):
    n = SB_ROWS
    s = k_ref.shape[0]
    row = _iota2(n, 0)
    col = _iota2(n, 1)
    before = col < row
    later = jnp.where(row > col, 1.0, 0.0).astype(BF16)
    later2 = jnp.concatenate([later, later], axis=0)
    total = lambda lg: jnp.broadcast_to(jnp.sum(lg, axis=1, keepdims=True), (lg.shape[0], LANES))

    def block(q, k0, rest, masked):
        z = _dot_nt(q, k_ref[pl.ds(k0, n), :]) * SCALE
        soft = jnp.log(1.0 + jnp.exp(-jnp.abs(z)))
        lsz = jnp.minimum(z, 0.0) - soft
        lsn = lsz - z
        lg = jnp.where(before, lsn, 0.0) if masked else lsn
        hi = lg.astype(BF16)
        lo = (lg - hi.astype(F32)).astype(BF16)
        sums = jnp.dot(jnp.concatenate([hi, lo], axis=1), later2, preferred_element_type=F32)
        tail = sums + jnp.concatenate([rest] * (n // LANES), axis=1)
        a = jnp.exp(lsz + tail)
        if masked:
            a = jnp.where(before, a, 0.0)
        return _dot(a, v_ref[pl.ds(k0, n), :]), rest + total(lg)

    def pairs(qs, k_los, rests, masked):
        rep = lambda r: jnp.concatenate([r] * (n // LANES), axis=1)
        z = [_dot_nt(q, k_ref[pl.ds(k_lo, 2 * n), :]) * SCALE for q, k_lo in zip(qs, k_los)]
        lsz = [jnp.minimum(zz, 0.0) - jnp.log(1.0 + jnp.exp(-jnp.abs(zz))) for zz in z]
        lsn = [l - zz for l, zz in zip(lsz, z)]
        lg = [jnp.concatenate([jnp.where(before, l[:, n:], 0.0) if masked else l[:, n:],
                               l[:, :n]], axis=0) for l in lsn]
        hi = [l.astype(BF16) for l in lg]
        lo = [(l - h.astype(F32)).astype(BF16) for l, h in zip(lg, hi)]
        sums = [jnp.dot(jnp.concatenate([h, l], axis=1), later2, preferred_element_type=F32)
                for h, l in zip(hi, lo)]
        tot = [total(l) for l in lg]
        rest_mid = [r + t[:n] for r, t in zip(rests, tot)]
        tail = [jnp.concatenate([sm[n:] + rep(rm), sm[:n] + rep(r)], axis=1)
                for sm, rm, r in zip(sums, rest_mid, rests)]
        a = [jnp.exp(l + t) for l, t in zip(lsz, tail)]
        if masked:
            a = [jnp.concatenate([aa[:, :n], jnp.where(before, aa[:, n:], 0.0)], axis=1) for aa in a]
        pv = [_dot(aa, v_ref[pl.ds(k_lo, 2 * n), :]) for aa, k_lo in zip(a, k_los)]
        return pv, [rm + t[n:] for rm, t in zip(rest_mid, tot)]

    def pair(q, k_lo, rest, masked):
        pv, rest = pairs([q], [k_lo], [rest], masked)
        return pv[0], rest[0]

    def live(rest):
        return (jnp.max(rest) > SB_CUTOFF).astype(jnp.int32)

    acc0, _ = block(q_ref[0:n, :], 0, jnp.zeros((n, LANES), F32), True)
    o_ref[0:n, :] = acc0.astype(BF16)

    def first(idx):
        q0 = [pl.multiple_of(i * n, n) for i in idx]
        qs = [q_ref[pl.ds(s0, n), :] for s0 in q0]
        accs, rests = pairs(qs, [pl.multiple_of(s0 - n, n) for s0 in q0],
                            [jnp.zeros((n, LANES), F32)] * len(idx), True)
        return list(zip(qs, accs, rests))

    def finish(i, q, acc, rest):
        n_pairs = (i - 1) // 2

        def cond(c):
            return (c[0] < n_pairs) & (c[3] > 0)

        def body(c):
            t, acc, rest, _ = c
            k_lo = pl.multiple_of((i - 3 - 2 * t) * n, n)
            pv, rest = pair(q, k_lo, rest, False)
            return t + 1, acc + pv, rest, live(rest)

        _, acc, rest, alive = lax.while_loop(cond, body, (jnp.int32(0), acc, rest, live(rest)))
        acc = lax.cond(((i - 1) % 2 == 1) & (alive > 0),
                       lambda: acc + block(q, 0, rest, False)[0], lambda: acc)
        o_ref[pl.ds(pl.multiple_of(i * n, n), n), :] = acc.astype(BF16)

    def qblock_group(idx):
        for i, start in zip(idx, first(idx)):
            finish(i, *start)

    def group_step(p, _):
        qblock_group([1 + SB_QBLOCKS_PER_STEP * p + t for t in range(SB_QBLOCKS_PER_STEP)])
        return 0

    nq = s // n
    n_groups = (nq - 1) // SB_QBLOCKS_PER_STEP
    lax.fori_loop(0, n_groups, group_step, 0)
    rest_blocks = [jnp.int32(i) for i in range(1 + n_groups * SB_QBLOCKS_PER_STEP, nq)]
    if rest_blocks:
        qblock_group(rest_blocks)


def _sb_call(qkv):
    b, s, _ = qkv.shape
    nh = N_SB_HEADS
    head = lambda off: pl.BlockSpec((None, s, HEAD_DIM), lambda bi, h: (bi, 0, off + h))
    return pl.pallas_call(
        _sb_kernel,
        grid=(b, nh),
        in_specs=[head(0), head(nh), head(2 * nh)],
        out_specs=head(0),
        out_shape=jax.ShapeDtypeStruct((b, s, nh * HEAD_DIM), BF16),
        compiler_params=pltpu.CompilerParams(
            dimension_semantics=("parallel", "parallel"), vmem_limit_bytes=VMEM_LIMIT),
        name="stickbreaking_attention",
    )(qkv, qkv, qkv)


EVEN_W_ROWS = 256
EVEN_W_COLS = 4 * D_DN + 4 * D_FOX + LANES


def _even_w_in_kernel(wt_ref, o_ref):
    dn_wide = 4 * D_DN
    gates = 2 * N_DN_HEADS
    fox_wide = 4 * D_FOX
    wt = wt_ref[...]
    o_ref[:, :dn_wide] = wt[:dn_wide].T.astype(BF16)
    o_ref[:, dn_wide:dn_wide + fox_wide] = (
        wt[dn_wide + gates:dn_wide + gates + fox_wide].T.astype(BF16))
    small = jnp.concatenate(
        [wt[dn_wide:dn_wide + gates], wt[dn_wide + gates + fox_wide:],
         jnp.zeros((LANES - gates - N_FOX_HEADS, wt.shape[1]), F32)], axis=0)
    o_ref[:, dn_wide + fox_wide:] = small.T.astype(BF16)


def _even_w_in(w):
    layers, rows, cols = w.shape
    return pl.pallas_call(
        _even_w_in_kernel,
        grid=(layers, rows // EVEN_W_ROWS),
        in_specs=[pl.BlockSpec((None, cols, EVEN_W_ROWS), lambda l, r: (l, 0, r))],
        out_specs=pl.BlockSpec((None, EVEN_W_ROWS, EVEN_W_COLS), lambda l, r: (l, r, 0)),
        out_shape=jax.ShapeDtypeStruct((layers, rows, EVEN_W_COLS), BF16),
        compiler_params=pltpu.CompilerParams(
            dimension_semantics=("parallel", "parallel"), vmem_limit_bytes=VMEM_LIMIT),
        name="reorder_w_in_even",
    )(jnp.swapaxes(w, 1, 2))


def kernel(x, norm_ffn1, ffn1_w_gu, ffn1_w_down, norm_mix, w_in_even, dn_conv_w, dn_a_log,
           dn_dt_bias, dn_norm_g, fox_q_norm_g, fox_k_norm_g, fox_f_bias, w_out_even,
           w_in_odd, w_out_odd, norm_ffn2, ffn2_w_gu, ffn2_w_down):
    b, s, d = x.shape
    depth = norm_ffn1.shape[0]
    t = b * s
    bf = lambda a: a.astype(BF16)
    wgu1, wd1, wgu2, wd2 = bf(ffn1_w_gu), bf(ffn1_w_down), bf(ffn2_w_gu), bf(ffn2_w_down)
    w_in_e, w_out_e = _even_w_in(w_in_even), bf(w_out_even)
    w_in_o, w_out_o = bf(w_in_odd), bf(w_out_odd)
    x = x.reshape(t, d)
    for l in range(depth):
        j = l // 2
        x = _ffn_call(x, [], norm_ffn1[l], wgu1, wd1, l)
        if l % 2 == 0:
            qkv, dgate, fq, fk, fv, fgate, small = _inproj_even_call(
                x, s, norm_mix[l], w_in_e, j, fox_q_norm_g[j], fox_k_norm_g[j], dn_conv_w[j])
            r3 = lambda a: a.reshape(b, s, a.shape[-1])
            o_dn = _deltanet_call(r3(qkv), r3(dgate), r3(small), dn_a_log[j], dn_dt_bias[j],
                                  dn_norm_g[j])
            o_fox = _fox_call(r3(fq), r3(fk), r3(fv), r3(small), fox_f_bias[j], r3(fgate))
            pre = [(o_dn.reshape(t, D_DN), w_out_e, j, 0), (o_fox.reshape(t, D_FOX), w_out_e, j, 1)]
        else:
            qkv = _inproj_odd_call(x, norm_mix[l], w_in_o, j)
            o = _sb_call(qkv.reshape(b, s, qkv.shape[-1]))
            pre = [(o.reshape(t, o.shape[-1]), w_out_o, j, 0)]
        x = _ffn_call(x, pre, norm_ffn2[l], wgu2, wd2, l)
    return x.reshape(b, s, d)
```

```python
import functools

import jax
import jax.numpy as jnp
from jax import lax
from jax.experimental import pallas as pl
from jax.experimental.pallas import tpu as pltpu

F32 = jnp.float32
BF16 = jnp.bfloat16

D_MODEL = 1024
HEAD_DIM = 128
N_DN_HEADS = 4
N_FOX_HEADS = 4
N_SB_HEADS = 8
D_DN = N_DN_HEADS * HEAD_DIM
D_FOX = N_FOX_HEADS * HEAD_DIM
CONV_WIDTH = 4
D_FF = 2816
EPS = 1e-6
SCALE = HEAD_DIM ** -0.5

LANES = 128
SUBLANES = 8
VMEM_LIMIT = 56 * 1024 * 1024

FFN_ROWS = 512
SB_ROWS = 256
SB_QBLOCKS_PER_STEP = 5
FOX_ROWS = 512
FOX_SAFE_LOGIT_GAP = 60.0
FOX_HEADS_PER_LOOP = 4
DN_ROWS = 512
DN_GROUP = 256
DN_CHUNK = 256

LANE_BETA = 0
LANE_DECAY = N_DN_HEADS
LANE_FORGET = 2 * N_DN_HEADS


def _dot(a, b):
    return jnp.dot(a.astype(BF16), b.astype(BF16), preferred_element_type=F32)


def _dot_nt(a, b):
    return lax.dot_general(a.astype(BF16), b.astype(BF16), (((1,), (1,)), ((), ())),
                           preferred_element_type=F32)


def _dot_tn(a, b):
    return lax.dot_general(a.astype(BF16), b.astype(BF16), (((0,), (0,)), ((), ())),
                           preferred_element_type=F32)


def _split_bf16(x, parts):
    out = []
    r = x
    for _ in range(parts - 1):
        p = r.astype(BF16)
        out.append(p)
        r = r - p.astype(F32)
    out.append(r.astype(BF16))
    return out


def _dot_exact_lhs(m_bf16, x, parts):
    acc = None
    for p in _split_bf16(x, parts):
        t = jnp.dot(m_bf16, p, preferred_element_type=F32)
        acc = t if acc is None else acc + t
    return acc


def _dot_exact_rhs(x, m_bf16, parts):
    acc = None
    for p in _split_bf16(x, parts):
        t = jnp.dot(p, m_bf16, preferred_element_type=F32)
        acc = t if acc is None else acc + t
    return acc


def _rms(xf, g):
    return xf * lax.rsqrt(jnp.mean(xf * xf, axis=-1, keepdims=True) + EPS) * g


def _softplus(x):
    return jnp.maximum(x, 0.0) + jnp.log1p(jnp.exp(-jnp.abs(x)))


def _log_sigmoid(x):
    return jnp.minimum(x, 0.0) - jnp.log1p(jnp.exp(-jnp.abs(x)))


def _sigmoid(x):
    return 1.0 / (1.0 + jnp.exp(-x))


def _silu(x):
    return x * _sigmoid(x)


def _iota2(n, axis):
    return lax.broadcasted_iota(jnp.int32, (n, n), axis)


def _const_spec(shape):
    return pl.BlockSpec(shape, lambda *_: (0,) * len(shape))


def _layer_spec(shape, layer, row_block=0):
    return pl.BlockSpec((None,) + tuple(shape), lambda *_: (layer, row_block, 0))


def _ffn_kernel(n_pre, *refs):
    x_ref = refs[0]
    pre = refs[1:1 + 2 * n_pre]
    g_ref, wgu_ref, wd_ref, o_ref = refs[1 + 2 * n_pre:]
    x = x_ref[...]
    for p in range(n_pre):
        x = x + jnp.dot(pre[2 * p][...], pre[2 * p + 1][...], preferred_element_type=F32)
    h = _rms(x, g_ref[...]).astype(BF16)
    gate = jnp.dot(h, wgu_ref[:, :D_FF], preferred_element_type=F32)
    up = jnp.dot(h, wgu_ref[:, D_FF:], preferred_element_type=F32)
    act = (_silu(gate) * up).astype(BF16)
    o_ref[...] = x + 0.5 * jnp.dot(act, wd_ref[...], preferred_element_type=F32)


def _ffn_call(x, pre, norm_g, wgu_all, wd_all, layer):
    t = x.shape[0]
    row = lambda i: (i, 0)
    in_specs = [pl.BlockSpec((FFN_ROWS, D_MODEL), row)]
    args = [x]
    for o, w_all, j, rb in pre:
        in_specs += [pl.BlockSpec((FFN_ROWS, o.shape[1]), row),
                     _layer_spec((o.shape[1], D_MODEL), j, rb)]
        args += [o, w_all]
    in_specs += [_const_spec((1, D_MODEL)), _layer_spec(wgu_all.shape[1:], layer),
                 _layer_spec(wd_all.shape[1:], layer)]
    args += [norm_g.reshape(1, D_MODEL), wgu_all, wd_all]
    return pl.pallas_call(
        functools.partial(_ffn_kernel, len(pre)),
        grid=(t // FFN_ROWS,),
        in_specs=in_specs,
        out_specs=pl.BlockSpec((FFN_ROWS, D_MODEL), row),
        out_shape=jax.ShapeDtypeStruct((t, D_MODEL), F32),
        compiler_params=pltpu.CompilerParams(
            dimension_semantics=("parallel",), vmem_limit_bytes=VMEM_LIMIT),
        name="ffn",
    )(*args)


def _inproj_even_kernel(tiles_per_seq, x_ref, g_ref, w_ref, qg_ref, kg_ref, convw_ref,
                        qkv_ref, dgate_ref, fq_ref, fk_ref, fv_ref, fgate_ref, small_ref,
                        xx_ref):
    i = pl.program_id(0)
    n = FFN_ROWS
    h = _rms(x_ref[...], g_ref[...]).astype(BF16)
    c = 0

    def proj(width):
        nonlocal c
        y = jnp.dot(h, w_ref[:, c:c + width], preferred_element_type=F32)
        c += width
        return y

    @pl.when(i % tiles_per_seq == 0)
    def _():
        xx_ref[n:n + SUBLANES, :] = jnp.zeros((SUBLANES, 3 * D_DN), F32)

    xx_ref[0:SUBLANES, :] = xx_ref[n:n + SUBLANES, :]
    xx_ref[SUBLANES:SUBLANES + n, :] = proj(3 * D_DN)
    ones_sq = jnp.ones((HEAD_DIM, HEAD_DIM), BF16)

    def conv_tile(tile):
        sl = slice(tile * HEAD_DIM, (tile + 1) * HEAD_DIM)
        xx = xx_ref[:, sl]
        y = xx[SUBLANES:] * convw_ref[CONV_WIDTH - 1:CONV_WIDTH, sl]
        for d in range(1, CONV_WIDTH):
            tap = convw_ref[CONV_WIDTH - 1 - d:CONV_WIDTH - d, sl]
            y = y + pltpu.roll(xx, d, axis=0)[SUBLANES:] * tap
        y = _silu(y)
        if tile < 2 * N_DN_HEADS:
            ss = _dot_exact_rhs(y * y, ones_sq, 2)
            y = y * (lax.rsqrt(ss + EPS) * (SCALE if tile < N_DN_HEADS else 1.0))
        qkv_ref[:, sl] = y

    def fox_norm(val, gain_ref, out_ref):
        for hd in range(N_FOX_HEADS):
            sl = slice(hd * HEAD_DIM, (hd + 1) * HEAD_DIM)
            out_ref[:, sl] = _rms(val[:, sl], gain_ref[...]).astype(BF16)

    dgate_ref[...] = proj(D_DN)
    conv_tile(0)
    conv_tile(1)
    fq = proj(D_FOX)
    conv_tile(2)
    conv_tile(3)
    fk = proj(D_FOX)
    fox_norm(fq, qg_ref, fq_ref)
    conv_tile(4)
    conv_tile(5)
    fv_ref[...] = proj(D_FOX).astype(BF16)
    fox_norm(fk, kg_ref, fk_ref)
    conv_tile(6)
    conv_tile(7)
    fgate_ref[...] = proj(D_FOX)
    conv_tile(8)
    conv_tile(9)
    small_ref[...] = proj(LANES)
    conv_tile(10)
    conv_tile(11)


def _inproj_even_call(x, seq_len, norm_g, w_all, layer, q_g, k_g, conv_w):
    t = x.shape[0]
    row = lambda i: (i, 0)
    widths = [(3 * D_DN, F32), (D_DN, F32), (D_FOX, BF16), (D_FOX, BF16), (D_FOX, BF16),
              (D_FOX, F32), (LANES, F32)]
    return pl.pallas_call(
        functools.partial(_inproj_even_kernel, seq_len // FFN_ROWS),
        grid=(t // FFN_ROWS,),
        in_specs=[pl.BlockSpec((FFN_ROWS, D_MODEL), row), _const_spec((1, D_MODEL)),
                  _layer_spec(w_all.shape[1:], layer), _const_spec((1, HEAD_DIM)),
                  _const_spec((1, HEAD_DIM)), _const_spec((CONV_WIDTH, 3 * D_DN))],
        out_specs=[pl.BlockSpec((FFN_ROWS, n), row) for n, _ in widths],
        out_shape=[jax.ShapeDtypeStruct((t, n), dt) for n, dt in widths],
        scratch_shapes=[pltpu.VMEM((SUBLANES + FFN_ROWS, 3 * D_DN), F32)],
        compiler_params=pltpu.CompilerParams(
            dimension_semantics=("arbitrary",), vmem_limit_bytes=VMEM_LIMIT),
        name="inproj_even",
    )(x, norm_g.reshape(1, D_MODEL), w_all, q_g.reshape(1, HEAD_DIM), k_g.reshape(1, HEAD_DIM),
      conv_w)


def _inproj_odd_kernel(x_ref, g_ref, w_ref, o_ref):
    h = _rms(x_ref[...], g_ref[...]).astype(BF16)
    o_ref[...] = jnp.dot(h, w_ref[...], preferred_element_type=F32).astype(BF16)


def _inproj_odd_call(x, norm_g, w_all, layer):
    t = x.shape[0]
    n = w_all.shape[2]
    row = lambda i: (i, 0)
    return pl.pallas_call(
        _inproj_odd_kernel,
        grid=(t // FFN_ROWS,),
        in_specs=[pl.BlockSpec((FFN_ROWS, D_MODEL), row), _const_spec((1, D_MODEL)),
                  _layer_spec(w_all.shape[1:], layer)],
        out_specs=pl.BlockSpec((FFN_ROWS, n), row),
        out_shape=jax.ShapeDtypeStruct((t, n), BF16),
        compiler_params=pltpu.CompilerParams(
            dimension_semantics=("parallel",), vmem_limit_bytes=VMEM_LIMIT),
        name="inproj_odd",
    )(x, norm_g.reshape(1, D_MODEL), w_all)


def _deltanet_kernel(qkv_ref, gate_ref, small_ref, alog_ref, dtb_ref, ng_ref, o_ref,
                     state_ref, sums_ref):
    i = pl.program_id(1)
    n = DN_ROWS
    g = DN_GROUP
    heads = range(N_DN_HEADS)
    groups = range(n // g)

    row = _iota2(g, 0)
    col = _iota2(g, 1)
    diff = row ^ col
    same = (diff >> (DN_CHUNK.bit_length() - 1)) == 0
    incl = same & (col <= row)

    @pl.when(i == 0)
    def _():
        state_ref[...] = jnp.zeros_like(state_ref)
        sums_ref[0:g, :] = jnp.where(incl, 1.0, 0.0).astype(BF16)
        sums_ref[g:2 * g, :] = jnp.where(same, 1.0, 0.0).astype(BF16)

    sm = small_ref[...]
    beta_all = _sigmoid(sm)
    g_all = -jnp.exp(alog_ref[...]) * _softplus(sm + dtb_ref[...])

    hsl = lambda base_col, hd: slice(base_col + hd * HEAD_DIM, base_col + (hd + 1) * HEAD_DIM)
    rs = lambda gi: slice(gi * g, (gi + 1) * g)

    g_sums = [_dot_exact_lhs(sums_ref[...], g_all[rs(gi)], 3) for gi in groups]
    gc_all = [s[0:g] for s in g_sums]
    gl_all = [s[g:2 * g] for s in g_sums]
    gc_all_t = [c.T for c in gc_all]

    lane = lambda arr, hd: arr[:, LANE_DECAY + hd:LANE_DECAY + hd + 1]
    on_diag = diff == 0
    pair = diff == 1
    levels = []
    b = 2
    while b < DN_CHUNK:
        levels.append(b.bit_length() - 1)
        b *= 2
    n_chunks = g // DN_CHUNK
    q, k, gc, gl, egc, decay, rhs, a, x = {}, {}, {}, {}, {}, {}, {}, {}, {}
    sol, attn, qd, kd, egl = {}, {}, {}, {}, {}
    state = [state_ref[hd] for hd in heads]
    outs = {gi: [[] for _ in heads] for gi in groups}

    def prep(gi, hd):
        c = (gi, hd)
        q[c] = qkv_ref[rs(gi), hsl(0, hd)]
        k[c] = qkv_ref[rs(gi), hsl(D_DN, hd)]
        beta = beta_all[rs(gi), LANE_BETA + hd:LANE_BETA + hd + 1]
        gc[c] = lane(gc_all[gi], hd)
        gl[c] = lane(gl_all[gi], hd)
        egc[c] = jnp.exp(gc[c])
        decay[c] = jnp.where(
            incl, jnp.exp(gc[c] - gc_all_t[gi][LANE_DECAY + hd:LANE_DECAY + hd + 1, :]), 0.0)
        kb = k[c] * beta
        rhs[c] = jnp.concatenate(
            [qkv_ref[rs(gi), hsl(2 * D_DN, hd)] * beta, kb * egc[c]], axis=1)
        a[c] = jnp.where(on_diag, 0.0, _dot_nt(kb, k[c]) * decay[c])
        x[c] = jnp.where(on_diag, 1.0, jnp.where(pair, -a[c], 0.0))

    def invert_level(gi, sh):
        off = (diff >> sh) == 1
        t = {hd: _dot(x[gi, hd], jnp.where(off, a[gi, hd], 0.0)) for hd in heads}
        for hd in heads:
            x[gi, hd] = x[gi, hd] - _dot(t[hd], x[gi, hd])

    def solve(gi):
        for hd in heads:
            c = (gi, hd)
            sol[c] = _dot(x[c], rhs[c])
            attn[c] = _dot_nt(q[c], k[c]) * decay[c]
            qd[c] = q[c] * egc[c]
            kd[c] = k[c] * jnp.exp(gl[c] - gc[c])
            egl[c] = jnp.exp(gl[c])

    def scan_chunk(gi, ch):
        r = slice(ch * DN_CHUNK, (ch + 1) * DN_CHUNK)
        for hd in heads:
            c = (gi, hd)
            v_new = sol[c][r, :HEAD_DIM] - _dot(sol[c][r, HEAD_DIM:], state[hd])
            outs[gi][hd].append(_dot(qd[c][r], state[hd]) + _dot(attn[c][r, r], v_new))
            state[hd] = (state[hd] * egl[c][ch * DN_CHUNK:ch * DN_CHUNK + 1, :]
                         + _dot_tn(kd[c][r], v_new))

    def emit(gi):
        for hd in heads:
            o = jnp.concatenate(outs[gi][hd], axis=0)
            sl = hsl(0, hd)
            o_ref[rs(gi), sl] = (_rms(o, ng_ref[...]) * _silu(gate_ref[rs(gi), sl])).astype(BF16)

    for gi in groups:
        for hd in heads:
            prep(gi, hd)
    for sh in levels:
        for gi in groups:
            invert_level(gi, sh)
    for gi in groups:
        solve(gi)
    for gi in groups:
        for ch in range(n_chunks):
            scan_chunk(gi, ch)
        emit(gi)
    for hd in heads:
        state_ref[hd] = state[hd]


def _deltanet_call(qkv, gate, small, a_log, dt_bias, norm_g):
    b, s, _ = qkv.shape
    lane_vec = lambda v: jnp.zeros((1, LANES), F32).at[0, LANE_DECAY:LANE_DECAY + N_DN_HEADS].set(v)
    blk = lambda w: pl.BlockSpec((None, DN_ROWS, w), lambda bi, i: (bi, i, 0))
    return pl.pallas_call(
        _deltanet_kernel,
        grid=(b, s // DN_ROWS),
        in_specs=[blk(3 * D_DN), blk(D_DN), blk(LANES),
                  _const_spec((1, LANES)), _const_spec((1, LANES)), _const_spec((1, HEAD_DIM))],
        out_specs=blk(D_DN),
        out_shape=jax.ShapeDtypeStruct((b, s, D_DN), BF16),
        scratch_shapes=[pltpu.VMEM((N_DN_HEADS, HEAD_DIM, HEAD_DIM), F32),
                        pltpu.VMEM((2 * DN_GROUP, DN_GROUP), BF16)],
        compiler_params=pltpu.CompilerParams(
            dimension_semantics=("parallel", "arbitrary"), vmem_limit_bytes=VMEM_LIMIT),
        name="deltanet",
    )(qkv, gate, small, lane_vec(a_log), lane_vec(dt_bias), norm_g.reshape(1, HEAD_DIM))


def _fox_kernel(q_ref, k_ref, v_ref, small_ref, fbias_ref, gate_ref, o_ref, crow_ref, vaug_ref,
                kn2_ref):
    i = pl.program_id(1)
    n = FOX_ROWS
    s = k_ref.shape[0]
    row = _iota2(n, 0)
    col = _iota2(n, 1)
    causal = col <= row

    @pl.when(i == 0)
    def _():
        tri = jnp.where(causal, 1.0, 0.0).astype(BF16)
        carry = jnp.zeros((1, LANES), F32)
        for blk in range(s // n):
            r = slice(blk * n, (blk + 1) * n)
            lf = _log_sigmoid(small_ref[r, :] + fbias_ref[...])
            c = _dot_exact_lhs(tri, lf, 3) + carry
            carry = c[n - 1:n, :]
            crow_ref[:, r] = c.T
        for hd in range(N_FOX_HEADS):
            vaug_ref[:, 2 * hd * HEAD_DIM:(2 * hd + 1) * HEAD_DIM] = (
                v_ref[:, hd * HEAD_DIM:(hd + 1) * HEAD_DIM])
            vaug_ref[:, (2 * hd + 1) * HEAD_DIM:(2 * hd + 2) * HEAD_DIM] = (
                jnp.ones((s, HEAD_DIM), BF16))
            kn2 = jnp.float32(0.0)
            for blk in range(s // n):
                kf = k_ref[blk * n:(blk + 1) * n, hd * HEAD_DIM:(hd + 1) * HEAD_DIM].astype(F32)
                kn2 = jnp.maximum(kn2, jnp.max(jnp.sum(kf * kf, axis=-1, keepdims=True)))
            kn2_ref[hd] = kn2

    hsl = lambda hd: slice(hd * HEAD_DIM, (hd + 1) * HEAD_DIM)

    for h0 in range(0, N_FOX_HEADS, FOX_HEADS_PER_LOOP):
        hds = range(h0, h0 + FOX_HEADS_PER_LOOP)

        def step(carry, k0, masked):
            ms = carry[0::2]
            accs = carry[1::2]
            logits = [_dot_nt(q_ref[:, hsl(hd)], k_ref[pl.ds(k0, n), hsl(hd)]) * SCALE
                      - crow_ref[LANE_FORGET + hd:LANE_FORGET + hd + 1, pl.ds(k0, n)]
                      for hd in hds]
            if masked:
                logits = [jnp.where(causal, lg, -1e30) for lg in logits]
            m_new = [jnp.maximum(m, jnp.max(lg, axis=-1, keepdims=True))
                     for m, lg in zip(ms, logits)]
            p = [jnp.exp(lg - mn) for lg, mn in zip(logits, m_new)]
            pv = [_dot(pp, vaug_ref[pl.ds(k0, n), 2 * hd * HEAD_DIM:(2 * hd + 2) * HEAD_DIM])
                  for pp, hd in zip(p, hds)]
            accs = [jnp.exp(m - mn) * acc + x for m, mn, acc, x in zip(ms, m_new, accs, pv)]
            out = []
            for mn, acc in zip(m_new, accs):
                out += [mn, acc]
            return tuple(out)

        init = (jnp.full((n, 1), -1e30, F32),
                jnp.zeros((n, 2 * HEAD_DIM), F32)) * FOX_HEADS_PER_LOOP
        carry = step(init, pl.multiple_of(i * n, n), True)

        safe = jnp.bool_(True)
        for hd in hds:
            qf = q_ref[:, hsl(hd)].astype(F32)
            qn2 = jnp.max(jnp.sum(qf * qf, axis=-1, keepdims=True))
            safe = safe & (4.0 * SCALE * SCALE * qn2 * kn2_ref[hd] < FOX_SAFE_LOGIT_GAP ** 2)

        def fixed_max_loop(c):
            ms = c[0::2]

            def body(j, accs):
                k0 = pl.multiple_of(j * n, n)
                p = [jnp.exp(_dot_nt(q_ref[:, hsl(hd)], k_ref[pl.ds(k0, n), hsl(hd)]) * SCALE
                             - crow_ref[LANE_FORGET + hd:LANE_FORGET + hd + 1, pl.ds(k0, n)] - m)
                     for hd, m in zip(hds, ms)]
                pv = [_dot(pp, vaug_ref[pl.ds(k0, n), 2 * hd * HEAD_DIM:(2 * hd + 2) * HEAD_DIM])
                      for pp, hd in zip(p, hds)]
                return tuple(acc + x for acc, x in zip(accs, pv))

            accs = lax.fori_loop(0, i, body, tuple(c[1::2]))
            out = []
            for m, acc in zip(ms, accs):
                out += [m, acc]
            return tuple(out)

        def online_loop(c):
            return lax.fori_loop(0, i, lambda j, cc: step(cc, pl.multiple_of(j * n, n), False), c)

        carry = lax.cond(safe, fixed_max_loop, online_loop, carry)
        for idx, hd in enumerate(hds):
            sl = slice(hd * HEAD_DIM, (hd + 1) * HEAD_DIM)
            acc = carry[2 * idx + 1]
            o = acc[:, :HEAD_DIM] / acc[:, HEAD_DIM:]
            o_ref[:, sl] = (o * _sigmoid(gate_ref[:, sl])).astype(BF16)


def _fox_call(q, k, v, small, f_bias, gate):
    b, s, _ = q.shape
    blk = lambda w: pl.BlockSpec((None, FOX_ROWS, w), lambda bi, i: (bi, i, 0))
    full = lambda w: pl.BlockSpec((None, s, w), lambda bi, i: (bi, 0, 0))
    fb = jnp.zeros((1, LANES), F32).at[0, LANE_FORGET:LANE_FORGET + N_FOX_HEADS].set(f_bias)
    return pl.pallas_call(
        _fox_kernel,
        grid=(b, s // FOX_ROWS),
        in_specs=[blk(D_FOX), full(D_FOX), full(D_FOX), full(LANES), _const_spec((1, LANES)),
                  blk(D_FOX)],
        out_specs=blk(D_FOX),
        out_shape=jax.ShapeDtypeStruct((b, s, D_FOX), BF16),
        scratch_shapes=[pltpu.VMEM((LANES, s), F32), pltpu.VMEM((s, 2 * D_FOX), BF16),
                        pltpu.SMEM((N_FOX_HEADS,), F32)],
        compiler_params=pltpu.CompilerParams(
            dimension_semantics=("parallel", "arbitrary"), vmem_limit_bytes=VMEM_LIMIT),
        name="fox_attention",
    )(q, k, v, small, fb, gate)


SB_CUTOFF = -105.0


def _sb_kernel(q_ref, k_ref, v_ref, o_ref):
    n = SB_ROWS
    s = k_ref.shape[0]
    row = _iota2(n, 0)
    col = _iota2(n, 1)
    before = col < row
    later = jnp.where(row > col, 1.0, 0.0).astype(BF16)
    later2 = jnp.concatenate([later, later], axis=0)
    total = lambda lg: jnp.broadcast_to(jnp.sum(lg, axis=1, keepdims=True), (lg.shape[0], LANES))

    def block(q, k0, rest, masked):
        z = _dot_nt(q, k_ref[pl.ds(k0, n), :]) * SCALE
        soft = jnp.log(1.0 + jnp.exp(-jnp.abs(z)))
        lsz = jnp.minimum(z, 0.0) - soft
        lsn = lsz - z
        lg = jnp.where(before, lsn, 0.0) if masked else lsn
        hi = lg.astype(BF16)
        lo = (lg - hi.astype(F32)).astype(BF16)
        sums = jnp.dot(jnp.concatenate([hi, lo], axis=1), later2, preferred_element_type=F32)
        tail = sums + jnp.concatenate([rest] * (n // LANES), axis=1)
        a = jnp.exp(lsz + tail)
        if masked:
            a = jnp.where(before, a, 0.0)
        return _dot(a, v_ref[pl.ds(k0, n), :]), rest + total(lg)

    def pairs(qs, k_los, rests, masked):
        rep = lambda r: jnp.concatenate([r] * (n // LANES), axis=1)
        z = [_dot_nt(q, k_ref[pl.ds(k_lo, 2 * n), :]) * SCALE for q, k_lo in zip(qs, k_los)]
        lsz = [jnp.minimum(zz, 0.0) - jnp.log(1.0 + jnp.exp(-jnp.abs(zz))) for zz in z]
        lsn = [l - zz for l, zz in zip(lsz, z)]
        lg = [jnp.concatenate([jnp.where(before, l[:, n:], 0.0) if masked else l[:, n:],
                               l[:, :n]], axis=0) for l in lsn]
        hi = [l.astype(BF16) for l in lg]
        lo = [(l - h.astype(F32)).astype(BF16) for l, h in zip(lg, hi)]
        sums = [jnp.dot(jnp.concatenate([h, l], axis=1), later2, preferred_element_type=F32)
                for h, l in zip(hi, lo)]
        tot = [total(l) for l in lg]
        rest_mid = [r + t[:n] for r, t in zip(rests, tot)]
        tail = [jnp.concatenate([sm[n:] + rep(rm), sm[:n] + rep(r)], axis=1)
                for sm, rm, r in zip(sums, rest_mid, rests)]
        a = [jnp.exp(l + t) for l, t in zip(lsz, tail)]
        if masked:
            a = [jnp.concatenate([aa[:, :n], jnp.where(before, aa[:, n:], 0.0)], axis=1) for aa in a]
        pv = [_dot(aa, v_ref[pl.ds(k_lo, 2 * n), :]) for aa, k_lo in zip(a, k_los)]
        return pv, [rm + t[n:] for rm, t in zip(rest_mid, tot)]

    def pair(q, k_lo, rest, masked):
        pv, rest = pairs([q], [k_lo], [rest], masked)
        return pv[0], rest[0]

    def live(rest):
        return (jnp.max(rest) > SB_CUTOFF).astype(jnp.int32)

    acc0, _ = block(q_ref[0:n, :], 0, jnp.zeros((n, LANES), F32), True)
    o_ref[0:n, :] = acc0.astype(BF16)

    def first(idx):
        q0 = [pl.multiple_of(i * n, n) for i in idx]
        qs = [q_ref[pl.ds(s0, n), :] for s0 in q0]
        accs, rests = pairs(qs, [pl.multiple_of(s0 - n, n) for s0 in q0],
                            [jnp.zeros((n, LANES), F32)] * len(idx), True)
        return list(zip(qs, accs, rests))

    def finish(i, q, acc, rest):
        n_pairs = (i - 1) // 2

        def cond(c):
            return (c[0] < n_pairs) & (c[3] > 0)

        def body(c):
            t, acc, rest, _ = c
            k_lo = pl.multiple_of((i - 3 - 2 * t) * n, n)
            pv, rest = pair(q, k_lo, rest, False)
            return t + 1, acc + pv, rest, live(rest)

        _, acc, rest, alive = lax.while_loop(cond, body, (jnp.int32(0), acc, rest, live(rest)))
        acc = lax.cond(((i - 1) % 2 == 1) & (alive > 0),
                       lambda: acc + block(q, 0, rest, False)[0], lambda: acc)
        o_ref[pl.ds(pl.multiple_of(i * n, n), n), :] = acc.astype(BF16)

    def qblock_group(idx):
        for i, start in zip(idx, first(idx)):
            finish(i, *start)

    def group_step(p, _):
        qblock_group([1 + SB_QBLOCKS_PER_STEP * p + t for t in range(SB_QBLOCKS_PER_STEP)])
        return 0

    nq = s // n
    n_groups = (nq - 1) // SB_QBLOCKS_PER_STEP
    lax.fori_loop(0, n_groups, group_step, 0)
    rest_blocks = [jnp.int32(i) for i in range(1 + n_groups * SB_QBLOCKS_PER_STEP, nq)]
    if rest_blocks:
        qblock_group(rest_blocks)


def _sb_call(qkv):
    b, s, _ = qkv.shape
    nh = N_SB_HEADS
    head = lambda off: pl.BlockSpec((None, s, HEAD_DIM), lambda bi, h: (bi, 0, off + h))
    return pl.pallas_call(
        _sb_kernel,
        grid=(b, nh),
        in_specs=[head(0), head(nh), head(2 * nh)],
        out_specs=head(0),
        out_shape=jax.ShapeDtypeStruct((b, s, nh * HEAD_DIM), BF16),
        compiler_params=pltpu.CompilerParams(
            dimension_semantics=("parallel", "parallel"), vmem_limit_bytes=VMEM_LIMIT),
        name="stickbreaking_attention",
    )(qkv, qkv, qkv)


EVEN_W_ROWS = 256
EVEN_W_COLS = 4 * D_DN + 4 * D_FOX + LANES


def _even_w_in_kernel(wt_ref, o_ref):
    dn_wide = 4 * D_DN
    gates = 2 * N_DN_HEADS
    fox_wide = 4 * D_FOX
    wt = wt_ref[...]
    o_ref[:, :dn_wide] = wt[:dn_wide].T.astype(BF16)
    o_ref[:, dn_wide:dn_wide + fox_wide] = (
        wt[dn_wide + gates:dn_wide + gates + fox_wide].T.astype(BF16))
    small = jnp.concatenate(
        [wt[dn_wide:dn_wide + gates], wt[dn_wide + gates + fox_wide:],
         jnp.zeros((LANES - gates - N_FOX_HEADS, wt.shape[1]), F32)], axis=0)
    o_ref[:, dn_wide + fox_wide:] = small.T.astype(BF16)


def _even_w_in(w):
    layers, rows, cols = w.shape
    return pl.pallas_call(
        _even_w_in_kernel,
        grid=(layers, rows // EVEN_W_ROWS),
        in_specs=[pl.BlockSpec((None, cols, EVEN_W_ROWS), lambda l, r: (l, 0, r))],
        out_specs=pl.BlockSpec((None, EVEN_W_ROWS, EVEN_W_COLS), lambda l, r: (l, r, 0)),
        out_shape=jax.ShapeDtypeStruct((layers, rows, EVEN_W_COLS), BF16),
        compiler_params=pltpu.CompilerParams(
            dimension_semantics=("parallel", "parallel"), vmem_limit_bytes=VMEM_LIMIT),
        name="reorder_w_in_even",
    )(jnp.swapaxes(w, 1, 2))


def kernel(x, norm_ffn1, ffn1_w_gu, ffn1_w_down, norm_mix, w_in_even, dn_conv_w, dn_a_log,
           dn_dt_bias, dn_norm_g, fox_q_norm_g, fox_k_norm_g, fox_f_bias, w_out_even,
           w_in_odd, w_out_odd, norm_ffn2, ffn2_w_gu, ffn2_w_down):
    b, s, d = x.shape
    depth = norm_ffn1.shape[0]
    t = b * s
    bf = lambda a: a.astype(BF16)
    wgu1, wd1, wgu2, wd2 = bf(ffn1_w_gu), bf(ffn1_w_down), bf(ffn2_w_gu), bf(ffn2_w_down)
    w_in_e, w_out_e = _even_w_in(w_in_even), bf(w_out_even)
    w_in_o, w_out_o = bf(w_in_odd), bf(w_out_odd)
    x = x.reshape(t, d)
    for l in range(depth):
        j = l // 2
        x = _ffn_call(x, [], norm_ffn1[l], wgu1, wd1, l)
        if l % 2 == 0:
            qkv, dgate, fq, fk, fv, fgate, small = _inproj_even_call(
                x, s, norm_mix[l], w_in_e, j, fox_q_norm_g[j], fox_k_norm_g[j], dn_conv_w[j])
            r3 = lambda a: a.reshape(b, s, a.shape[-1])
            o_dn = _deltanet_call(r3(qkv), r3(dgate), r3(small), dn_a_log[j], dn_dt_bias[j],
                                  dn_norm_g[j])
            o_fox = _fox_call(r3(fq), r3(fk), r3(fv), r3(small), fox_f_bias[j], r3(fgate))
            pre = [(o_dn.reshape(t, D_DN), w_out_e, j, 0), (o_fox.reshape(t, D_FOX), w_out_e, j, 1)]
        else:
            qkv = _inproj_odd_call(x, norm_mix[l], w_in_o, j)
            o = _sb_call(qkv.reshape(b, s, qkv.shape[-1]))
            pre = [(o.reshape(t, o.shape[-1]), w_out_o, j, 0)]
        x = _ffn_call(x, pre, norm_ffn2[l], wgu2, wd2, l)
    return x.reshape(b, s, d)
```

```python
import functools

import jax
import jax.numpy as jnp
from jax import lax
from jax.experimental import pallas as pl
from jax.experimental.pallas import tpu as pltpu

F32 = jnp.float32
BF16 = jnp.bfloat16

D_MODEL = 1024
HEAD_DIM = 128
N_DN_HEADS = 4
N_FOX_HEADS = 4
N_SB_HEADS = 8
D_DN = N_DN_HEADS * HEAD_DIM
D_FOX = N_FOX_HEADS * HEAD_DIM
CONV_WIDTH = 4
D_FF = 2816
EPS = 1e-6
SCALE = HEAD_DIM ** -0.5
LOG2E = 1.4426950408889634
SCALE_LOG2E = SCALE * LOG2E

LANES = 128
SUBLANES = 8
VMEM_LIMIT = 56 * 1024 * 1024

FFN_ROWS = 512
SB_ROWS = 256
SB_QBLOCKS_PER_STEP = 5
FOX_ROWS = 512
FOX_SAFE_LOGIT_GAP = 60.0
FOX_HEADS_PER_LOOP = 4
DN_ROWS = 512
DN_GROUP = 256
DN_CHUNK = 256

LANE_BETA = 0
LANE_DECAY = N_DN_HEADS
LANE_FORGET = 2 * N_DN_HEADS


def _dot(a, b):
    return jnp.dot(a.astype(BF16), b.astype(BF16), preferred_element_type=F32)


def _dot_nt(a, b):
    return lax.dot_general(a.astype(BF16), b.astype(BF16), (((1,), (1,)), ((), ())),
                           preferred_element_type=F32)


def _dot_tn(a, b):
    return lax.dot_general(a.astype(BF16), b.astype(BF16), (((0,), (0,)), ((), ())),
                           preferred_element_type=F32)


def _split_bf16(x, parts):
    out = []
    r = x
    for _ in range(parts - 1):
        p = r.astype(BF16)
        out.append(p)
        r = r - p.astype(F32)
    out.append(r.astype(BF16))
    return out


def _dot_exact_lhs(m_bf16, x, parts):
    acc = None
    for p in _split_bf16(x, parts):
        t = jnp.dot(m_bf16, p, preferred_element_type=F32)
        acc = t if acc is None else acc + t
    return acc


def _dot_exact_rhs(x, m_bf16, parts):
    acc = None
    for p in _split_bf16(x, parts):
        t = jnp.dot(p, m_bf16, preferred_element_type=F32)
        acc = t if acc is None else acc + t
    return acc


def _rms(xf, g):
    return xf * lax.rsqrt(jnp.mean(xf * xf, axis=-1, keepdims=True) + EPS) * g


def _softplus(x):
    return jnp.maximum(x, 0.0) + jnp.log1p(jnp.exp(-jnp.abs(x)))


def _log_sigmoid(x):
    return jnp.minimum(x, 0.0) - jnp.log1p(jnp.exp(-jnp.abs(x)))


def _sigmoid(x):
    return 1.0 / (1.0 + jnp.exp(-x))


def _silu(x):
    return x * _sigmoid(x)


def _iota2(n, axis):
    return lax.broadcasted_iota(jnp.int32, (n, n), axis)


def _const_spec(shape):
    return pl.BlockSpec(shape, lambda *_: (0,) * len(shape))


def _layer_spec(shape, layer, row_block=0):
    return pl.BlockSpec((None,) + tuple(shape), lambda *_: (layer, row_block, 0))


def _ffn_kernel(n_pre, *refs):
    x_ref = refs[0]
    pre = refs[1:1 + 2 * n_pre]
    g_ref, wgu_ref, wd_ref, o_ref = refs[1 + 2 * n_pre:]
    x = x_ref[...]
    for p in range(n_pre):
        x = x + jnp.dot(pre[2 * p][...], pre[2 * p + 1][...], preferred_element_type=F32)
    h = _rms(x, g_ref[...]).astype(BF16)
    gate = jnp.dot(h, wgu_ref[:, :D_FF], preferred_element_type=F32)
    up = jnp.dot(h, wgu_ref[:, D_FF:], preferred_element_type=F32)
    act = (_silu(gate) * up).astype(BF16)
    o_ref[...] = x + 0.5 * jnp.dot(act, wd_ref[...], preferred_element_type=F32)


def _ffn_call(x, pre, norm_g, wgu_all, wd_all, layer):
    t = x.shape[0]
    row = lambda i: (i, 0)
    in_specs = [pl.BlockSpec((FFN_ROWS, D_MODEL), row)]
    args = [x]
    for o, w_all, j, rb in pre:
        in_specs += [pl.BlockSpec((FFN_ROWS, o.shape[1]), row),
                     _layer_spec((o.shape[1], D_MODEL), j, rb)]
        args += [o, w_all]
    in_specs += [_const_spec((1, D_MODEL)), _layer_spec(wgu_all.shape[1:], layer),
                 _layer_spec(wd_all.shape[1:], layer)]
    args += [norm_g.reshape(1, D_MODEL), wgu_all, wd_all]
    return pl.pallas_call(
        functools.partial(_ffn_kernel, len(pre)),
        grid=(t // FFN_ROWS,),
        in_specs=in_specs,
        out_specs=pl.BlockSpec((FFN_ROWS, D_MODEL), row),
        out_shape=jax.ShapeDtypeStruct((t, D_MODEL), F32),
        compiler_params=pltpu.CompilerParams(
            dimension_semantics=("parallel",), vmem_limit_bytes=VMEM_LIMIT),
        name="ffn",
    )(*args)


def _inproj_even_kernel(tiles_per_seq, x_ref, g_ref, w_ref, qg_ref, kg_ref, convw_ref,
                        qkv_ref, dgate_ref, fq_ref, fk_ref, fv_ref, fgate_ref, small_ref,
                        xx_ref):
    i = pl.program_id(0)
    n = FFN_ROWS
    h = _rms(x_ref[...], g_ref[...]).astype(BF16)
    c = 0

    def proj(width):
        nonlocal c
        y = jnp.dot(h, w_ref[:, c:c + width], preferred_element_type=F32)
        c += width
        return y

    @pl.when(i % tiles_per_seq == 0)
    def _():
        xx_ref[n:n + SUBLANES, :] = jnp.zeros((SUBLANES, 3 * D_DN), F32)

    xx_ref[0:SUBLANES, :] = xx_ref[n:n + SUBLANES, :]
    xx_ref[SUBLANES:SUBLANES + n, :] = proj(3 * D_DN)
    ones_sq = jnp.ones((HEAD_DIM, HEAD_DIM), BF16)

    def conv_tile(tile):
        sl = slice(tile * HEAD_DIM, (tile + 1) * HEAD_DIM)
        xx = xx_ref[:, sl]
        y = xx[SUBLANES:] * convw_ref[CONV_WIDTH - 1:CONV_WIDTH, sl]
        for d in range(1, CONV_WIDTH):
            tap = convw_ref[CONV_WIDTH - 1 - d:CONV_WIDTH - d, sl]
            y = y + pltpu.roll(xx, d, axis=0)[SUBLANES:] * tap
        y = _silu(y)
        if tile < 2 * N_DN_HEADS:
            ss = _dot_exact_rhs(y * y, ones_sq, 2)
            y = y * (lax.rsqrt(ss + EPS) * (SCALE if tile < N_DN_HEADS else 1.0))
        qkv_ref[:, sl] = y

    def fox_norm(val, gain_ref, out_ref):
        for hd in range(N_FOX_HEADS):
            sl = slice(hd * HEAD_DIM, (hd + 1) * HEAD_DIM)
            out_ref[:, sl] = _rms(val[:, sl], gain_ref[...]).astype(BF16)

    dgate_ref[...] = proj(D_DN)
    conv_tile(0)
    conv_tile(1)
    fq = proj(D_FOX)
    conv_tile(2)
    conv_tile(3)
    fk = proj(D_FOX)
    fox_norm(fq, qg_ref, fq_ref)
    conv_tile(4)
    conv_tile(5)
    fv_ref[...] = proj(D_FOX).astype(BF16)
    fox_norm(fk, kg_ref, fk_ref)
    conv_tile(6)
    conv_tile(7)
    fgate_ref[...] = proj(D_FOX)
    conv_tile(8)
    conv_tile(9)
    small_ref[...] = proj(LANES)
    conv_tile(10)
    conv_tile(11)


def _inproj_even_call(x, seq_len, norm_g, w_all, layer, q_g, k_g, conv_w):
    t = x.shape[0]
    row = lambda i: (i, 0)
    widths = [(3 * D_DN, F32), (D_DN, F32), (D_FOX, BF16), (D_FOX, BF16), (D_FOX, BF16),
              (D_FOX, F32), (LANES, F32)]
    return pl.pallas_call(
        functools.partial(_inproj_even_kernel, seq_len // FFN_ROWS),
        grid=(t // FFN_ROWS,),
        in_specs=[pl.BlockSpec((FFN_ROWS, D_MODEL), row), _const_spec((1, D_MODEL)),
                  _layer_spec(w_all.shape[1:], layer), _const_spec((1, HEAD_DIM)),
                  _const_spec((1, HEAD_DIM)), _const_spec((CONV_WIDTH, 3 * D_DN))],
        out_specs=[pl.BlockSpec((FFN_ROWS, n), row) for n, _ in widths],
        out_shape=[jax.ShapeDtypeStruct((t, n), dt) for n, dt in widths],
        scratch_shapes=[pltpu.VMEM((SUBLANES + FFN_ROWS, 3 * D_DN), F32)],
        compiler_params=pltpu.CompilerParams(
            dimension_semantics=("arbitrary",), vmem_limit_bytes=VMEM_LIMIT),
        name="inproj_even",
    )(x, norm_g.reshape(1, D_MODEL), w_all, q_g.reshape(1, HEAD_DIM), k_g.reshape(1, HEAD_DIM),
      conv_w)


def _inproj_odd_kernel(x_ref, g_ref, w_ref, o_ref):
    h = _rms(x_ref[...], g_ref[...]).astype(BF16)
    o_ref[...] = jnp.dot(h, w_ref[...], preferred_element_type=F32).astype(BF16)


def _inproj_odd_call(x, norm_g, w_all, layer):
    t = x.shape[0]
    n = w_all.shape[2]
    row = lambda i: (i, 0)
    return pl.pallas_call(
        _inproj_odd_kernel,
        grid=(t // FFN_ROWS,),
        in_specs=[pl.BlockSpec((FFN_ROWS, D_MODEL), row), _const_spec((1, D_MODEL)),
                  _layer_spec(w_all.shape[1:], layer)],
        out_specs=pl.BlockSpec((FFN_ROWS, n), row),
        out_shape=jax.ShapeDtypeStruct((t, n), BF16),
        compiler_params=pltpu.CompilerParams(
            dimension_semantics=("parallel",), vmem_limit_bytes=VMEM_LIMIT),
        name="inproj_odd",
    )(x, norm_g.reshape(1, D_MODEL), w_all)


def _deltanet_kernel(qkv_ref, gate_ref, small_ref, alog_ref, dtb_ref, ng_ref, o_ref,
                     state_ref, sums_ref):
    i = pl.program_id(1)
    n = DN_ROWS
    g = DN_GROUP
    heads = range(N_DN_HEADS)
    groups = range(n // g)

    row = _iota2(g, 0)
    col = _iota2(g, 1)
    diff = row ^ col
    same = (diff >> (DN_CHUNK.bit_length() - 1)) == 0
    incl = same & (col <= row)

    @pl.when(i == 0)
    def _():
        state_ref[...] = jnp.zeros_like(state_ref)
        sums_ref[0:g, :] = jnp.where(incl, 1.0, 0.0).astype(BF16)
        sums_ref[g:2 * g, :] = jnp.where(same, 1.0, 0.0).astype(BF16)

    sm = small_ref[...]
    beta_all = _sigmoid(sm)
    g_all = -jnp.exp(alog_ref[...]) * _softplus(sm + dtb_ref[...])

    hsl = lambda base_col, hd: slice(base_col + hd * HEAD_DIM, base_col + (hd + 1) * HEAD_DIM)
    rs = lambda gi: slice(gi * g, (gi + 1) * g)

    g_sums = [_dot_exact_lhs(sums_ref[...], g_all[rs(gi)], 3) for gi in groups]
    gc_all = [s[0:g] for s in g_sums]
    gl_all = [s[g:2 * g] for s in g_sums]
    gc_all_t = [c.T for c in gc_all]

    lane = lambda arr, hd: arr[:, LANE_DECAY + hd:LANE_DECAY + hd + 1]
    on_diag = diff == 0
    pair = diff == 1
    levels = []
    b = 2
    while b < DN_CHUNK:
        levels.append(b.bit_length() - 1)
        b *= 2
    n_chunks = g // DN_CHUNK
    q, k, gc, gl, egc, decay, rhs, a, x = {}, {}, {}, {}, {}, {}, {}, {}, {}
    sol, attn, qd, kd, egl = {}, {}, {}, {}, {}
    state = [state_ref[hd] for hd in heads]
    outs = {gi: [[] for _ in heads] for gi in groups}

    def prep(gi, hd):
        c = (gi, hd)
        q[c] = qkv_ref[rs(gi), hsl(0, hd)]
        k[c] = qkv_ref[rs(gi), hsl(D_DN, hd)]
        beta = beta_all[rs(gi), LANE_BETA + hd:LANE_BETA + hd + 1]
        gc[c] = lane(gc_all[gi], hd)
        gl[c] = lane(gl_all[gi], hd)
        egc[c] = jnp.exp(gc[c])
        decay[c] = jnp.where(
            incl, jnp.exp(gc[c] - gc_all_t[gi][LANE_DECAY + hd:LANE_DECAY + hd + 1, :]), 0.0)
        kb = k[c] * beta
        rhs[c] = jnp.concatenate(
            [qkv_ref[rs(gi), hsl(2 * D_DN, hd)] * beta, kb * egc[c]], axis=1)
        a[c] = jnp.where(on_diag, 0.0, _dot_nt(kb, k[c]) * decay[c])
        x[c] = jnp.where(on_diag, 1.0, jnp.where(pair, -a[c], 0.0))

    def invert_level(gi, sh):
        off = (diff >> sh) == 1
        t = {hd: _dot(x[gi, hd], jnp.where(off, a[gi, hd], 0.0)) for hd in heads}
        for hd in heads:
            x[gi, hd] = x[gi, hd] - _dot(t[hd], x[gi, hd])

    def solve(gi):
        for hd in heads:
            c = (gi, hd)
            sol[c] = _dot(x[c], rhs[c])
            attn[c] = _dot_nt(q[c], k[c]) * decay[c]
            qd[c] = q[c] * egc[c]
            kd[c] = k[c] * jnp.exp(gl[c] - gc[c])
            egl[c] = jnp.exp(gl[c])

    def scan_chunk(gi, ch):
        r = slice(ch * DN_CHUNK, (ch + 1) * DN_CHUNK)
        for hd in heads:
            c = (gi, hd)
            v_new = sol[c][r, :HEAD_DIM] - _dot(sol[c][r, HEAD_DIM:], state[hd])
            outs[gi][hd].append(_dot(qd[c][r], state[hd]) + _dot(attn[c][r, r], v_new))
            state[hd] = (state[hd] * egl[c][ch * DN_CHUNK:ch * DN_CHUNK + 1, :]
                         + _dot_tn(kd[c][r], v_new))

    def emit(gi):
        for hd in heads:
            o = jnp.concatenate(outs[gi][hd], axis=0)
            sl = hsl(0, hd)
            o_ref[rs(gi), sl] = (_rms(o, ng_ref[...]) * _silu(gate_ref[rs(gi), sl])).astype(BF16)

    for gi in groups:
        for hd in heads:
            prep(gi, hd)
    for sh in levels:
        for gi in groups:
            invert_level(gi, sh)
    for gi in groups:
        solve(gi)
    for gi in groups:
        for ch in range(n_chunks):
            scan_chunk(gi, ch)
        emit(gi)
    for hd in heads:
        state_ref[hd] = state[hd]


def _deltanet_call(qkv, gate, small, a_log, dt_bias, norm_g):
    b, s, _ = qkv.shape
    lane_vec = lambda v: jnp.zeros((1, LANES), F32).at[0, LANE_DECAY:LANE_DECAY + N_DN_HEADS].set(v)
    blk = lambda w: pl.BlockSpec((None, DN_ROWS, w), lambda bi, i: (bi, i, 0))
    return pl.pallas_call(
        _deltanet_kernel,
        grid=(b, s // DN_ROWS),
        in_specs=[blk(3 * D_DN), blk(D_DN), blk(LANES),
                  _const_spec((1, LANES)), _const_spec((1, LANES)), _const_spec((1, HEAD_DIM))],
        out_specs=blk(D_DN),
        out_shape=jax.ShapeDtypeStruct((b, s, D_DN), BF16),
        scratch_shapes=[pltpu.VMEM((N_DN_HEADS, HEAD_DIM, HEAD_DIM), F32),
                        pltpu.VMEM((2 * DN_GROUP, DN_GROUP), BF16)],
        compiler_params=pltpu.CompilerParams(
            dimension_semantics=("parallel", "arbitrary"), vmem_limit_bytes=VMEM_LIMIT),
        name="deltanet",
    )(qkv, gate, small, lane_vec(a_log), lane_vec(dt_bias), norm_g.reshape(1, HEAD_DIM))


def _fox_kernel(q_ref, k_ref, v_ref, small_ref, fbias_ref, gate_ref, o_ref, crow_ref, vaug_ref,
                kn2_ref):
    i = pl.program_id(1)
    n = FOX_ROWS
    s = k_ref.shape[0]
    row = _iota2(n, 0)
    col = _iota2(n, 1)
    causal = col <= row

    @pl.when(i == 0)
    def _():
        tri = jnp.where(causal, 1.0, 0.0).astype(BF16)
        carry = jnp.zeros((1, LANES), F32)
        for blk in range(s // n):
            r = slice(blk * n, (blk + 1) * n)
            lf = _log_sigmoid(small_ref[r, :] + fbias_ref[...])
            c = _dot_exact_lhs(tri, lf, 3) + carry
            carry = c[n - 1:n, :]
            crow_ref[:, r] = (c * LOG2E).T
        for hd in range(N_FOX_HEADS):
            vaug_ref[:, 2 * hd * HEAD_DIM:(2 * hd + 1) * HEAD_DIM] = (
                v_ref[:, hd * HEAD_DIM:(hd + 1) * HEAD_DIM])
            vaug_ref[:, (2 * hd + 1) * HEAD_DIM:(2 * hd + 2) * HEAD_DIM] = (
                jnp.ones((s, HEAD_DIM), BF16))
            kn2 = jnp.float32(0.0)
            for blk in range(s // n):
                kf = k_ref[blk * n:(blk + 1) * n, hd * HEAD_DIM:(hd + 1) * HEAD_DIM].astype(F32)
                kn2 = jnp.maximum(kn2, jnp.max(jnp.sum(kf * kf, axis=-1, keepdims=True)))
            kn2_ref[hd] = kn2

    hsl = lambda hd: slice(hd * HEAD_DIM, (hd + 1) * HEAD_DIM)

    for h0 in range(0, N_FOX_HEADS, FOX_HEADS_PER_LOOP):
        hds = range(h0, h0 + FOX_HEADS_PER_LOOP)

        def step(carry, k0, masked):
            ms = carry[0::2]
            accs = carry[1::2]
            logits = [_dot_nt(q_ref[:, hsl(hd)], k_ref[pl.ds(k0, n), hsl(hd)]) * SCALE_LOG2E
                      - crow_ref[LANE_FORGET + hd:LANE_FORGET + hd + 1, pl.ds(k0, n)]
                      for hd in hds]
            if masked:
                logits = [jnp.where(causal, lg, -1e30) for lg in logits]
            m_new = [jnp.maximum(m, jnp.max(lg, axis=-1, keepdims=True))
                     for m, lg in zip(ms, logits)]
            p = [jnp.exp2(lg - mn) for lg, mn in zip(logits, m_new)]
            pv = [_dot(pp, vaug_ref[pl.ds(k0, n), 2 * hd * HEAD_DIM:(2 * hd + 2) * HEAD_DIM])
                  for pp, hd in zip(p, hds)]
            accs = [jnp.exp2(m - mn) * acc + x for m, mn, acc, x in zip(ms, m_new, accs, pv)]
            out = []
            for mn, acc in zip(m_new, accs):
                out += [mn, acc]
            return tuple(out)

        init = (jnp.full((n, 1), -1e30, F32),
                jnp.zeros((n, 2 * HEAD_DIM), F32)) * FOX_HEADS_PER_LOOP
        carry = step(init, pl.multiple_of(i * n, n), True)

        safe = jnp.bool_(True)
        for hd in hds:
            qf = q_ref[:, hsl(hd)].astype(F32)
            qn2 = jnp.max(jnp.sum(qf * qf, axis=-1, keepdims=True))
            safe = safe & (4.0 * SCALE * SCALE * qn2 * kn2_ref[hd] < FOX_SAFE_LOGIT_GAP ** 2)

        def fixed_max_loop(c):
            ms = c[0::2]

            def body(j, accs):
                k0 = pl.multiple_of(j * n, n)
                p = [jnp.exp2(_dot_nt(q_ref[:, hsl(hd)], k_ref[pl.ds(k0, n), hsl(hd)]) * SCALE_LOG2E
                              - crow_ref[LANE_FORGET + hd:LANE_FORGET + hd + 1, pl.ds(k0, n)] - m)
                     for hd, m in zip(hds, ms)]
                pv = [_dot(pp, vaug_ref[pl.ds(k0, n), 2 * hd * HEAD_DIM:(2 * hd + 2) * HEAD_DIM])
                      for pp, hd in zip(p, hds)]
                return tuple(acc + x for acc, x in zip(accs, pv))

            accs = lax.fori_loop(0, i, body, tuple(c[1::2]))
            out = []
            for m, acc in zip(ms, accs):
                out += [m, acc]
            return tuple(out)

        def online_loop(c):
            return lax.fori_loop(0, i, lambda j, cc: step(cc, pl.multiple_of(j * n, n), False), c)

        carry = lax.cond(safe, fixed_max_loop, online_loop, carry)
        for idx, hd in enumerate(hds):
            sl = slice(hd * HEAD_DIM, (hd + 1) * HEAD_DIM)
            acc = carry[2 * idx + 1]
            o = acc[:, :HEAD_DIM] / acc[:, HEAD_DIM:]
            o_ref[:, sl] = (o * _sigmoid(gate_ref[:, sl])).astype(BF16)


def _fox_call(q, k, v, small, f_bias, gate):
    b, s, _ = q.shape
    blk = lambda w: pl.BlockSpec((None, FOX_ROWS, w), lambda bi, i: (bi, i, 0))
    full = lambda w: pl.BlockSpec((None, s, w), lambda bi, i: (bi, 0, 0))
    fb = jnp.zeros((1, LANES), F32).at[0, LANE_FORGET:LANE_FORGET + N_FOX_HEADS].set(f_bias)
    return pl.pallas_call(
        _fox_kernel,
        grid=(b, s // FOX_ROWS),
        in_specs=[blk(D_FOX), full(D_FOX), full(D_FOX), full(LANES), _const_spec((1, LANES)),
                  blk(D_FOX)],
        out_specs=blk(D_FOX),
        out_shape=jax.ShapeDtypeStruct((b, s, D_FOX), BF16),
        scratch_shapes=[pltpu.VMEM((LANES, s), F32), pltpu.VMEM((s, 2 * D_FOX), BF16),
                        pltpu.SMEM((N_FOX_HEADS,), F32)],
        compiler_params=pltpu.CompilerParams(
            dimension_semantics=("parallel", "arbitrary"), vmem_limit_bytes=VMEM_LIMIT),
        name="fox_attention",
    )(q, k, v, small, fb, gate)


SB_CUTOFF = -105.0 * LOG2E


def _sb_kernel(q_ref, k_ref, v_ref, o_ref):
    n = SB_ROWS
    s = k_ref.shape[0]
    row = _iota2(n, 0)
    col = _iota2(n, 1)
    before = col < row
    later = jnp.where(row > col, 1.0, 0.0).astype(BF16)
    later2 = jnp.concatenate([later, later], axis=0)
    total = lambda lg: jnp.broadcast_to(jnp.sum(lg, axis=1, keepdims=True), (lg.shape[0], LANES))

    def block(q, k0, rest, masked):
        z = _dot_nt(q, k_ref[pl.ds(k0, n), :]) * SCALE_LOG2E
        soft = jnp.log2(1.0 + jnp.exp2(-jnp.abs(z)))
        lsz = jnp.minimum(z, 0.0) - soft
        lsn = lsz - z
        lg = jnp.where(before, lsn, 0.0) if masked else lsn
        hi = lg.astype(BF16)
        lo = (lg - hi.astype(F32)).astype(BF16)
        sums = jnp.dot(jnp.concatenate([hi, lo], axis=1), later2, preferred_element_type=F32)
        tail = sums + jnp.concatenate([rest] * (n // LANES), axis=1)
        a = jnp.exp2(lsz + tail)
        if masked:
            a = jnp.where(before, a, 0.0)
        return _dot(a, v_ref[pl.ds(k0, n), :]), rest + total(lg)

    def pairs(qs, k_los, rests, masked):
        rep = lambda r: jnp.concatenate([r] * (n // LANES), axis=1)
        z = [_dot_nt(q, k_ref[pl.ds(k_lo, 2 * n), :]) * SCALE_LOG2E for q, k_lo in zip(qs, k_los)]
        lsz = [jnp.minimum(zz, 0.0) - jnp.log2(1.0 + jnp.exp2(-jnp.abs(zz))) for zz in z]
        lsn = [l - zz for l, zz in zip(lsz, z)]
        lg = [jnp.concatenate([jnp.where(before, l[:, n:], 0.0) if masked else l[:, n:],
                               l[:, :n]], axis=0) for l in lsn]
        hi = [l.astype(BF16) for l in lg]
        lo = [(l - h.astype(F32)).astype(BF16) for l, h in zip(lg, hi)]
        sums = [jnp.dot(jnp.concatenate([h, l], axis=1), later2, preferred_element_type=F32)
                for h, l in zip(hi, lo)]
        tot = [total(l) for l in lg]
        rest_mid = [r + t[:n] for r, t in zip(rests, tot)]
        tail = [jnp.concatenate([sm[n:] + rep(rm), sm[:n] + rep(r)], axis=1)
                for sm, rm, r in zip(sums, rest_mid, rests)]
        a = [jnp.exp2(l + t) for l, t in zip(lsz, tail)]
        if masked:
            a = [jnp.concatenate([aa[:, :n], jnp.where(before, aa[:, n:], 0.0)], axis=1) for aa in a]
        pv = [_dot(aa, v_ref[pl.ds(k_lo, 2 * n), :]) for aa, k_lo in zip(a, k_los)]
        return pv, [rm + t[n:] for rm, t in zip(rest_mid, tot)]

    def pair(q, k_lo, rest, masked):
        pv, rest = pairs([q], [k_lo], [rest], masked)
        return pv[0], rest[0]

    def live(rest):
        return (jnp.max(rest) > SB_CUTOFF).astype(jnp.int32)

    acc0, _ = block(q_ref[0:n, :], 0, jnp.zeros((n, LANES), F32), True)
    o_ref[0:n, :] = acc0.astype(BF16)

    def first(idx):
        q0 = [pl.multiple_of(i * n, n) for i in idx]
        qs = [q_ref[pl.ds(s0, n), :] for s0 in q0]
        accs, rests = pairs(qs, [pl.multiple_of(s0 - n, n) for s0 in q0],
                            [jnp.zeros((n, LANES), F32)] * len(idx), True)
        return list(zip(qs, accs, rests))

    def finish(i, q, acc, rest):
        n_pairs = (i - 1) // 2

        def cond(c):
            return (c[0] < n_pairs) & (c[3] > 0)

        def body(c):
            t, acc, rest, _ = c
            k_lo = pl.multiple_of((i - 3 - 2 * t) * n, n)
            pv, rest = pair(q, k_lo, rest, False)
            return t + 1, acc + pv, rest, live(rest)

        _, acc, rest, alive = lax.while_loop(cond, body, (jnp.int32(0), acc, rest, live(rest)))
        acc = lax.cond(((i - 1) % 2 == 1) & (alive > 0),
                       lambda: acc + block(q, 0, rest, False)[0], lambda: acc)
        o_ref[pl.ds(pl.multiple_of(i * n, n), n), :] = acc.astype(BF16)

    def qblock_group(idx):
        for i, start in zip(idx, first(idx)):
            finish(i, *start)

    def group_step(p, _):
        qblock_group([1 + SB_QBLOCKS_PER_STEP * p + t for t in range(SB_QBLOCKS_PER_STEP)])
        return 0

    nq = s // n
    n_groups = (nq - 1) // SB_QBLOCKS_PER_STEP
    lax.fori_loop(0, n_groups, group_step, 0)
    rest_blocks = [jnp.int32(i) for i in range(1 + n_groups * SB_QBLOCKS_PER_STEP, nq)]
    if rest_blocks:
        qblock_group(rest_blocks)


def _sb_call(qkv):
    b, s, _ = qkv.shape
    nh = N_SB_HEADS
    head = lambda off: pl.BlockSpec((None, s, HEAD_DIM), lambda bi, h: (bi, 0, off + h))
    return pl.pallas_call(
        _sb_kernel,
        grid=(b, nh),
        in_specs=[head(0), head(nh), head(2 * nh)],
        out_specs=head(0),
        out_shape=jax.ShapeDtypeStruct((b, s, nh * HEAD_DIM), BF16),
        compiler_params=pltpu.CompilerParams(
            dimension_semantics=("parallel", "parallel"), vmem_limit_bytes=VMEM_LIMIT),
        name="stickbreaking_attention",
    )(qkv, qkv, qkv)


EVEN_W_ROWS = 256
EVEN_W_COLS = 4 * D_DN + 4 * D_FOX + LANES


def _even_w_in_kernel(wt_ref, o_ref):
    dn_wide = 4 * D_DN
    gates = 2 * N_DN_HEADS
    fox_wide = 4 * D_FOX
    wt = wt_ref[...]
    o_ref[:, :dn_wide] = wt[:dn_wide].T.astype(BF16)
    o_ref[:, dn_wide:dn_wide + fox_wide] = (
        wt[dn_wide + gates:dn_wide + gates + fox_wide].T.astype(BF16))
    small = jnp.concatenate(
        [wt[dn_wide:dn_wide + gates], wt[dn_wide + gates + fox_wide:],
         jnp.zeros((LANES - gates - N_FOX_HEADS, wt.shape[1]), F32)], axis=0)
    o_ref[:, dn_wide + fox_wide:] = small.T.astype(BF16)


def _even_w_in(w):
    layers, rows, cols = w.shape
    return pl.pallas_call(
        _even_w_in_kernel,
        grid=(layers, rows // EVEN_W_ROWS),
        in_specs=[pl.BlockSpec((None, cols, EVEN_W_ROWS), lambda l, r: (l, 0, r))],
        out_specs=pl.BlockSpec((None, EVEN_W_ROWS, EVEN_W_COLS), lambda l, r: (l, r, 0)),
        out_shape=jax.ShapeDtypeStruct((layers, rows, EVEN_W_COLS), BF16),
        compiler_params=pltpu.CompilerParams(
            dimension_semantics=("parallel", "parallel"), vmem_limit_bytes=VMEM_LIMIT),
        name="reorder_w_in_even",
    )(jnp.swapaxes(w, 1, 2))


def kernel(x, norm_ffn1, ffn1_w_gu, ffn1_w_down, norm_mix, w_in_even, dn_conv_w, dn_a_log,
           dn_dt_bias, dn_norm_g, fox_q_norm_g, fox_k_norm_g, fox_f_bias, w_out_even,
           w_in_odd, w_out_odd, norm_ffn2, ffn2_w_gu, ffn2_w_down):
    b, s, d = x.shape
    depth = norm_ffn1.shape[0]
    t = b * s
    bf = lambda a: a.astype(BF16)
    wgu1, wd1, wgu2, wd2 = bf(ffn1_w_gu), bf(ffn1_w_down), bf(ffn2_w_gu), bf(ffn2_w_down)
    w_in_e, w_out_e = _even_w_in(w_in_even), bf(w_out_even)
    w_in_o, w_out_o = bf(w_in_odd), bf(w_out_odd)
    x = x.reshape(t, d)
    for l in range(depth):
        j = l // 2
        x = _ffn_call(x, [], norm_ffn1[l], wgu1, wd1, l)
        if l % 2 == 0:
            qkv, dgate, fq, fk, fv, fgate, small = _inproj_even_call(
                x, s, norm_mix[l], w_in_e, j, fox_q_norm_g[j], fox_k_norm_g[j], dn_conv_w[j])
            r3 = lambda a: a.reshape(b, s, a.shape[-1])
            o_dn = _deltanet_call(r3(qkv), r3(dgate), r3(small), dn_a_log[j], dn_dt_bias[j],
                                  dn_norm_g[j])
            o_fox = _fox_call(r3(fq), r3(fk), r3(fv), r3(small), fox_f_bias[j], r3(fgate))
            pre = [(o_dn.reshape(t, D_DN), w_out_e, j, 0), (o_fox.reshape(t, D_FOX), w_out_e, j, 1)]
        else:
            qkv = _inproj_odd_call(x, norm_mix[l], w_in_o, j)
            o = _sb_call(qkv.reshape(b, s, qkv.shape[-1]))
            pre = [(o.reshape(t, o.shape[-1]), w_out_o, j, 0)]
        x = _ffn_call(x, pre, norm_ffn2[l], wgu2, wd2, l)
    return x.reshape(b, s, d)
```

```python
import functools

import jax
import jax.numpy as jnp
from jax import lax
from jax.experimental import pallas as pl
from jax.experimental.pallas import tpu as pltpu

F32 = jnp.float32
BF16 = jnp.bfloat16

D_MODEL = 1024
HEAD_DIM = 128
N_DN_HEADS = 4
N_FOX_HEADS = 4
N_SB_HEADS = 8
D_DN = N_DN_HEADS * HEAD_DIM
D_FOX = N_FOX_HEADS * HEAD_DIM
CONV_WIDTH = 4
D_FF = 2816
EPS = 1e-6
SCALE = HEAD_DIM ** -0.5
LOG2E = 1.4426950408889634
SCALE_LOG2E = SCALE * LOG2E

LANES = 128
SUBLANES = 8
VMEM_LIMIT = 56 * 1024 * 1024

FFN_ROWS = 512
SB_ROWS = 256
SB_QBLOCKS_PER_STEP = 15
FOX_ROWS = 512
FOX_SAFE_LOGIT_GAP = 60.0
FOX_HEADS_PER_LOOP = 4
DN_ROWS = 512
DN_GROUP = 256
DN_CHUNK = 256

LANE_BETA = 0
LANE_DECAY = N_DN_HEADS
LANE_FORGET = 2 * N_DN_HEADS


def _dot(a, b):
    return jnp.dot(a.astype(BF16), b.astype(BF16), preferred_element_type=F32)


def _dot_nt(a, b):
    return lax.dot_general(a.astype(BF16), b.astype(BF16), (((1,), (1,)), ((), ())),
                           preferred_element_type=F32)


def _dot_tn(a, b):
    return lax.dot_general(a.astype(BF16), b.astype(BF16), (((0,), (0,)), ((), ())),
                           preferred_element_type=F32)


def _split_bf16(x, parts):
    out = []
    r = x
    for _ in range(parts - 1):
        p = r.astype(BF16)
        out.append(p)
        r = r - p.astype(F32)
    out.append(r.astype(BF16))
    return out


def _dot_exact_lhs(m_bf16, x, parts):
    acc = None
    for p in _split_bf16(x, parts):
        t = jnp.dot(m_bf16, p, preferred_element_type=F32)
        acc = t if acc is None else acc + t
    return acc


def _dot_exact_rhs(x, m_bf16, parts):
    acc = None
    for p in _split_bf16(x, parts):
        t = jnp.dot(p, m_bf16, preferred_element_type=F32)
        acc = t if acc is None else acc + t
    return acc


def _rms(xf, g):
    return xf * lax.rsqrt(jnp.mean(xf * xf, axis=-1, keepdims=True) + EPS) * g


def _softplus(x):
    return jnp.maximum(x, 0.0) + jnp.log1p(jnp.exp(-jnp.abs(x)))


def _log_sigmoid(x):
    return jnp.minimum(x, 0.0) - jnp.log1p(jnp.exp(-jnp.abs(x)))


def _sigmoid(x):
    return 1.0 / (1.0 + jnp.exp(-x))


def _silu(x):
    return x * _sigmoid(x)


def _iota2(n, axis):
    return lax.broadcasted_iota(jnp.int32, (n, n), axis)


def _const_spec(shape):
    return pl.BlockSpec(shape, lambda *_: (0,) * len(shape))


def _layer_spec(shape, layer, row_block=0):
    return pl.BlockSpec((None,) + tuple(shape), lambda *_: (layer, row_block, 0))


def _ffn_kernel(n_pre, *refs):
    x_ref = refs[0]
    pre = refs[1:1 + 2 * n_pre]
    g_ref, wgu_ref, wd_ref, o_ref = refs[1 + 2 * n_pre:]
    x = x_ref[...]
    for p in range(n_pre):
        x = x + jnp.dot(pre[2 * p][...], pre[2 * p + 1][...], preferred_element_type=F32)
    h = _rms(x, g_ref[...]).astype(BF16)
    gate = jnp.dot(h, wgu_ref[:, :D_FF], preferred_element_type=F32)
    up = jnp.dot(h, wgu_ref[:, D_FF:], preferred_element_type=F32)
    act = (_silu(gate) * up).astype(BF16)
    o_ref[...] = x + 0.5 * jnp.dot(act, wd_ref[...], preferred_element_type=F32)


def _ffn_call(x, pre, norm_g, wgu_all, wd_all, layer):
    t = x.shape[0]
    row = lambda i: (i, 0)
    in_specs = [pl.BlockSpec((FFN_ROWS, D_MODEL), row)]
    args = [x]
    for o, w_all, j, rb in pre:
        in_specs += [pl.BlockSpec((FFN_ROWS, o.shape[1]), row),
                     _layer_spec((o.shape[1], D_MODEL), j, rb)]
        args += [o, w_all]
    in_specs += [_const_spec((1, D_MODEL)), _layer_spec(wgu_all.shape[1:], layer),
                 _layer_spec(wd_all.shape[1:], layer)]
    args += [norm_g.reshape(1, D_MODEL), wgu_all, wd_all]
    return pl.pallas_call(
        functools.partial(_ffn_kernel, len(pre)),
        grid=(t // FFN_ROWS,),
        in_specs=in_specs,
        out_specs=pl.BlockSpec((FFN_ROWS, D_MODEL), row),
        out_shape=jax.ShapeDtypeStruct((t, D_MODEL), F32),
        compiler_params=pltpu.CompilerParams(
            dimension_semantics=("parallel",), vmem_limit_bytes=VMEM_LIMIT),
        name="ffn",
    )(*args)


def _inproj_even_kernel(tiles_per_seq, x_ref, g_ref, w_ref, qg_ref, kg_ref, convw_ref,
                        qkv_ref, dgate_ref, fq_ref, fk_ref, fv_ref, fgate_ref, small_ref,
                        xx_ref):
    i = pl.program_id(0)
    n = FFN_ROWS
    h = _rms(x_ref[...], g_ref[...]).astype(BF16)
    c = 0

    def proj(width):
        nonlocal c
        y = jnp.dot(h, w_ref[:, c:c + width], preferred_element_type=F32)
        c += width
        return y

    @pl.when(i % tiles_per_seq == 0)
    def _():
        xx_ref[n:n + SUBLANES, :] = jnp.zeros((SUBLANES, 3 * D_DN), F32)

    xx_ref[0:SUBLANES, :] = xx_ref[n:n + SUBLANES, :]
    xx_ref[SUBLANES:SUBLANES + n, :] = proj(3 * D_DN)
    ones_sq = jnp.ones((HEAD_DIM, HEAD_DIM), BF16)

    def conv_tile(tile):
        sl = slice(tile * HEAD_DIM, (tile + 1) * HEAD_DIM)
        xx = xx_ref[:, sl]
        y = xx[SUBLANES:] * convw_ref[CONV_WIDTH - 1:CONV_WIDTH, sl]
        for d in range(1, CONV_WIDTH):
            tap = convw_ref[CONV_WIDTH - 1 - d:CONV_WIDTH - d, sl]
            y = y + pltpu.roll(xx, d, axis=0)[SUBLANES:] * tap
        y = _silu(y)
        if tile < 2 * N_DN_HEADS:
            ss = _dot_exact_rhs(y * y, ones_sq, 2)
            y = y * (lax.rsqrt(ss + EPS) * (SCALE if tile < N_DN_HEADS else 1.0))
        qkv_ref[:, sl] = y

    def fox_norm(val, gain_ref, out_ref):
        for hd in range(N_FOX_HEADS):
            sl = slice(hd * HEAD_DIM, (hd + 1) * HEAD_DIM)
            out_ref[:, sl] = _rms(val[:, sl], gain_ref[...]).astype(BF16)

    dgate_ref[...] = proj(D_DN)
    conv_tile(0)
    conv_tile(1)
    fq = proj(D_FOX)
    conv_tile(2)
    conv_tile(3)
    fk = proj(D_FOX)
    fox_norm(fq, qg_ref, fq_ref)
    conv_tile(4)
    conv_tile(5)
    fv_ref[...] = proj(D_FOX).astype(BF16)
    fox_norm(fk, kg_ref, fk_ref)
    conv_tile(6)
    conv_tile(7)
    fgate_ref[...] = proj(D_FOX)
    conv_tile(8)
    conv_tile(9)
    small_ref[...] = proj(LANES)
    conv_tile(10)
    conv_tile(11)


def _inproj_even_call(x, seq_len, norm_g, w_all, layer, q_g, k_g, conv_w):
    t = x.shape[0]
    row = lambda i: (i, 0)
    widths = [(3 * D_DN, F32), (D_DN, F32), (D_FOX, BF16), (D_FOX, BF16), (D_FOX, BF16),
              (D_FOX, F32), (LANES, F32)]
    return pl.pallas_call(
        functools.partial(_inproj_even_kernel, seq_len // FFN_ROWS),
        grid=(t // FFN_ROWS,),
        in_specs=[pl.BlockSpec((FFN_ROWS, D_MODEL), row), _const_spec((1, D_MODEL)),
                  _layer_spec(w_all.shape[1:], layer), _const_spec((1, HEAD_DIM)),
                  _const_spec((1, HEAD_DIM)), _const_spec((CONV_WIDTH, 3 * D_DN))],
        out_specs=[pl.BlockSpec((FFN_ROWS, n), row) for n, _ in widths],
        out_shape=[jax.ShapeDtypeStruct((t, n), dt) for n, dt in widths],
        scratch_shapes=[pltpu.VMEM((SUBLANES + FFN_ROWS, 3 * D_DN), F32)],
        compiler_params=pltpu.CompilerParams(
            dimension_semantics=("arbitrary",), vmem_limit_bytes=VMEM_LIMIT),
        name="inproj_even",
    )(x, norm_g.reshape(1, D_MODEL), w_all, q_g.reshape(1, HEAD_DIM), k_g.reshape(1, HEAD_DIM),
      conv_w)


def _inproj_odd_kernel(x_ref, g_ref, w_ref, o_ref):
    h = _rms(x_ref[...], g_ref[...]).astype(BF16)
    o_ref[...] = jnp.dot(h, w_ref[...], preferred_element_type=F32).astype(BF16)


def _inproj_odd_call(x, norm_g, w_all, layer):
    t = x.shape[0]
    n = w_all.shape[2]
    row = lambda i: (i, 0)
    return pl.pallas_call(
        _inproj_odd_kernel,
        grid=(t // FFN_ROWS,),
        in_specs=[pl.BlockSpec((FFN_ROWS, D_MODEL), row), _const_spec((1, D_MODEL)),
                  _layer_spec(w_all.shape[1:], layer)],
        out_specs=pl.BlockSpec((FFN_ROWS, n), row),
        out_shape=jax.ShapeDtypeStruct((t, n), BF16),
        compiler_params=pltpu.CompilerParams(
            dimension_semantics=("parallel",), vmem_limit_bytes=VMEM_LIMIT),
        name="inproj_odd",
    )(x, norm_g.reshape(1, D_MODEL), w_all)


def _deltanet_kernel(qkv_ref, gate_ref, small_ref, alog_ref, dtb_ref, ng_ref, o_ref,
                     state_ref, sums_ref):
    i = pl.program_id(1)
    n = DN_ROWS
    g = DN_GROUP
    heads = range(N_DN_HEADS)
    groups = range(n // g)

    row = _iota2(g, 0)
    col = _iota2(g, 1)
    diff = row ^ col
    same = (diff >> (DN_CHUNK.bit_length() - 1)) == 0
    incl = same & (col <= row)

    @pl.when(i == 0)
    def _():
        state_ref[...] = jnp.zeros_like(state_ref)
        sums_ref[0:g, :] = jnp.where(incl, 1.0, 0.0).astype(BF16)
        sums_ref[g:2 * g, :] = jnp.where(same, 1.0, 0.0).astype(BF16)

    sm = small_ref[...]
    beta_all = _sigmoid(sm)
    g_all = -jnp.exp(alog_ref[...]) * _softplus(sm + dtb_ref[...])

    hsl = lambda base_col, hd: slice(base_col + hd * HEAD_DIM, base_col + (hd + 1) * HEAD_DIM)
    rs = lambda gi: slice(gi * g, (gi + 1) * g)

    g_sums = [_dot_exact_lhs(sums_ref[...], g_all[rs(gi)], 3) for gi in groups]
    gc_all = [s[0:g] for s in g_sums]
    gl_all = [s[g:2 * g] for s in g_sums]
    gc_all_t = [c.T for c in gc_all]

    lane = lambda arr, hd: arr[:, LANE_DECAY + hd:LANE_DECAY + hd + 1]
    on_diag = diff == 0
    pair = diff == 1
    levels = []
    b = 2
    while b < DN_CHUNK:
        levels.append(b.bit_length() - 1)
        b *= 2
    n_chunks = g // DN_CHUNK
    q, k, gc, gl, egc, decay, rhs, a, x = {}, {}, {}, {}, {}, {}, {}, {}, {}
    sol, attn, qd, kd, egl = {}, {}, {}, {}, {}
    state = [state_ref[hd] for hd in heads]
    outs = {gi: [[] for _ in heads] for gi in groups}

    def prep(gi, hd):
        c = (gi, hd)
        q[c] = qkv_ref[rs(gi), hsl(0, hd)]
        k[c] = qkv_ref[rs(gi), hsl(D_DN, hd)]
        beta = beta_all[rs(gi), LANE_BETA + hd:LANE_BETA + hd + 1]
        gc[c] = lane(gc_all[gi], hd)
        gl[c] = lane(gl_all[gi], hd)
        egc[c] = jnp.exp(gc[c])
        decay[c] = jnp.where(
            incl, jnp.exp(gc[c] - gc_all_t[gi][LANE_DECAY + hd:LANE_DECAY + hd + 1, :]), 0.0)
        kb = k[c] * beta
        rhs[c] = jnp.concatenate(
            [qkv_ref[rs(gi), hsl(2 * D_DN, hd)] * beta, kb * egc[c]], axis=1)
        a[c] = jnp.where(on_diag, 0.0, _dot_nt(kb, k[c]) * decay[c])
        x[c] = jnp.where(on_diag, 1.0, jnp.where(pair, -a[c], 0.0))

    def invert_level(gi, sh):
        off = (diff >> sh) == 1
        t = {hd: _dot(x[gi, hd], jnp.where(off, a[gi, hd], 0.0)) for hd in heads}
        for hd in heads:
            x[gi, hd] = x[gi, hd] - _dot(t[hd], x[gi, hd])

    def solve(gi):
        for hd in heads:
            c = (gi, hd)
            sol[c] = _dot(x[c], rhs[c])
            attn[c] = _dot_nt(q[c], k[c]) * decay[c]
            qd[c] = q[c] * egc[c]
            kd[c] = k[c] * jnp.exp(gl[c] - gc[c])
            egl[c] = jnp.exp(gl[c])

    def scan_chunk(gi, ch):
        r = slice(ch * DN_CHUNK, (ch + 1) * DN_CHUNK)
        for hd in heads:
            c = (gi, hd)
            v_new = sol[c][r, :HEAD_DIM] - _dot(sol[c][r, HEAD_DIM:], state[hd])
            outs[gi][hd].append(_dot(qd[c][r], state[hd]) + _dot(attn[c][r, r], v_new))
            state[hd] = (state[hd] * egl[c][ch * DN_CHUNK:ch * DN_CHUNK + 1, :]
                         + _dot_tn(kd[c][r], v_new))

    def emit(gi):
        for hd in heads:
            o = jnp.concatenate(outs[gi][hd], axis=0)
            sl = hsl(0, hd)
            o_ref[rs(gi), sl] = (_rms(o, ng_ref[...]) * _silu(gate_ref[rs(gi), sl])).astype(BF16)

    for gi in groups:
        for hd in heads:
            prep(gi, hd)
    for sh in levels:
        for gi in groups:
            invert_level(gi, sh)
    for gi in groups:
        solve(gi)
    for gi in groups:
        for ch in range(n_chunks):
            scan_chunk(gi, ch)
        emit(gi)
    for hd in heads:
        state_ref[hd] = state[hd]


def _deltanet_call(qkv, gate, small, a_log, dt_bias, norm_g):
    b, s, _ = qkv.shape
    lane_vec = lambda v: jnp.zeros((1, LANES), F32).at[0, LANE_DECAY:LANE_DECAY + N_DN_HEADS].set(v)
    blk = lambda w: pl.BlockSpec((None, DN_ROWS, w), lambda bi, i: (bi, i, 0))
    return pl.pallas_call(
        _deltanet_kernel,
        grid=(b, s // DN_ROWS),
        in_specs=[blk(3 * D_DN), blk(D_DN), blk(LANES),
                  _const_spec((1, LANES)), _const_spec((1, LANES)), _const_spec((1, HEAD_DIM))],
        out_specs=blk(D_DN),
        out_shape=jax.ShapeDtypeStruct((b, s, D_DN), BF16),
        scratch_shapes=[pltpu.VMEM((N_DN_HEADS, HEAD_DIM, HEAD_DIM), F32),
                        pltpu.VMEM((2 * DN_GROUP, DN_GROUP), BF16)],
        compiler_params=pltpu.CompilerParams(
            dimension_semantics=("parallel", "arbitrary"), vmem_limit_bytes=VMEM_LIMIT),
        name="deltanet",
    )(qkv, gate, small, lane_vec(a_log), lane_vec(dt_bias), norm_g.reshape(1, HEAD_DIM))


def _fox_kernel(q_ref, k_ref, v_ref, small_ref, fbias_ref, gate_ref, o_ref, crow_ref, vaug_ref,
                kn2_ref):
    i = pl.program_id(1)
    n = FOX_ROWS
    s = k_ref.shape[0]
    row = _iota2(n, 0)
    col = _iota2(n, 1)
    causal = col <= row

    @pl.when(i == 0)
    def _():
        tri = jnp.where(causal, 1.0, 0.0).astype(BF16)
        carry = jnp.zeros((1, LANES), F32)
        for blk in range(s // n):
            r = slice(blk * n, (blk + 1) * n)
            lf = _log_sigmoid(small_ref[r, :] + fbias_ref[...])
            c = _dot_exact_lhs(tri, lf, 3) + carry
            carry = c[n - 1:n, :]
            crow_ref[:, r] = (c * LOG2E).T
        for hd in range(N_FOX_HEADS):
            vaug_ref[:, 2 * hd * HEAD_DIM:(2 * hd + 1) * HEAD_DIM] = (
                v_ref[:, hd * HEAD_DIM:(hd + 1) * HEAD_DIM])
            vaug_ref[:, (2 * hd + 1) * HEAD_DIM:(2 * hd + 2) * HEAD_DIM] = (
                jnp.ones((s, HEAD_DIM), BF16))
            kn2 = jnp.float32(0.0)
            for blk in range(s // n):
                kf = k_ref[blk * n:(blk + 1) * n, hd * HEAD_DIM:(hd + 1) * HEAD_DIM].astype(F32)
                kn2 = jnp.maximum(kn2, jnp.max(jnp.sum(kf * kf, axis=-1, keepdims=True)))
            kn2_ref[hd] = kn2

    hsl = lambda hd: slice(hd * HEAD_DIM, (hd + 1) * HEAD_DIM)

    for h0 in range(0, N_FOX_HEADS, FOX_HEADS_PER_LOOP):
        hds = range(h0, h0 + FOX_HEADS_PER_LOOP)

        def step(carry, k0, masked):
            ms = carry[0::2]
            accs = carry[1::2]
            logits = [_dot_nt(q_ref[:, hsl(hd)], k_ref[pl.ds(k0, n), hsl(hd)]) * SCALE_LOG2E
                      - crow_ref[LANE_FORGET + hd:LANE_FORGET + hd + 1, pl.ds(k0, n)]
                      for hd in hds]
            if masked:
                logits = [jnp.where(causal, lg, -1e30) for lg in logits]
            m_new = [jnp.maximum(m, jnp.max(lg, axis=-1, keepdims=True))
                     for m, lg in zip(ms, logits)]
            p = [jnp.exp2(lg - mn) for lg, mn in zip(logits, m_new)]
            pv = [_dot(pp, vaug_ref[pl.ds(k0, n), 2 * hd * HEAD_DIM:(2 * hd + 2) * HEAD_DIM])
                  for pp, hd in zip(p, hds)]
            accs = [jnp.exp2(m - mn) * acc + x for m, mn, acc, x in zip(ms, m_new, accs, pv)]
            out = []
            for mn, acc in zip(m_new, accs):
                out += [mn, acc]
            return tuple(out)

        init = (jnp.full((n, 1), -1e30, F32),
                jnp.zeros((n, 2 * HEAD_DIM), F32)) * FOX_HEADS_PER_LOOP
        carry = step(init, pl.multiple_of(i * n, n), True)

        safe = jnp.bool_(True)
        for hd in hds:
            qf = q_ref[:, hsl(hd)].astype(F32)
            qn2 = jnp.max(jnp.sum(qf * qf, axis=-1, keepdims=True))
            safe = safe & (4.0 * SCALE * SCALE * qn2 * kn2_ref[hd] < FOX_SAFE_LOGIT_GAP ** 2)

        def fixed_max_loop(c):
            ms = c[0::2]

            def body(j, accs):
                k0 = pl.multiple_of(j * n, n)
                p = [jnp.exp2(_dot_nt(q_ref[:, hsl(hd)], k_ref[pl.ds(k0, n), hsl(hd)]) * SCALE_LOG2E
                              - crow_ref[LANE_FORGET + hd:LANE_FORGET + hd + 1, pl.ds(k0, n)] - m)
                     for hd, m in zip(hds, ms)]
                pv = [_dot(pp, vaug_ref[pl.ds(k0, n), 2 * hd * HEAD_DIM:(2 * hd + 2) * HEAD_DIM])
                      for pp, hd in zip(p, hds)]
                return tuple(acc + x for acc, x in zip(accs, pv))

            accs = lax.fori_loop(0, i, body, tuple(c[1::2]))
            out = []
            for m, acc in zip(ms, accs):
                out += [m, acc]
            return tuple(out)

        def online_loop(c):
            return lax.fori_loop(0, i, lambda j, cc: step(cc, pl.multiple_of(j * n, n), False), c)

        carry = lax.cond(safe, fixed_max_loop, online_loop, carry)
        for idx, hd in enumerate(hds):
            sl = slice(hd * HEAD_DIM, (hd + 1) * HEAD_DIM)
            acc = carry[2 * idx + 1]
            o = acc[:, :HEAD_DIM] / acc[:, HEAD_DIM:]
            o_ref[:, sl] = (o * _sigmoid(gate_ref[:, sl])).astype(BF16)


def _fox_call(q, k, v, small, f_bias, gate):
    b, s, _ = q.shape
    blk = lambda w: pl.BlockSpec((None, FOX_ROWS, w), lambda bi, i: (bi, i, 0))
    full = lambda w: pl.BlockSpec((None, s, w), lambda bi, i: (bi, 0, 0))
    fb = jnp.zeros((1, LANES), F32).at[0, LANE_FORGET:LANE_FORGET + N_FOX_HEADS].set(f_bias)
    return pl.pallas_call(
        _fox_kernel,
        grid=(b, s // FOX_ROWS),
        in_specs=[blk(D_FOX), full(D_FOX), full(D_FOX), full(LANES), _const_spec((1, LANES)),
                  blk(D_FOX)],
        out_specs=blk(D_FOX),
        out_shape=jax.ShapeDtypeStruct((b, s, D_FOX), BF16),
        scratch_shapes=[pltpu.VMEM((LANES, s), F32), pltpu.VMEM((s, 2 * D_FOX), BF16),
                        pltpu.SMEM((N_FOX_HEADS,), F32)],
        compiler_params=pltpu.CompilerParams(
            dimension_semantics=("parallel", "arbitrary"), vmem_limit_bytes=VMEM_LIMIT),
        name="fox_attention",
    )(q, k, v, small, fb, gate)


SB_CUTOFF = -105.0 * LOG2E


def _sb_kernel(q_ref, k_ref, v_ref, o_ref):
    n = SB_ROWS
    s = k_ref.shape[0]
    row = _iota2(n, 0)
    col = _iota2(n, 1)
    before = col < row
    later = jnp.where(row > col, 1.0, 0.0).astype(BF16)
    later2 = jnp.concatenate([later, later], axis=0)
    total = lambda lg: jnp.broadcast_to(jnp.sum(lg, axis=1, keepdims=True), (lg.shape[0], LANES))

    def block(q, k0, rest, masked):
        z = _dot_nt(q, k_ref[pl.ds(k0, n), :]) * SCALE_LOG2E
        soft = jnp.log2(1.0 + jnp.exp2(-jnp.abs(z)))
        lsz = jnp.minimum(z, 0.0) - soft
        lsn = lsz - z
        lg = jnp.where(before, lsn, 0.0) if masked else lsn
        hi = lg.astype(BF16)
        lo = (lg - hi.astype(F32)).astype(BF16)
        sums = jnp.dot(jnp.concatenate([hi, lo], axis=1), later2, preferred_element_type=F32)
        tail = sums + jnp.concatenate([rest] * (n // LANES), axis=1)
        a = jnp.exp2(lsz + tail)
        if masked:
            a = jnp.where(before, a, 0.0)
        return _dot(a, v_ref[pl.ds(k0, n), :]), rest + total(lg)

    def pairs(qs, k_los, rests, masked):
        rep = lambda r: jnp.concatenate([r] * (n // LANES), axis=1)
        z = [_dot_nt(q, k_ref[pl.ds(k_lo, 2 * n), :]) * SCALE_LOG2E for q, k_lo in zip(qs, k_los)]
        lsz = [jnp.minimum(zz, 0.0) - jnp.log2(1.0 + jnp.exp2(-jnp.abs(zz))) for zz in z]
        lsn = [l - zz for l, zz in zip(lsz, z)]
        lg = [jnp.concatenate([jnp.where(before, l[:, n:], 0.0) if masked else l[:, n:],
                               l[:, :n]], axis=0) for l in lsn]
        hi = [l.astype(BF16) for l in lg]
        lo = [(l - h.astype(F32)).astype(BF16) for l, h in zip(lg, hi)]
        sums = [jnp.dot(jnp.concatenate([h, l], axis=1), later2, preferred_element_type=F32)
                for h, l in zip(hi, lo)]
        tot = [total(l) for l in lg]
        rest_mid = [r + t[:n] for r, t in zip(rests, tot)]
        tail = [jnp.concatenate([sm[n:] + rep(rm), sm[:n] + rep(r)], axis=1)
                for sm, rm, r in zip(sums, rest_mid, rests)]
        a = [jnp.exp2(l + t) for l, t in zip(lsz, tail)]
        if masked:
            a = [jnp.concatenate([aa[:, :n], jnp.where(before, aa[:, n:], 0.0)], axis=1) for aa in a]
        pv = [_dot(aa, v_ref[pl.ds(k_lo, 2 * n), :]) for aa, k_lo in zip(a, k_los)]
        return pv, [rm + t[n:] for rm, t in zip(rest_mid, tot)]

    def pair(q, k_lo, rest, masked):
        pv, rest = pairs([q], [k_lo], [rest], masked)
        return pv[0], rest[0]

    def live(rest):
        return (jnp.max(rest) > SB_CUTOFF).astype(jnp.int32)

    acc0, _ = block(q_ref[0:n, :], 0, jnp.zeros((n, LANES), F32), True)
    o_ref[0:n, :] = acc0.astype(BF16)

    def first(idx):
        q0 = [pl.multiple_of(i * n, n) for i in idx]
        qs = [q_ref[pl.ds(s0, n), :] for s0 in q0]
        accs, rests = pairs(qs, [pl.multiple_of(s0 - n, n) for s0 in q0],
                            [jnp.zeros((n, LANES), F32)] * len(idx), True)
        return list(zip(qs, accs, rests, [live(r) for r in rests]))

    def finish(i, q, acc, rest, alive0):
        n_pairs = (i - 1) // 2

        def cond(c):
            return (c[0] < n_pairs) & (c[3] > 0)

        def body(c):
            t, acc, rest, _ = c
            k_lo = pl.multiple_of((i - 3 - 2 * t) * n, n)
            pv, rest = pair(q, k_lo, rest, False)
            return t + 1, acc + pv, rest, live(rest)

        _, acc, rest, alive = lax.while_loop(cond, body, (jnp.int32(0), acc, rest, alive0))
        acc = lax.cond(((i - 1) % 2 == 1) & (alive > 0),
                       lambda: acc + block(q, 0, rest, False)[0], lambda: acc)
        o_ref[pl.ds(pl.multiple_of(i * n, n), n), :] = acc.astype(BF16)

    def qblock_group(idx):
        for i, start in zip(idx, first(idx)):
            finish(i, *start)

    def group_step(p, _):
        qblock_group([1 + SB_QBLOCKS_PER_STEP * p + t for t in range(SB_QBLOCKS_PER_STEP)])
        return 0

    nq = s // n
    n_groups = (nq - 1) // SB_QBLOCKS_PER_STEP
    lax.fori_loop(0, n_groups, group_step, 0)
    rest_blocks = [jnp.int32(i) for i in range(1 + n_groups * SB_QBLOCKS_PER_STEP, nq)]
    if rest_blocks:
        qblock_group(rest_blocks)


def _sb_call(qkv):
    b, s, _ = qkv.shape
    nh = N_SB_HEADS
    head = lambda off: pl.BlockSpec((None, s, HEAD_DIM), lambda bi, h: (bi, 0, off + h))
    return pl.pallas_call(
        _sb_kernel,
        grid=(b, nh),
        in_specs=[head(0), head(nh), head(2 * nh)],
        out_specs=head(0),
        out_shape=jax.ShapeDtypeStruct((b, s, nh * HEAD_DIM), BF16),
        compiler_params=pltpu.CompilerParams(
            dimension_semantics=("parallel", "parallel"), vmem_limit_bytes=VMEM_LIMIT),
        name="stickbreaking_attention",
    )(qkv, qkv, qkv)


EVEN_W_ROWS = 256
EVEN_W_COLS = 4 * D_DN + 4 * D_FOX + LANES


def _even_w_in_kernel(wt_ref, o_ref):
    dn_wide = 4 * D_DN
    gates = 2 * N_DN_HEADS
    fox_wide = 4 * D_FOX
    wt = wt_ref[...]
    o_ref[:, :dn_wide] = wt[:dn_wide].T.astype(BF16)
    o_ref[:, dn_wide:dn_wide + fox_wide] = (
        wt[dn_wide + gates:dn_wide + gates + fox_wide].T.astype(BF16))
    small = jnp.concatenate(
        [wt[dn_wide:dn_wide + gates], wt[dn_wide + gates + fox_wide:],
         jnp.zeros((LANES - gates - N_FOX_HEADS, wt.shape[1]), F32)], axis=0)
    o_ref[:, dn_wide + fox_wide:] = small.T.astype(BF16)


def _even_w_in(w):
    layers, rows, cols = w.shape
    return pl.pallas_call(
        _even_w_in_kernel,
        grid=(layers, rows // EVEN_W_ROWS),
        in_specs=[pl.BlockSpec((None, cols, EVEN_W_ROWS), lambda l, r: (l, 0, r))],
        out_specs=pl.BlockSpec((None, EVEN_W_ROWS, EVEN_W_COLS), lambda l, r: (l, r, 0)),
        out_shape=jax.ShapeDtypeStruct((layers, rows, EVEN_W_COLS), BF16),
        compiler_params=pltpu.CompilerParams(
            dimension_semantics=("parallel", "parallel"), vmem_limit_bytes=VMEM_LIMIT),
        name="reorder_w_in_even",
    )(jnp.swapaxes(w, 1, 2))


def kernel(x, norm_ffn1, ffn1_w_gu, ffn1_w_down, norm_mix, w_in_even, dn_conv_w, dn_a_log,
           dn_dt_bias, dn_norm_g, fox_q_norm_g, fox_k_norm_g, fox_f_bias, w_out_even,
           w_in_odd, w_out_odd, norm_ffn2, ffn2_w_gu, ffn2_w_down):
    b, s, d = x.shape
    depth = norm_ffn1.shape[0]
    t = b * s
    bf = lambda a: a.astype(BF16)
    wgu1, wd1, wgu2, wd2 = bf(ffn1_w_gu), bf(ffn1_w_down), bf(ffn2_w_gu), bf(ffn2_w_down)
    w_in_e, w_out_e = _even_w_in(w_in_even), bf(w_out_even)
    w_in_o, w_out_o = bf(w_in_odd), bf(w_out_odd)
    x = x.reshape(t, d)
    for l in range(depth):
        j = l // 2
        x = _ffn_call(x, [], norm_ffn1[l], wgu1, wd1, l)
        if l % 2 == 0:
            qkv, dgate, fq, fk, fv, fgate, small = _inproj_even_call(
                x, s, norm_mix[l], w_in_e, j, fox_q_norm_g[j], fox_k_norm_g[j], dn_conv_w[j])
            r3 = lambda a: a.reshape(b, s, a.shape[-1])
            o_dn = _deltanet_call(r3(qkv), r3(dgate), r3(small), dn_a_log[j], dn_dt_bias[j],
                                  dn_norm_g[j])
            o_fox = _fox_call(r3(fq), r3(fk), r3(fv), r3(small), fox_f_bias[j], r3(fgate))
            pre = [(o_dn.reshape(t, D_DN), w_out_e, j, 0), (o_fox.reshape(t, D_FOX), w_out_e, j, 1)]
        else:
            qkv = _inproj_odd_call(x, norm_mix[l], w_in_o, j)
            o = _sb_call(qkv.reshape(b, s, qkv.shape[-1]))
            pre = [(o.reshape(t, o.shape[-1]), w_out_o, j, 0)]
        x = _ffn_call(x, pre, norm_ffn2[l], wgu2, wd2, l)
    return x.reshape(b, s, d)
```

```python
import functools

import jax
import jax.numpy as jnp
from jax import lax
from jax.experimental import pallas as pl
from jax.experimental.pallas import tpu as pltpu

F32 = jnp.float32
BF16 = jnp.bfloat16

D_MODEL = 1024
HEAD_DIM = 128
N_DN_HEADS = 4
N_FOX_HEADS = 4
N_SB_HEADS = 8
D_DN = N_DN_HEADS * HEAD_DIM
D_FOX = N_FOX_HEADS * HEAD_DIM
CONV_WIDTH = 4
D_FF = 2816
EPS = 1e-6
SCALE = HEAD_DIM ** -0.5
LOG2E = 1.4426950408889634
SCALE_LOG2E = SCALE * LOG2E

LANES = 128
SUBLANES = 8
VMEM_LIMIT = 56 * 1024 * 1024

FFN_ROWS = 512
SB_ROWS = 256
SB_QBLOCKS_PER_STEP = 15
FOX_ROWS = 512
FOX_SAFE_LOGIT_GAP = 60.0
FOX_HEADS_PER_LOOP = 4
DN_ROWS = 512
DN_GROUP = 256
DN_CHUNK = 256

LANE_BETA = 0
LANE_DECAY = N_DN_HEADS
LANE_FORGET = 2 * N_DN_HEADS


def _dot(a, b):
    return jnp.dot(a.astype(BF16), b.astype(BF16), preferred_element_type=F32)


def _dot_nt(a, b):
    return lax.dot_general(a.astype(BF16), b.astype(BF16), (((1,), (1,)), ((), ())),
                           preferred_element_type=F32)


def _dot_tn(a, b):
    return lax.dot_general(a.astype(BF16), b.astype(BF16), (((0,), (0,)), ((), ())),
                           preferred_element_type=F32)


def _split_bf16(x, parts):
    out = []
    r = x
    for _ in range(parts - 1):
        p = r.astype(BF16)
        out.append(p)
        r = r - p.astype(F32)
    out.append(r.astype(BF16))
    return out


def _dot_exact_lhs(m_bf16, x, parts):
    acc = None
    for p in _split_bf16(x, parts):
        t = jnp.dot(m_bf16, p, preferred_element_type=F32)
        acc = t if acc is None else acc + t
    return acc


def _dot_exact_rhs(x, m_bf16, parts):
    acc = None
    for p in _split_bf16(x, parts):
        t = jnp.dot(p, m_bf16, preferred_element_type=F32)
        acc = t if acc is None else acc + t
    return acc


def _rms(xf, g):
    return xf * lax.rsqrt(jnp.mean(xf * xf, axis=-1, keepdims=True) + EPS) * g


def _softplus(x):
    return jnp.maximum(x, 0.0) + jnp.log1p(jnp.exp(-jnp.abs(x)))


def _log_sigmoid(x):
    return jnp.minimum(x, 0.0) - jnp.log1p(jnp.exp(-jnp.abs(x)))


def _sigmoid(x):
    return 1.0 / (1.0 + jnp.exp(-x))


def _silu(x):
    return x * _sigmoid(x)


def _iota2(n, axis):
    return lax.broadcasted_iota(jnp.int32, (n, n), axis)


def _const_spec(shape):
    return pl.BlockSpec(shape, lambda *_: (0,) * len(shape))


def _layer_spec(shape, layer, row_block=0):
    return pl.BlockSpec((None,) + tuple(shape), lambda *_: (layer, row_block, 0))


def _ffn_kernel(n_pre, *refs):
    x_ref = refs[0]
    pre = refs[1:1 + 2 * n_pre]
    g_ref, wgu_ref, wd_ref, o_ref = refs[1 + 2 * n_pre:]
    x = x_ref[...]
    for p in range(n_pre):
        x = x + jnp.dot(pre[2 * p][...], pre[2 * p + 1][...], preferred_element_type=F32)
    h = _rms(x, g_ref[...]).astype(BF16)
    gate = jnp.dot(h, wgu_ref[:, :D_FF], preferred_element_type=F32)
    up = jnp.dot(h, wgu_ref[:, D_FF:], preferred_element_type=F32)
    act = (_silu(gate) * up).astype(BF16)
    o_ref[...] = x + 0.5 * jnp.dot(act, wd_ref[...], preferred_element_type=F32)


def _ffn_call(x, pre, norm_g, wgu_all, wd_all, layer):
    t = x.shape[0]
    row = lambda i: (i, 0)
    in_specs = [pl.BlockSpec((FFN_ROWS, D_MODEL), row)]
    args = [x]
    for o, w_all, j, rb in pre:
        in_specs += [pl.BlockSpec((FFN_ROWS, o.shape[1]), row),
                     _layer_spec((o.shape[1], D_MODEL), j, rb)]
        args += [o, w_all]
    in_specs += [_const_spec((1, D_MODEL)), _layer_spec(wgu_all.shape[1:], layer),
                 _layer_spec(wd_all.shape[1:], layer)]
    args += [norm_g.reshape(1, D_MODEL), wgu_all, wd_all]
    return pl.pallas_call(
        functools.partial(_ffn_kernel, len(pre)),
        grid=(t // FFN_ROWS,),
        in_specs=in_specs,
        out_specs=pl.BlockSpec((FFN_ROWS, D_MODEL), row),
        out_shape=jax.ShapeDtypeStruct((t, D_MODEL), F32),
        compiler_params=pltpu.CompilerParams(
            dimension_semantics=("parallel",), vmem_limit_bytes=VMEM_LIMIT),
        name="ffn",
    )(*args)


def _inproj_even_kernel(tiles_per_seq, x_ref, g_ref, w_ref, qg_ref, kg_ref, convw_ref,
                        qkv_ref, dgate_ref, fq_ref, fk_ref, fv_ref, fgate_ref, small_ref,
                        xx_ref):
    i = pl.program_id(0)
    n = FFN_ROWS
    h = _rms(x_ref[...], g_ref[...]).astype(BF16)
    c = 0

    def proj(width):
        nonlocal c
        y = jnp.dot(h, w_ref[:, c:c + width], preferred_element_type=F32)
        c += width
        return y

    @pl.when(i % tiles_per_seq == 0)
    def _():
        xx_ref[n:n + SUBLANES, :] = jnp.zeros((SUBLANES, 3 * D_DN), F32)

    xx_ref[0:SUBLANES, :] = xx_ref[n:n + SUBLANES, :]
    xx_ref[SUBLANES:SUBLANES + n, :] = proj(3 * D_DN)
    ones_sq = jnp.ones((HEAD_DIM, HEAD_DIM), BF16)

    def conv_tile(tile):
        sl = slice(tile * HEAD_DIM, (tile + 1) * HEAD_DIM)
        xx = xx_ref[:, sl]
        y = xx[SUBLANES:] * convw_ref[CONV_WIDTH - 1:CONV_WIDTH, sl]
        for d in range(1, CONV_WIDTH):
            tap = convw_ref[CONV_WIDTH - 1 - d:CONV_WIDTH - d, sl]
            y = y + pltpu.roll(xx, d, axis=0)[SUBLANES:] * tap
        y = _silu(y)
        if tile < 2 * N_DN_HEADS:
            ss = _dot_exact_rhs(y * y, ones_sq, 2)
            y = y * (lax.rsqrt(ss + EPS) * (SCALE if tile < N_DN_HEADS else 1.0))
        qkv_ref[:, sl] = y

    def fox_norm(val, gain_ref, out_ref):
        for hd in range(N_FOX_HEADS):
            sl = slice(hd * HEAD_DIM, (hd + 1) * HEAD_DIM)
            out_ref[:, sl] = _rms(val[:, sl], gain_ref[...]).astype(BF16)

    dgate_ref[...] = proj(D_DN)
    conv_tile(0)
    conv_tile(1)
    fq = proj(D_FOX)
    conv_tile(2)
    conv_tile(3)
    fk = proj(D_FOX)
    fox_norm(fq, qg_ref, fq_ref)
    conv_tile(4)
    conv_tile(5)
    fv_ref[...] = proj(D_FOX).astype(BF16)
    fox_norm(fk, kg_ref, fk_ref)
    conv_tile(6)
    conv_tile(7)
    fgate_ref[...] = proj(D_FOX)
    conv_tile(8)
    conv_tile(9)
    small_ref[...] = proj(LANES)
    conv_tile(10)
    conv_tile(11)


def _inproj_even_call(x, seq_len, norm_g, w_all, layer, q_g, k_g, conv_w):
    t = x.shape[0]
    row = lambda i: (i, 0)
    widths = [(3 * D_DN, F32), (D_DN, F32), (D_FOX, BF16), (D_FOX, BF16), (D_FOX, BF16),
              (D_FOX, F32), (LANES, F32)]
    return pl.pallas_call(
        functools.partial(_inproj_even_kernel, seq_len // FFN_ROWS),
        grid=(t // FFN_ROWS,),
        in_specs=[pl.BlockSpec((FFN_ROWS, D_MODEL), row), _const_spec((1, D_MODEL)),
                  _layer_spec(w_all.shape[1:], layer), _const_spec((1, HEAD_DIM)),
                  _const_spec((1, HEAD_DIM)), _const_spec((CONV_WIDTH, 3 * D_DN))],
        out_specs=[pl.BlockSpec((FFN_ROWS, n), row) for n, _ in widths],
        out_shape=[jax.ShapeDtypeStruct((t, n), dt) for n, dt in widths],
        scratch_shapes=[pltpu.VMEM((SUBLANES + FFN_ROWS, 3 * D_DN), F32)],
        compiler_params=pltpu.CompilerParams(
            dimension_semantics=("arbitrary",), vmem_limit_bytes=VMEM_LIMIT),
        name="inproj_even",
    )(x, norm_g.reshape(1, D_MODEL), w_all, q_g.reshape(1, HEAD_DIM), k_g.reshape(1, HEAD_DIM),
      conv_w)


def _inproj_odd_kernel(x_ref, g_ref, w_ref, o_ref):
    h = _rms(x_ref[...], g_ref[...]).astype(BF16)
    o_ref[...] = jnp.dot(h, w_ref[...], preferred_element_type=F32).astype(BF16)


def _inproj_odd_call(x, norm_g, w_all, layer):
    t = x.shape[0]
    n = w_all.shape[2]
    row = lambda i: (i, 0)
    return pl.pallas_call(
        _inproj_odd_kernel,
        grid=(t // FFN_ROWS,),
        in_specs=[pl.BlockSpec((FFN_ROWS, D_MODEL), row), _const_spec((1, D_MODEL)),
                  _layer_spec(w_all.shape[1:], layer)],
        out_specs=pl.BlockSpec((FFN_ROWS, n), row),
        out_shape=jax.ShapeDtypeStruct((t, n), BF16),
        compiler_params=pltpu.CompilerParams(
            dimension_semantics=("parallel",), vmem_limit_bytes=VMEM_LIMIT),
        name="inproj_odd",
    )(x, norm_g.reshape(1, D_MODEL), w_all)


def _deltanet_kernel(qkv_ref, gate_ref, small_ref, alog_ref, dtb_ref, ng_ref, o_ref,
                     state_ref, sums_ref):
    i = pl.program_id(1)
    n = DN_ROWS
    g = DN_GROUP
    heads = range(N_DN_HEADS)
    groups = range(n // g)

    row = _iota2(g, 0)
    col = _iota2(g, 1)
    diff = row ^ col
    same = (diff >> (DN_CHUNK.bit_length() - 1)) == 0
    incl = same & (col <= row)

    @pl.when(i == 0)
    def _():
        state_ref[...] = jnp.zeros_like(state_ref)
        sums_ref[0:g, :] = jnp.where(incl, 1.0, 0.0).astype(BF16)
        sums_ref[g:2 * g, :] = jnp.where(same, 1.0, 0.0).astype(BF16)

    sm = small_ref[...]
    beta_all = _sigmoid(sm)
    g_all = -jnp.exp(alog_ref[...]) * _softplus(sm + dtb_ref[...])

    hsl = lambda base_col, hd: slice(base_col + hd * HEAD_DIM, base_col + (hd + 1) * HEAD_DIM)
    rs = lambda gi: slice(gi * g, (gi + 1) * g)

    g_sums = [_dot_exact_lhs(sums_ref[...], g_all[rs(gi)], 3) for gi in groups]
    gc_all = [s[0:g] for s in g_sums]
    gl_all = [s[g:2 * g] for s in g_sums]
    gc_all_t = [c.T for c in gc_all]

    lane = lambda arr, hd: arr[:, LANE_DECAY + hd:LANE_DECAY + hd + 1]
    on_diag = diff == 0
    pair = diff == 1
    levels = []
    b = 2
    while b < DN_CHUNK:
        levels.append(b.bit_length() - 1)
        b *= 2
    n_chunks = g // DN_CHUNK
    q, k, gc, gl, egc, decay, rhs, a, x = {}, {}, {}, {}, {}, {}, {}, {}, {}
    sol, attn, qd, kd, egl = {}, {}, {}, {}, {}
    state = [state_ref[hd] for hd in heads]
    outs = {gi: [[] for _ in heads] for gi in groups}

    def prep(gi, hd):
        c = (gi, hd)
        q[c] = qkv_ref[rs(gi), hsl(0, hd)]
        k[c] = qkv_ref[rs(gi), hsl(D_DN, hd)]
        beta = beta_all[rs(gi), LANE_BETA + hd:LANE_BETA + hd + 1]
        gc[c] = lane(gc_all[gi], hd)
        gl[c] = lane(gl_all[gi], hd)
        egc[c] = jnp.exp(gc[c])
        decay[c] = jnp.where(
            incl, jnp.exp(gc[c] - gc_all_t[gi][LANE_DECAY + hd:LANE_DECAY + hd + 1, :]), 0.0)
        kb = k[c] * beta
        rhs[c] = jnp.concatenate(
            [qkv_ref[rs(gi), hsl(2 * D_DN, hd)] * beta, kb * egc[c]], axis=1)
        a[c] = jnp.where(on_diag, 0.0, _dot_nt(kb, k[c]) * decay[c])
        x[c] = jnp.where(on_diag, 1.0, jnp.where(pair, -a[c], 0.0))

    def invert_level(gi, sh):
        off = (diff >> sh) == 1
        t = {hd: _dot(x[gi, hd], jnp.where(off, a[gi, hd], 0.0)) for hd in heads}
        for hd in heads:
            x[gi, hd] = x[gi, hd] - _dot(t[hd], x[gi, hd])

    def solve(gi):
        for hd in heads:
            c = (gi, hd)
            sol[c] = _dot(x[c], rhs[c])
            attn[c] = _dot_nt(q[c], k[c]) * decay[c]
            qd[c] = q[c] * egc[c]
            kd[c] = k[c] * jnp.exp(gl[c] - gc[c])
            egl[c] = jnp.exp(gl[c])

    def scan_chunk(gi, ch):
        r = slice(ch * DN_CHUNK, (ch + 1) * DN_CHUNK)
        for hd in heads:
            c = (gi, hd)
            v_new = sol[c][r, :HEAD_DIM] - _dot(sol[c][r, HEAD_DIM:], state[hd])
            outs[gi][hd].append(_dot(qd[c][r], state[hd]) + _dot(attn[c][r, r], v_new))
            state[hd] = (state[hd] * egl[c][ch * DN_CHUNK:ch * DN_CHUNK + 1, :]
                         + _dot_tn(kd[c][r], v_new))

    def emit(gi):
        for hd in heads:
            o = jnp.concatenate(outs[gi][hd], axis=0)
            sl = hsl(0, hd)
            o_ref[rs(gi), sl] = (_rms(o, ng_ref[...]) * _silu(gate_ref[rs(gi), sl])).astype(BF16)

    for gi in groups:
        for hd in heads:
            prep(gi, hd)
    for sh in levels:
        for gi in groups:
            invert_level(gi, sh)
    for gi in groups:
        solve(gi)
    for gi in groups:
        for ch in range(n_chunks):
            scan_chunk(gi, ch)
        emit(gi)
    for hd in heads:
        state_ref[hd] = state[hd]


def _deltanet_call(qkv, gate, small, a_log, dt_bias, norm_g):
    b, s, _ = qkv.shape
    lane_vec = lambda v: jnp.zeros((1, LANES), F32).at[0, LANE_DECAY:LANE_DECAY + N_DN_HEADS].set(v)
    blk = lambda w: pl.BlockSpec((None, DN_ROWS, w), lambda bi, i: (bi, i, 0))
    return pl.pallas_call(
        _deltanet_kernel,
        grid=(b, s // DN_ROWS),
        in_specs=[blk(3 * D_DN), blk(D_DN), blk(LANES),
                  _const_spec((1, LANES)), _const_spec((1, LANES)), _const_spec((1, HEAD_DIM))],
        out_specs=blk(D_DN),
        out_shape=jax.ShapeDtypeStruct((b, s, D_DN), BF16),
        scratch_shapes=[pltpu.VMEM((N_DN_HEADS, HEAD_DIM, HEAD_DIM), F32),
                        pltpu.VMEM((2 * DN_GROUP, DN_GROUP), BF16)],
        compiler_params=pltpu.CompilerParams(
            dimension_semantics=("parallel", "arbitrary"), vmem_limit_bytes=VMEM_LIMIT),
        name="deltanet",
    )(qkv, gate, small, lane_vec(a_log), lane_vec(dt_bias), norm_g.reshape(1, HEAD_DIM))


def _fox_kernel(q_ref, k_ref, v_ref, small_ref, fbias_ref, gate_ref, o_ref, crow_ref, vaug_ref,
                kn2_ref):
    i = pl.program_id(1)
    n = FOX_ROWS
    s = k_ref.shape[0]
    row = _iota2(n, 0)
    col = _iota2(n, 1)
    causal = col <= row

    @pl.when(i == 0)
    def _():
        tri = jnp.where(causal, 1.0, 0.0).astype(BF16)
        carry = jnp.zeros((1, LANES), F32)
        for blk in range(s // n):
            r = slice(blk * n, (blk + 1) * n)
            lf = _log_sigmoid(small_ref[r, :] + fbias_ref[...])
            c = _dot_exact_lhs(tri, lf, 3) + carry
            carry = c[n - 1:n, :]
            crow_ref[:, r] = (c * LOG2E).T
        for hd in range(N_FOX_HEADS):
            vaug_ref[:, 2 * hd * HEAD_DIM:(2 * hd + 1) * HEAD_DIM] = (
                v_ref[:, hd * HEAD_DIM:(hd + 1) * HEAD_DIM])
            vaug_ref[:, (2 * hd + 1) * HEAD_DIM:(2 * hd + 2) * HEAD_DIM] = (
                jnp.ones((s, HEAD_DIM), BF16))
            kn2 = jnp.float32(0.0)
            for blk in range(s // n):
                kf = k_ref[blk * n:(blk + 1) * n, hd * HEAD_DIM:(hd + 1) * HEAD_DIM].astype(F32)
                kn2 = jnp.maximum(kn2, jnp.max(jnp.sum(kf * kf, axis=-1, keepdims=True)))
            kn2_ref[hd] = kn2

    hsl = lambda hd: slice(hd * HEAD_DIM, (hd + 1) * HEAD_DIM)

    for h0 in range(0, N_FOX_HEADS, FOX_HEADS_PER_LOOP):
        hds = range(h0, h0 + FOX_HEADS_PER_LOOP)

        def step(carry, k0, masked):
            ms = carry[0::2]
            accs = carry[1::2]
            logits = [_dot_nt(q_ref[:, hsl(hd)], k_ref[pl.ds(k0, n), hsl(hd)]) * SCALE_LOG2E
                      - crow_ref[LANE_FORGET + hd:LANE_FORGET + hd + 1, pl.ds(k0, n)]
                      for hd in hds]
            if masked:
                logits = [jnp.where(causal, lg, -1e30) for lg in logits]
            m_new = [jnp.maximum(m, jnp.max(lg, axis=-1, keepdims=True))
                     for m, lg in zip(ms, logits)]
            p = [jnp.exp2(lg - mn) for lg, mn in zip(logits, m_new)]
            pv = [_dot(pp, vaug_ref[pl.ds(k0, n), 2 * hd * HEAD_DIM:(2 * hd + 2) * HEAD_DIM])
                  for pp, hd in zip(p, hds)]
            accs = [jnp.exp2(m - mn) * acc + x for m, mn, acc, x in zip(ms, m_new, accs, pv)]
            out = []
            for mn, acc in zip(m_new, accs):
                out += [mn, acc]
            return tuple(out)

        init = (jnp.full((n, 1), -1e30, F32),
                jnp.zeros((n, 2 * HEAD_DIM), F32)) * FOX_HEADS_PER_LOOP
        carry = step(init, pl.multiple_of(i * n, n), True)

        safe = jnp.bool_(True)
        for hd in hds:
            qf = q_ref[:, hsl(hd)].astype(F32)
            qn2 = jnp.max(jnp.sum(qf * qf, axis=-1, keepdims=True))
            safe = safe & (4.0 * SCALE * SCALE * qn2 * kn2_ref[hd] < FOX_SAFE_LOGIT_GAP ** 2)

        def fixed_max_loop(c):
            ms = c[0::2]

            def body(j, accs):
                k0 = pl.multiple_of(j * n, n)
                p = [jnp.exp2(_dot_nt(q_ref[:, hsl(hd)], k_ref[pl.ds(k0, n), hsl(hd)]) * SCALE_LOG2E
                              - crow_ref[LANE_FORGET + hd:LANE_FORGET + hd + 1, pl.ds(k0, n)] - m)
                     for hd, m in zip(hds, ms)]
                pv = [_dot(pp, vaug_ref[pl.ds(k0, n), 2 * hd * HEAD_DIM:(2 * hd + 2) * HEAD_DIM])
                      for pp, hd in zip(p, hds)]
                return tuple(acc + x for acc, x in zip(accs, pv))

            accs = lax.fori_loop(0, i, body, tuple(c[1::2]))
            out = []
            for m, acc in zip(ms, accs):
                out += [m, acc]
            return tuple(out)

        def online_loop(c):
            return lax.fori_loop(0, i, lambda j, cc: step(cc, pl.multiple_of(j * n, n), False), c)

        carry = lax.cond(safe, fixed_max_loop, online_loop, carry)
        for idx, hd in enumerate(hds):
            sl = slice(hd * HEAD_DIM, (hd + 1) * HEAD_DIM)
            acc = carry[2 * idx + 1]
            o = acc[:, :HEAD_DIM] / acc[:, HEAD_DIM:]
            o_ref[:, sl] = (o * _sigmoid(gate_ref[:, sl])).astype(BF16)


def _fox_call(q, k, v, small, f_bias, gate):
    b, s, _ = q.shape
    blk = lambda w: pl.BlockSpec((None, FOX_ROWS, w), lambda bi, i: (bi, i, 0))
    full = lambda w: pl.BlockSpec((None, s, w), lambda bi, i: (bi, 0, 0))
    fb = jnp.zeros((1, LANES), F32).at[0, LANE_FORGET:LANE_FORGET + N_FOX_HEADS].set(f_bias)
    return pl.pallas_call(
        _fox_kernel,
        grid=(b, s // FOX_ROWS),
        in_specs=[blk(D_FOX), full(D_FOX), full(D_FOX), full(LANES), _const_spec((1, LANES)),
                  blk(D_FOX)],
        out_specs=blk(D_FOX),
        out_shape=jax.ShapeDtypeStruct((b, s, D_FOX), BF16),
        scratch_shapes=[pltpu.VMEM((LANES, s), F32), pltpu.VMEM((s, 2 * D_FOX), BF16),
                        pltpu.SMEM((N_FOX_HEADS,), F32)],
        compiler_params=pltpu.CompilerParams(
            dimension_semantics=("parallel", "arbitrary"), vmem_limit_bytes=VMEM_LIMIT),
        name="fox_attention",
    )(q, k, v, small, fb, gate)


SB_CUTOFF = -105.0 * LOG2E


def _sb_kernel(q_ref, k_ref, v_ref, o_ref):
    n = SB_ROWS
    s = k_ref.shape[0]
    row = _iota2(n, 0)
    col = _iota2(n, 1)
    before = col < row
    later = jnp.where(row > col, 1.0, 0.0).astype(BF16)
    later2 = jnp.concatenate([later, later], axis=0)
    total = lambda lg: jnp.broadcast_to(jnp.sum(lg, axis=1, keepdims=True), (lg.shape[0], LANES))

    def block(q, k0, rest, masked):
        z = _dot_nt(q, k_ref[pl.ds(k0, n), :]) * SCALE_LOG2E
        soft = jnp.log2(1.0 + jnp.exp2(-jnp.abs(z)))
        lsz = jnp.minimum(z, 0.0) - soft
        lsn = lsz - z
        lg = jnp.where(before, lsn, 0.0) if masked else lsn
        hi = lg.astype(BF16)
        lo = (lg - hi.astype(F32)).astype(BF16)
        sums = jnp.dot(jnp.concatenate([hi, lo], axis=1), later2, preferred_element_type=F32)
        tail = sums + jnp.concatenate([rest] * (n // LANES), axis=1)
        a = jnp.exp2(lsz + tail)
        if masked:
            a = jnp.where(before, a, 0.0)
        return _dot(a, v_ref[pl.ds(k0, n), :]), rest + total(lg)

    def pairs(qs, k_los, rests, masked):
        rep = lambda r: jnp.concatenate([r] * (n // LANES), axis=1)
        z = [_dot_nt(q, k_ref[pl.ds(k_lo, 2 * n), :]) * SCALE_LOG2E for q, k_lo in zip(qs, k_los)]
        lsz = [jnp.minimum(zz, 0.0) - jnp.log2(1.0 + jnp.exp2(-jnp.abs(zz))) for zz in z]
        lsn = [l - zz for l, zz in zip(lsz, z)]
        lg = [jnp.concatenate([jnp.where(before, l[:, n:], 0.0) if masked else l[:, n:],
                               l[:, :n]], axis=0) for l in lsn]
        hi = [l.astype(BF16) for l in lg]
        lo = [(l - h.astype(F32)).astype(BF16) for l, h in zip(lg, hi)]
        sums = [jnp.dot(jnp.concatenate([h, l], axis=1), later2, preferred_element_type=F32)
                for h, l in zip(hi, lo)]
        tot = [total(l) for l in lg]
        rest_mid = [r + t[:n] for r, t in zip(rests, tot)]
        tail = [jnp.concatenate([sm[n:] + rep(rm), sm[:n] + rep(r)], axis=1)
                for sm, rm, r in zip(sums, rest_mid, rests)]
        a = [jnp.exp2(l + t) for l, t in zip(lsz, tail)]
        if masked:
            a = [jnp.concatenate([aa[:, :n], jnp.where(before, aa[:, n:], 0.0)], axis=1) for aa in a]
        pv = [_dot(aa, v_ref[pl.ds(k_lo, 2 * n), :]) for aa, k_lo in zip(a, k_los)]
        return pv, [rm + t[n:] for rm, t in zip(rest_mid, tot)]

    def pair(q, k_lo, rest, masked):
        pv, rest = pairs([q], [k_lo], [rest], masked)
        return pv[0], rest[0]

    def live(rest):
        return (jnp.max(rest) > SB_CUTOFF).astype(jnp.int32)

    acc0, _ = block(q_ref[0:n, :], 0, jnp.zeros((n, LANES), F32), True)
    o_ref[0:n, :] = acc0.astype(BF16)

    def first(idx):
        qs = [q_ref[i * n:(i + 1) * n, :] for i in idx]
        accs, rests = pairs(qs, [(i - 1) * n for i in idx],
                            [jnp.zeros((n, LANES), F32)] * len(idx), True)
        return list(zip(qs, accs, rests, [live(r) for r in rests]))

    def finish(i, q, acc, rest, alive0):
        n_pairs = (i - 1) // 2

        def cond(c):
            return (c[0] < n_pairs) & (c[3] > 0)

        def body(c):
            t, acc, rest, _ = c
            k_lo = pl.multiple_of((i - 3 - 2 * t) * n, n)
            pv, rest = pair(q, k_lo, rest, False)
            return t + 1, acc + pv, rest, live(rest)

        alive = alive0
        if n_pairs > 0:
            _, acc, rest, alive = lax.while_loop(cond, body, (jnp.int32(0), acc, rest, alive0))
        if (i - 1) % 2 == 1:
            acc = lax.cond(alive > 0, lambda: acc + block(q, 0, rest, False)[0], lambda: acc)
        o_ref[i * n:(i + 1) * n, :] = acc.astype(BF16)

    blocks = list(range(1, s // n))
    for g0 in range(0, len(blocks), SB_QBLOCKS_PER_STEP):
        idx = blocks[g0:g0 + SB_QBLOCKS_PER_STEP]
        for i, start in zip(idx, first(idx)):
            finish(i, *start)


def _sb_call(qkv):
    b, s, _ = qkv.shape
    nh = N_SB_HEADS
    head = lambda off: pl.BlockSpec((None, s, HEAD_DIM), lambda bi, h: (bi, 0, off + h))
    return pl.pallas_call(
        _sb_kernel,
        grid=(b, nh),
        in_specs=[head(0), head(nh), head(2 * nh)],
        out_specs=head(0),
        out_shape=jax.ShapeDtypeStruct((b, s, nh * HEAD_DIM), BF16),
        compiler_params=pltpu.CompilerParams(
            dimension_semantics=("parallel", "parallel"), vmem_limit_bytes=VMEM_LIMIT),
        name="stickbreaking_attention",
    )(qkv, qkv, qkv)


EVEN_W_ROWS = 256
EVEN_W_COLS = 4 * D_DN + 4 * D_FOX + LANES


def _even_w_in_kernel(wt_ref, o_ref):
    dn_wide = 4 * D_DN
    gates = 2 * N_DN_HEADS
    fox_wide = 4 * D_FOX
    wt = wt_ref[...]
    o_ref[:, :dn_wide] = wt[:dn_wide].T.astype(BF16)
    o_ref[:, dn_wide:dn_wide + fox_wide] = (
        wt[dn_wide + gates:dn_wide + gates + fox_wide].T.astype(BF16))
    small = jnp.concatenate(
        [wt[dn_wide:dn_wide + gates], wt[dn_wide + gates + fox_wide:],
         jnp.zeros((LANES - gates - N_FOX_HEADS, wt.shape[1]), F32)], axis=0)
    o_ref[:, dn_wide + fox_wide:] = small.T.astype(BF16)


def _even_w_in(w):
    layers, rows, cols = w.shape
    return pl.pallas_call(
        _even_w_in_kernel,
        grid=(layers, rows // EVEN_W_ROWS),
        in_specs=[pl.BlockSpec((None, cols, EVEN_W_ROWS), lambda l, r: (l, 0, r))],
        out_specs=pl.BlockSpec((None, EVEN_W_ROWS, EVEN_W_COLS), lambda l, r: (l, r, 0)),
        out_shape=jax.ShapeDtypeStruct((layers, rows, EVEN_W_COLS), BF16),
        compiler_params=pltpu.CompilerParams(
            dimension_semantics=("parallel", "parallel"), vmem_limit_bytes=VMEM_LIMIT),
        name="reorder_w_in_even",
    )(jnp.swapaxes(w, 1, 2))


def kernel(x, norm_ffn1, ffn1_w_gu, ffn1_w_down, norm_mix, w_in_even, dn_conv_w, dn_a_log,
           dn_dt_bias, dn_norm_g, fox_q_norm_g, fox_k_norm_g, fox_f_bias, w_out_even,
           w_in_odd, w_out_odd, norm_ffn2, ffn2_w_gu, ffn2_w_down):
    b, s, d = x.shape
    depth = norm_ffn1.shape[0]
    t = b * s
    bf = lambda a: a.astype(BF16)
    wgu1, wd1, wgu2, wd2 = bf(ffn1_w_gu), bf(ffn1_w_down), bf(ffn2_w_gu), bf(ffn2_w_down)
    w_in_e, w_out_e = _even_w_in(w_in_even), bf(w_out_even)
    w_in_o, w_out_o = bf(w_in_odd), bf(w_out_odd)
    x = x.reshape(t, d)
    for l in range(depth):
        j = l // 2
        x = _ffn_call(x, [], norm_ffn1[l], wgu1, wd1, l)
        if l % 2 == 0:
            qkv, dgate, fq, fk, fv, fgate, small = _inproj_even_call(
                x, s, norm_mix[l], w_in_e, j, fox_q_norm_g[j], fox_k_norm_g[j], dn_conv_w[j])
            r3 = lambda a: a.reshape(b, s, a.shape[-1])
            o_dn = _deltanet_call(r3(qkv), r3(dgate), r3(small), dn_a_log[j], dn_dt_bias[j],
                                  dn_norm_g[j])
            o_fox = _fox_call(r3(fq), r3(fk), r3(fv), r3(small), fox_f_bias[j], r3(fgate))
            pre = [(o_dn.reshape(t, D_DN), w_out_e, j, 0), (o_fox.reshape(t, D_FOX), w_out_e, j, 1)]
        else:
            qkv = _inproj_odd_call(x, norm_mix[l], w_in_o, j)
            o = _sb_call(qkv.reshape(b, s, qkv.shape[-1]))
            pre = [(o.reshape(t, o.shape[-1]), w_out_o, j, 0)]
        x = _ffn_call(x, pre, norm_ffn2[l], wgu2, wd2, l)
    return x.reshape(b, s, d)
```

```python
import functools

import jax
import jax.numpy as jnp
from jax import lax
from jax.experimental import pallas as pl
from jax.experimental.pallas import tpu as pltpu

F32 = jnp.float32
BF16 = jnp.bfloat16

D_MODEL = 1024
HEAD_DIM = 128
N_DN_HEADS = 4
N_FOX_HEADS = 4
N_SB_HEADS = 8
D_DN = N_DN_HEADS * HEAD_DIM
D_FOX = N_FOX_HEADS * HEAD_DIM
CONV_WIDTH = 4
D_FF = 2816
EPS = 1e-6
SCALE = HEAD_DIM ** -0.5
LOG2E = 1.4426950408889634
SCALE_LOG2E = SCALE * LOG2E

LANES = 128
SUBLANES = 8
VMEM_LIMIT = 56 * 1024 * 1024

FFN_ROWS = 512
SB_ROWS = 256
SB_QBLOCKS_PER_STEP = 15
FOX_ROWS = 512
FOX_SAFE_LOGIT_GAP = 60.0
FOX_HEADS_PER_LOOP = 4
DN_ROWS = 512
DN_GROUP = 256
DN_CHUNK = 256

LANE_BETA = 0
LANE_DECAY = N_DN_HEADS
LANE_FORGET = 2 * N_DN_HEADS


def _dot(a, b):
    return jnp.dot(a.astype(BF16), b.astype(BF16), preferred_element_type=F32)


def _dot_nt(a, b):
    return lax.dot_general(a.astype(BF16), b.astype(BF16), (((1,), (1,)), ((), ())),
                           preferred_element_type=F32)


def _dot_tn(a, b):
    return lax.dot_general(a.astype(BF16), b.astype(BF16), (((0,), (0,)), ((), ())),
                           preferred_element_type=F32)


def _split_bf16(x, parts):
    out = []
    r = x
    for _ in range(parts - 1):
        p = r.astype(BF16)
        out.append(p)
        r = r - p.astype(F32)
    out.append(r.astype(BF16))
    return out


def _dot_exact_lhs(m_bf16, x, parts):
    acc = None
    for p in _split_bf16(x, parts):
        t = jnp.dot(m_bf16, p, preferred_element_type=F32)
        acc = t if acc is None else acc + t
    return acc


def _dot_exact_rhs(x, m_bf16, parts):
    acc = None
    for p in _split_bf16(x, parts):
        t = jnp.dot(p, m_bf16, preferred_element_type=F32)
        acc = t if acc is None else acc + t
    return acc


def _rms(xf, g):
    return xf * lax.rsqrt(jnp.mean(xf * xf, axis=-1, keepdims=True) + EPS) * g


def _softplus(x):
    return jnp.maximum(x, 0.0) + jnp.log1p(jnp.exp(-jnp.abs(x)))


def _log_sigmoid(x):
    return jnp.minimum(x, 0.0) - jnp.log1p(jnp.exp(-jnp.abs(x)))


def _sigmoid(x):
    return 1.0 / (1.0 + jnp.exp(-x))


def _silu(x):
    return x * _sigmoid(x)


def _iota2(n, axis):
    return lax.broadcasted_iota(jnp.int32, (n, n), axis)


def _const_spec(shape):
    return pl.BlockSpec(shape, lambda *_: (0,) * len(shape))


def _layer_spec(shape, layer, row_block=0):
    return pl.BlockSpec((None,) + tuple(shape), lambda *_: (layer, row_block, 0))


def _ffn_kernel(n_pre, *refs):
    x_ref = refs[0]
    pre = refs[1:1 + 2 * n_pre]
    g_ref, wgu_ref, wd_ref, o_ref = refs[1 + 2 * n_pre:]
    x = x_ref[...]
    for p in range(n_pre):
        x = x + jnp.dot(pre[2 * p][...], pre[2 * p + 1][...], preferred_element_type=F32)
    h = _rms(x, g_ref[...]).astype(BF16)
    gate = jnp.dot(h, wgu_ref[:, :D_FF], preferred_element_type=F32)
    up = jnp.dot(h, wgu_ref[:, D_FF:], preferred_element_type=F32)
    act = (_silu(gate) * up).astype(BF16)
    o_ref[...] = x + 0.5 * jnp.dot(act, wd_ref[...], preferred_element_type=F32)


def _ffn_call(x, pre, norm_g, wgu_all, wd_all, layer):
    t = x.shape[0]
    row = lambda i: (i, 0)
    in_specs = [pl.BlockSpec((FFN_ROWS, D_MODEL), row)]
    args = [x]
    for o, w_all, j, rb in pre:
        in_specs += [pl.BlockSpec((FFN_ROWS, o.shape[1]), row),
                     _layer_spec((o.shape[1], D_MODEL), j, rb)]
        args += [o, w_all]
    in_specs += [_const_spec((1, D_MODEL)), _layer_spec(wgu_all.shape[1:], layer),
                 _layer_spec(wd_all.shape[1:], layer)]
    args += [norm_g.reshape(1, D_MODEL), wgu_all, wd_all]
    return pl.pallas_call(
        functools.partial(_ffn_kernel, len(pre)),
        grid=(t // FFN_ROWS,),
        in_specs=in_specs,
        out_specs=pl.BlockSpec((FFN_ROWS, D_MODEL), row),
        out_shape=jax.ShapeDtypeStruct((t, D_MODEL), F32),
        compiler_params=pltpu.CompilerParams(
            dimension_semantics=("parallel",), vmem_limit_bytes=VMEM_LIMIT),
        name="ffn",
    )(*args)


def _inproj_even_kernel(tiles_per_seq, x_ref, g_ref, w_ref, qg_ref, kg_ref, convw_ref,
                        qkv_ref, dgate_ref, fq_ref, fk_ref, fv_ref, fgate_ref, small_ref,
                        xx_ref):
    i = pl.program_id(0)
    n = FFN_ROWS
    h = _rms(x_ref[...], g_ref[...]).astype(BF16)
    c = 0

    def proj(width):
        nonlocal c
        y = jnp.dot(h, w_ref[:, c:c + width], preferred_element_type=F32)
        c += width
        return y

    @pl.when(i % tiles_per_seq == 0)
    def _():
        xx_ref[n:n + SUBLANES, :] = jnp.zeros((SUBLANES, 3 * D_DN), F32)

    xx_ref[0:SUBLANES, :] = xx_ref[n:n + SUBLANES, :]
    xx_ref[SUBLANES:SUBLANES + n, :] = proj(3 * D_DN)
    ones_sq = jnp.ones((HEAD_DIM, HEAD_DIM), BF16)

    def conv_tile(tile):
        sl = slice(tile * HEAD_DIM, (tile + 1) * HEAD_DIM)
        xx = xx_ref[:, sl]
        y = xx[SUBLANES:] * convw_ref[CONV_WIDTH - 1:CONV_WIDTH, sl]
        for d in range(1, CONV_WIDTH):
            tap = convw_ref[CONV_WIDTH - 1 - d:CONV_WIDTH - d, sl]
            y = y + pltpu.roll(xx, d, axis=0)[SUBLANES:] * tap
        y = _silu(y)
        if tile < 2 * N_DN_HEADS:
            ss = _dot_exact_rhs(y * y, ones_sq, 2)
            y = y * (lax.rsqrt(ss + EPS) * (SCALE if tile < N_DN_HEADS else 1.0))
        qkv_ref[:, sl] = y

    def fox_norm(val, gain_ref, out_ref):
        for hd in range(N_FOX_HEADS):
            sl = slice(hd * HEAD_DIM, (hd + 1) * HEAD_DIM)
            out_ref[:, sl] = _rms(val[:, sl], gain_ref[...]).astype(BF16)

    dgate_ref[...] = proj(D_DN)
    conv_tile(0)
    conv_tile(1)
    fq = proj(D_FOX)
    conv_tile(2)
    conv_tile(3)
    fk = proj(D_FOX)
    fox_norm(fq, qg_ref, fq_ref)
    conv_tile(4)
    conv_tile(5)
    fv_ref[...] = proj(D_FOX).astype(BF16)
    fox_norm(fk, kg_ref, fk_ref)
    conv_tile(6)
    conv_tile(7)
    fgate_ref[...] = proj(D_FOX)
    conv_tile(8)
    conv_tile(9)
    small_ref[...] = proj(LANES)
    conv_tile(10)
    conv_tile(11)


def _inproj_even_call(x, seq_len, norm_g, w_all, layer, q_g, k_g, conv_w):
    t = x.shape[0]
    row = lambda i: (i, 0)
    widths = [(3 * D_DN, F32), (D_DN, F32), (D_FOX, BF16), (D_FOX, BF16), (D_FOX, BF16),
              (D_FOX, F32), (LANES, F32)]
    return pl.pallas_call(
        functools.partial(_inproj_even_kernel, seq_len // FFN_ROWS),
        grid=(t // FFN_ROWS,),
        in_specs=[pl.BlockSpec((FFN_ROWS, D_MODEL), row), _const_spec((1, D_MODEL)),
                  _layer_spec(w_all.shape[1:], layer), _const_spec((1, HEAD_DIM)),
                  _const_spec((1, HEAD_DIM)), _const_spec((CONV_WIDTH, 3 * D_DN))],
        out_specs=[pl.BlockSpec((FFN_ROWS, n), row) for n, _ in widths],
        out_shape=[jax.ShapeDtypeStruct((t, n), dt) for n, dt in widths],
        scratch_shapes=[pltpu.VMEM((SUBLANES + FFN_ROWS, 3 * D_DN), F32)],
        compiler_params=pltpu.CompilerParams(
            dimension_semantics=("arbitrary",), vmem_limit_bytes=VMEM_LIMIT),
        name="inproj_even",
    )(x, norm_g.reshape(1, D_MODEL), w_all, q_g.reshape(1, HEAD_DIM), k_g.reshape(1, HEAD_DIM),
      conv_w)


def _inproj_odd_kernel(x_ref, g_ref, w_ref, o_ref):
    h = _rms(x_ref[...], g_ref[...]).astype(BF16)
    o_ref[...] = jnp.dot(h, w_ref[...], preferred_element_type=F32).astype(BF16)


def _inproj_odd_call(x, norm_g, w_all, layer):
    t = x.shape[0]
    n = w_all.shape[2]
    row = lambda i: (i, 0)
    return pl.pallas_call(
        _inproj_odd_kernel,
        grid=(t // FFN_ROWS,),
        in_specs=[pl.BlockSpec((FFN_ROWS, D_MODEL), row), _const_spec((1, D_MODEL)),
                  _layer_spec(w_all.shape[1:], layer)],
        out_specs=pl.BlockSpec((FFN_ROWS, n), row),
        out_shape=jax.ShapeDtypeStruct((t, n), BF16),
        compiler_params=pltpu.CompilerParams(
            dimension_semantics=("parallel",), vmem_limit_bytes=VMEM_LIMIT),
        name="inproj_odd",
    )(x, norm_g.reshape(1, D_MODEL), w_all)


def _deltanet_kernel(qkv_ref, gate_ref, small_ref, alog_ref, dtb_ref, ng_ref, o_ref,
                     state_ref, sums_ref):
    i = pl.program_id(1)
    n = DN_ROWS
    g = DN_GROUP
    heads = range(N_DN_HEADS)
    groups = range(n // g)

    row = _iota2(g, 0)
    col = _iota2(g, 1)
    diff = row ^ col
    same = (diff >> (DN_CHUNK.bit_length() - 1)) == 0
    incl = same & (col <= row)

    @pl.when(i == 0)
    def _():
        state_ref[...] = jnp.zeros_like(state_ref)
        sums_ref[0:g, :] = jnp.where(incl, 1.0, 0.0).astype(BF16)
        sums_ref[g:2 * g, :] = jnp.where(same, 1.0, 0.0).astype(BF16)

    sm = small_ref[...]
    beta_all = _sigmoid(sm)
    g_all = -jnp.exp(alog_ref[...]) * _softplus(sm + dtb_ref[...])

    hsl = lambda base_col, hd: slice(base_col + hd * HEAD_DIM, base_col + (hd + 1) * HEAD_DIM)
    rs = lambda gi: slice(gi * g, (gi + 1) * g)

    g_sums = [_dot_exact_lhs(sums_ref[...], g_all[rs(gi)], 3) for gi in groups]
    gc_all = [s[0:g] for s in g_sums]
    gl_all = [s[g:2 * g] for s in g_sums]
    gc_all_t = [c.T for c in gc_all]

    lane = lambda arr, hd: arr[:, LANE_DECAY + hd:LANE_DECAY + hd + 1]
    on_diag = diff == 0
    pair = diff == 1
    levels = []
    b = 2
    while b < DN_CHUNK:
        levels.append(b.bit_length() - 1)
        b *= 2
    n_chunks = g // DN_CHUNK
    q, k, gc, gl, egc, decay, rhs, a, x = {}, {}, {}, {}, {}, {}, {}, {}, {}
    sol, attn, qd, kd, egl = {}, {}, {}, {}, {}
    state = [state_ref[hd] for hd in heads]
    outs = {gi: [[] for _ in heads] for gi in groups}

    def prep(gi, hd):
        c = (gi, hd)
        q[c] = qkv_ref[rs(gi), hsl(0, hd)]
        k[c] = qkv_ref[rs(gi), hsl(D_DN, hd)]
        beta = beta_all[rs(gi), LANE_BETA + hd:LANE_BETA + hd + 1]
        gc[c] = lane(gc_all[gi], hd)
        gl[c] = lane(gl_all[gi], hd)
        egc[c] = jnp.exp(gc[c])
        decay[c] = jnp.where(
            incl, jnp.exp(gc[c] - gc_all_t[gi][LANE_DECAY + hd:LANE_DECAY + hd + 1, :]), 0.0)
        kb = k[c] * beta
        rhs[c] = jnp.concatenate(
            [qkv_ref[rs(gi), hsl(2 * D_DN, hd)] * beta, kb * egc[c]], axis=1)
        a[c] = jnp.where(on_diag, 0.0, _dot_nt(kb, k[c]) * decay[c])
        x[c] = jnp.where(on_diag, 1.0, jnp.where(pair, -a[c], 0.0))

    def invert_level(gi, sh):
        off = (diff >> sh) == 1
        t = {hd: _dot(x[gi, hd], jnp.where(off, a[gi, hd], 0.0)) for hd in heads}
        for hd in heads:
            x[gi, hd] = x[gi, hd] - _dot(t[hd], x[gi, hd])

    def solve(gi):
        for hd in heads:
            c = (gi, hd)
            sol[c] = _dot(x[c], rhs[c])
            attn[c] = _dot_nt(q[c], k[c]) * decay[c]
            qd[c] = q[c] * egc[c]
            kd[c] = k[c] * jnp.exp(gl[c] - gc[c])
            egl[c] = jnp.exp(gl[c])

    def scan_chunk(gi, ch):
        r = slice(ch * DN_CHUNK, (ch + 1) * DN_CHUNK)
        for hd in heads:
            c = (gi, hd)
            v_new = sol[c][r, :HEAD_DIM] - _dot(sol[c][r, HEAD_DIM:], state[hd])
            outs[gi][hd].append(_dot(qd[c][r], state[hd]) + _dot(attn[c][r, r], v_new))
            state[hd] = (state[hd] * egl[c][ch * DN_CHUNK:ch * DN_CHUNK + 1, :]
                         + _dot_tn(kd[c][r], v_new))

    def emit(gi):
        for hd in heads:
            o = jnp.concatenate(outs[gi][hd], axis=0)
            sl = hsl(0, hd)
            o_ref[rs(gi), sl] = (_rms(o, ng_ref[...]) * _silu(gate_ref[rs(gi), sl])).astype(BF16)

    for gi in groups:
        for hd in heads:
            prep(gi, hd)
    for sh in levels:
        for gi in groups:
            invert_level(gi, sh)
    for gi in groups:
        solve(gi)
    for gi in groups:
        for ch in range(n_chunks):
            scan_chunk(gi, ch)
        emit(gi)
    for hd in heads:
        state_ref[hd] = state[hd]


def _deltanet_call(qkv, gate, small, a_log, dt_bias, norm_g):
    b, s, _ = qkv.shape
    lane_vec = lambda v: jnp.zeros((1, LANES), F32).at[0, LANE_DECAY:LANE_DECAY + N_DN_HEADS].set(v)
    blk = lambda w: pl.BlockSpec((None, DN_ROWS, w), lambda bi, i: (bi, i, 0))
    return pl.pallas_call(
        _deltanet_kernel,
        grid=(b, s // DN_ROWS),
        in_specs=[blk(3 * D_DN), blk(D_DN), blk(LANES),
                  _const_spec((1, LANES)), _const_spec((1, LANES)), _const_spec((1, HEAD_DIM))],
        out_specs=blk(D_DN),
        out_shape=jax.ShapeDtypeStruct((b, s, D_DN), BF16),
        scratch_shapes=[pltpu.VMEM((N_DN_HEADS, HEAD_DIM, HEAD_DIM), F32),
                        pltpu.VMEM((2 * DN_GROUP, DN_GROUP), BF16)],
        compiler_params=pltpu.CompilerParams(
            dimension_semantics=("parallel", "arbitrary"), vmem_limit_bytes=VMEM_LIMIT),
        name="deltanet",
    )(qkv, gate, small, lane_vec(a_log), lane_vec(dt_bias), norm_g.reshape(1, HEAD_DIM))


def _fox_kernel(q_ref, k_ref, v_ref, small_ref, fbias_ref, gate_ref, o_ref, crow_ref, vaug_ref,
                kn2_ref, ccol_ref):
    i = pl.program_id(1)
    n = FOX_ROWS
    s = k_ref.shape[0]
    row = _iota2(n, 0)
    col = _iota2(n, 1)
    causal = col <= row

    @pl.when(i == 0)
    def _():
        tri = jnp.where(causal, 1.0, 0.0).astype(BF16)
        carry = jnp.zeros((1, LANES), F32)
        for blk in range(s // n):
            r = slice(blk * n, (blk + 1) * n)
            lf = _log_sigmoid(small_ref[r, :] + fbias_ref[...])
            c = _dot_exact_lhs(tri, lf, 3) + carry
            carry = c[n - 1:n, :]
            ccol_ref[r, :] = c * LOG2E
            crow_ref[:, r] = (c * LOG2E).T
        for hd in range(N_FOX_HEADS):
            vaug_ref[:, 2 * hd * HEAD_DIM:(2 * hd + 1) * HEAD_DIM] = (
                v_ref[:, hd * HEAD_DIM:(hd + 1) * HEAD_DIM])
            vaug_ref[:, (2 * hd + 1) * HEAD_DIM:(2 * hd + 2) * HEAD_DIM] = (
                jnp.ones((s, HEAD_DIM), BF16))
            kn2 = jnp.float32(0.0)
            for blk in range(s // n):
                kf = k_ref[blk * n:(blk + 1) * n, hd * HEAD_DIM:(hd + 1) * HEAD_DIM].astype(F32)
                kn2 = jnp.maximum(kn2, jnp.max(jnp.sum(kf * kf, axis=-1, keepdims=True)))
            kn2_ref[hd] = kn2

    hsl = lambda hd: slice(hd * HEAD_DIM, (hd + 1) * HEAD_DIM)

    for h0 in range(0, N_FOX_HEADS, FOX_HEADS_PER_LOOP):
        hds = range(h0, h0 + FOX_HEADS_PER_LOOP)

        def step(carry, k0, masked):
            ms = carry[0::2]
            accs = carry[1::2]
            logits = [_dot_nt(q_ref[:, hsl(hd)], k_ref[pl.ds(k0, n), hsl(hd)]) * SCALE_LOG2E
                      - crow_ref[LANE_FORGET + hd:LANE_FORGET + hd + 1, pl.ds(k0, n)]
                      for hd in hds]
            if masked:
                logits = [jnp.where(causal, lg, -1e30) for lg in logits]
            m_new = [jnp.maximum(m, jnp.max(lg, axis=-1, keepdims=True))
                     for m, lg in zip(ms, logits)]
            p = [jnp.exp2(lg - mn) for lg, mn in zip(logits, m_new)]
            pv = [_dot(pp, vaug_ref[pl.ds(k0, n), 2 * hd * HEAD_DIM:(2 * hd + 2) * HEAD_DIM])
                  for pp, hd in zip(p, hds)]
            accs = [jnp.exp2(m - mn) * acc + x for m, mn, acc, x in zip(ms, m_new, accs, pv)]
            out = []
            for mn, acc in zip(m_new, accs):
                out += [mn, acc]
            return tuple(out)

        init = (jnp.full((n, 1), -1e30, F32),
                jnp.zeros((n, 2 * HEAD_DIM), F32)) * FOX_HEADS_PER_LOOP
        q0 = pl.multiple_of(i * n, n)

        safe = jnp.bool_(True)
        for hd in hds:
            qf = q_ref[:, hsl(hd)].astype(F32)
            qn2 = jnp.max(jnp.sum(qf * qf, axis=-1, keepdims=True))
            safe = safe & (4.0 * SCALE * SCALE * qn2 * kn2_ref[hd] < FOX_SAFE_LOGIT_GAP ** 2)

        def fixed_stabiliser(_):
            cq = [ccol_ref[pl.ds(q0, n), LANE_FORGET + hd:LANE_FORGET + hd + 1] for hd in hds]

            def weights(k0):
                return [jnp.exp2(_dot_nt(q_ref[:, hsl(hd)], k_ref[pl.ds(k0, n), hsl(hd)])
                                 * SCALE_LOG2E
                                 - crow_ref[LANE_FORGET + hd:LANE_FORGET + hd + 1, pl.ds(k0, n)] + c)
                        for hd, c in zip(hds, cq)]

            def values(p, k0):
                return [_dot(pp, vaug_ref[pl.ds(k0, n), 2 * hd * HEAD_DIM:(2 * hd + 2) * HEAD_DIM])
                        for pp, hd in zip(p, hds)]

            accs = values([jnp.where(causal, pp, 0.0) for pp in weights(q0)], q0)

            def body(j, accs):
                k0 = pl.multiple_of(j * n, n)
                return tuple(acc + x for acc, x in zip(accs, values(weights(k0), k0)))

            accs = lax.fori_loop(0, i, body, tuple(accs))
            out = []
            for c, acc in zip(cq, accs):
                out += [-c, acc]
            return tuple(out)

        def online(_):
            c = step(init, q0, True)
            return lax.fori_loop(0, i, lambda j, cc: step(cc, pl.multiple_of(j * n, n), False), c)

        carry = lax.cond(safe, fixed_stabiliser, online, 0)
        for idx, hd in enumerate(hds):
            sl = slice(hd * HEAD_DIM, (hd + 1) * HEAD_DIM)
            acc = carry[2 * idx + 1]
            o = acc[:, :HEAD_DIM] / acc[:, HEAD_DIM:]
            o_ref[:, sl] = (o * _sigmoid(gate_ref[:, sl])).astype(BF16)


def _fox_call(q, k, v, small, f_bias, gate):
    b, s, _ = q.shape
    blk = lambda w: pl.BlockSpec((None, FOX_ROWS, w), lambda bi, i: (bi, i, 0))
    full = lambda w: pl.BlockSpec((None, s, w), lambda bi, i: (bi, 0, 0))
    fb = jnp.zeros((1, LANES), F32).at[0, LANE_FORGET:LANE_FORGET + N_FOX_HEADS].set(f_bias)
    return pl.pallas_call(
        _fox_kernel,
        grid=(b, s // FOX_ROWS),
        in_specs=[blk(D_FOX), full(D_FOX), full(D_FOX), full(LANES), _const_spec((1, LANES)),
                  blk(D_FOX)],
        out_specs=blk(D_FOX),
        out_shape=jax.ShapeDtypeStruct((b, s, D_FOX), BF16),
        scratch_shapes=[pltpu.VMEM((LANES, s), F32), pltpu.VMEM((s, 2 * D_FOX), BF16),
                        pltpu.SMEM((N_FOX_HEADS,), F32), pltpu.VMEM((s, LANES), F32)],
        compiler_params=pltpu.CompilerParams(
            dimension_semantics=("parallel", "arbitrary"), vmem_limit_bytes=VMEM_LIMIT),
        name="fox_attention",
    )(q, k, v, small, fb, gate)


SB_CUTOFF = -105.0 * LOG2E


def _sb_kernel(q_ref, k_ref, v_ref, o_ref):
    n = SB_ROWS
    s = k_ref.shape[0]
    row = _iota2(n, 0)
    col = _iota2(n, 1)
    before = col < row
    later = jnp.where(row > col, 1.0, 0.0).astype(BF16)
    later2 = jnp.concatenate([later, later], axis=0)
    total = lambda lg: jnp.broadcast_to(jnp.sum(lg, axis=1, keepdims=True), (lg.shape[0], LANES))

    def block(q, k0, rest, masked):
        z = _dot_nt(q, k_ref[pl.ds(k0, n), :]) * SCALE_LOG2E
        soft = jnp.log2(1.0 + jnp.exp2(-jnp.abs(z)))
        lsz = jnp.minimum(z, 0.0) - soft
        lsn = lsz - z
        lg = jnp.where(before, lsn, 0.0) if masked else lsn
        hi = lg.astype(BF16)
        lo = (lg - hi.astype(F32)).astype(BF16)
        sums = jnp.dot(jnp.concatenate([hi, lo], axis=1), later2, preferred_element_type=F32)
        tail = sums + jnp.concatenate([rest] * (n // LANES), axis=1)
        a = jnp.exp2(lsz + tail)
        if masked:
            a = jnp.where(before, a, 0.0)
        return _dot(a, v_ref[pl.ds(k0, n), :]), rest + total(lg)

    def pairs(qs, k_los, rests, masked):
        rep = lambda r: jnp.concatenate([r] * (n // LANES), axis=1)
        z = [_dot_nt(q, k_ref[pl.ds(k_lo, 2 * n), :]) * SCALE_LOG2E for q, k_lo in zip(qs, k_los)]
        lsz = [jnp.minimum(zz, 0.0) - jnp.log2(1.0 + jnp.exp2(-jnp.abs(zz))) for zz in z]
        lsn = [l - zz for l, zz in zip(lsz, z)]
        lg = [jnp.concatenate([jnp.where(before, l[:, n:], 0.0) if masked else l[:, n:],
                               l[:, :n]], axis=0) for l in lsn]
        hi = [l.astype(BF16) for l in lg]
        lo = [(l - h.astype(F32)).astype(BF16) for l, h in zip(lg, hi)]
        sums = [jnp.dot(jnp.concatenate([h, l], axis=1), later2, preferred_element_type=F32)
                for h, l in zip(hi, lo)]
        tot = [total(l) for l in lg]
        rest_mid = [r + t[:n] for r, t in zip(rests, tot)]
        tail = [jnp.concatenate([sm[n:] + rep(rm), sm[:n] + rep(r)], axis=1)
                for sm, rm, r in zip(sums, rest_mid, rests)]
        a = [jnp.exp2(l + t) for l, t in zip(lsz, tail)]
        if masked:
            a = [jnp.concatenate([aa[:, :n], jnp.where(before, aa[:, n:], 0.0)], axis=1) for aa in a]
        pv = [_dot(aa, v_ref[pl.ds(k_lo, 2 * n), :]) for aa, k_lo in zip(a, k_los)]
        return pv, [rm + t[n:] for rm, t in zip(rest_mid, tot)]

    def pair(q, k_lo, rest, masked):
        pv, rest = pairs([q], [k_lo], [rest], masked)
        return pv[0], rest[0]

    def live(rest):
        return (jnp.max(rest) > SB_CUTOFF).astype(jnp.int32)

    acc0, _ = block(q_ref[0:n, :], 0, jnp.zeros((n, LANES), F32), True)
    o_ref[0:n, :] = acc0.astype(BF16)

    def first(idx):
        qs = [q_ref[i * n:(i + 1) * n, :] for i in idx]
        accs, rests = pairs(qs, [(i - 1) * n for i in idx],
                            [jnp.zeros((n, LANES), F32)] * len(idx), True)
        return list(zip(qs, accs, rests, [live(r) for r in rests]))

    def finish(i, q, acc, rest, alive0):
        n_pairs = (i - 1) // 2

        def cond(c):
            return (c[0] < n_pairs) & (c[3] > 0)

        def body(c):
            t, acc, rest, _ = c
            k_lo = pl.multiple_of((i - 3 - 2 * t) * n, n)
            pv, rest = pair(q, k_lo, rest, False)
            return t + 1, acc + pv, rest, live(rest)

        alive = alive0
        if n_pairs > 0:
            _, acc, rest, alive = lax.while_loop(cond, body, (jnp.int32(0), acc, rest, alive0))
        if (i - 1) % 2 == 1:
            acc = lax.cond(alive > 0, lambda: acc + block(q, 0, rest, False)[0], lambda: acc)
        o_ref[i * n:(i + 1) * n, :] = acc.astype(BF16)

    blocks = list(range(1, s // n))
    for g0 in range(0, len(blocks), SB_QBLOCKS_PER_STEP):
        idx = blocks[g0:g0 + SB_QBLOCKS_PER_STEP]
        for i, start in zip(idx, first(idx)):
            finish(i, *start)


def _sb_call(qkv):
    b, s, _ = qkv.shape
    nh = N_SB_HEADS
    head = lambda off: pl.BlockSpec((None, s, HEAD_DIM), lambda bi, h: (bi, 0, off + h))
    return pl.pallas_call(
        _sb_kernel,
        grid=(b, nh),
        in_specs=[head(0), head(nh), head(2 * nh)],
        out_specs=head(0),
        out_shape=jax.ShapeDtypeStruct((b, s, nh * HEAD_DIM), BF16),
        compiler_params=pltpu.CompilerParams(
            dimension_semantics=("parallel", "parallel"), vmem_limit_bytes=VMEM_LIMIT),
        name="stickbreaking_attention",
    )(qkv, qkv, qkv)


EVEN_W_ROWS = 256
EVEN_W_COLS = 4 * D_DN + 4 * D_FOX + LANES


def _even_w_in_kernel(wt_ref, o_ref):
    dn_wide = 4 * D_DN
    gates = 2 * N_DN_HEADS
    fox_wide = 4 * D_FOX
    wt = wt_ref[...]
    o_ref[:, :dn_wide] = wt[:dn_wide].T.astype(BF16)
    o_ref[:, dn_wide:dn_wide + fox_wide] = (
        wt[dn_wide + gates:dn_wide + gates + fox_wide].T.astype(BF16))
    small = jnp.concatenate(
        [wt[dn_wide:dn_wide + gates], wt[dn_wide + gates + fox_wide:],
         jnp.zeros((LANES - gates - N_FOX_HEADS, wt.shape[1]), F32)], axis=0)
    o_ref[:, dn_wide + fox_wide:] = small.T.astype(BF16)


def _even_w_in(w):
    layers, rows, cols = w.shape
    return pl.pallas_call(
        _even_w_in_kernel,
        grid=(layers, rows // EVEN_W_ROWS),
        in_specs=[pl.BlockSpec((None, cols, EVEN_W_ROWS), lambda l, r: (l, 0, r))],
        out_specs=pl.BlockSpec((None, EVEN_W_ROWS, EVEN_W_COLS), lambda l, r: (l, r, 0)),
        out_shape=jax.ShapeDtypeStruct((layers, rows, EVEN_W_COLS), BF16),
        compiler_params=pltpu.CompilerParams(
            dimension_semantics=("parallel", "parallel"), vmem_limit_bytes=VMEM_LIMIT),
        name="reorder_w_in_even",
    )(jnp.swapaxes(w, 1, 2))


def kernel(x, norm_ffn1, ffn1_w_gu, ffn1_w_down, norm_mix, w_in_even, dn_conv_w, dn_a_log,
           dn_dt_bias, dn_norm_g, fox_q_norm_g, fox_k_norm_g, fox_f_bias, w_out_even,
           w_in_odd, w_out_odd, norm_ffn2, ffn2_w_gu, ffn2_w_down):
    b, s, d = x.shape
    depth = norm_ffn1.shape[0]
    t = b * s
    bf = lambda a: a.astype(BF16)
    wgu1, wd1, wgu2, wd2 = bf(ffn1_w_gu), bf(ffn1_w_down), bf(ffn2_w_gu), bf(ffn2_w_down)
    w_in_e, w_out_e = _even_w_in(w_in_even), bf(w_out_even)
    w_in_o, w_out_o = bf(w_in_odd), bf(w_out_odd)
    x = x.reshape(t, d)
    for l in range(depth):
        j = l // 2
        x = _ffn_call(x, [], norm_ffn1[l], wgu1, wd1, l)
        if l % 2 == 0:
            qkv, dgate, fq, fk, fv, fgate, small = _inproj_even_call(
                x, s, norm_mix[l], w_in_e, j, fox_q_norm_g[j], fox_k_norm_g[j], dn_conv_w[j])
            r3 = lambda a: a.reshape(b, s, a.shape[-1])
            o_dn = _deltanet_call(r3(qkv), r3(dgate), r3(small), dn_a_log[j], dn_dt_bias[j],
                                  dn_norm_g[j])
            o_fox = _fox_call(r3(fq), r3(fk), r3(fv), r3(small), fox_f_bias[j], r3(fgate))
            pre = [(o_dn.reshape(t, D_DN), w_out_e, j, 0), (o_fox.reshape(t, D_FOX), w_out_e, j, 1)]
        else:
            qkv = _inproj_odd_call(x, norm_mix[l], w_in_o, j)
            o = _sb_call(qkv.reshape(b, s, qkv.shape[-1]))
            pre = [(o.reshape(t, o.shape[-1]), w_out_o, j, 0)]
        x = _ffn_call(x, pre, norm_ffn2[l], wgu2, wd2, l)
    return x.reshape(b, s, d)
```
